```python
import math, functools
import jax, jax.numpy as jnp
from jax import lax
import numpy as np

D_MODEL = 2048
BATCH = 2
SEQ = 4096
DEPTH = 2
DEC_BATCH = 32
DEC_SEQ = 4
PAST_LEN = 8192
PAGE_SIZE = 128

H_A = 8
DK_A = 64
DV_A = 2 * DK_A
H_B = 8
DK_B = 64
DV_B = 2 * DK_B
RET_CHUNK = 128
C_CONV = D_MODEL // 2
K_CONV = 31
D_FF = 5632
K_FFN = 3
N_BUCKETS = 32
MAX_DIST = 128
Q_BLOCK = 128
EPS = 1e-6

A_Q = H_A * 2 * DK_A
A_COLS = H_A * (4 * DK_A + DV_A)
B_QK = H_B * DK_B
B_COLS = H_B * (2 * DK_B + 2 * DV_B)
C_COLS = 2 * C_CONV
G_COLS = 3 * D_MODEL
IN_COLS = A_COLS + B_COLS + C_COLS + G_COLS

kernel_name = 'hybrid_diffattn_retention_conformer_step'


def rms_norm(x, g):
    xf = x.astype(jnp.float32)
    y = xf * lax.rsqrt(jnp.mean(xf * xf, axis=-1, keepdims=True) + EPS)
    return (y * g.astype(jnp.float32)).astype(x.dtype)


def layer_norm(x, g, b):
    xf = x.astype(jnp.float32)
    mu = jnp.mean(xf, axis=-1, keepdims=True)
    var = jnp.mean(jnp.square(xf - mu), axis=-1, keepdims=True)
    y = (xf - mu) * lax.rsqrt(var + EPS) * g.astype(jnp.float32) + b.astype(jnp.float32)
    return y.astype(x.dtype)


def t5_bucket(dist):
    max_exact = N_BUCKETS // 2
    d = jnp.maximum(dist, 0)
    df = jnp.maximum(d, 1).astype(jnp.float32)
    large = max_exact + (jnp.log(df / max_exact) / math.log(MAX_DIST / max_exact)
                         * (N_BUCKETS - max_exact)).astype(jnp.int32)
    return jnp.where(d < max_exact, d, jnp.minimum(large, N_BUCKETS - 1))


def diff_logits(q, k, q_pos, k_pos, rel_bias):
    s = jnp.einsum('bqhcd,bkhcd->bhcqk', q, k).astype(jnp.float32) * (DK_A ** -0.5)
    dist = q_pos[:, None] - k_pos[None, :]
    bias = jnp.transpose(rel_bias[t5_bucket(dist)].astype(jnp.float32), (2, 0, 1))
    s = s + bias[None, :, None]
    return jnp.where((dist >= 0)[None, None, None], s, -1e30)


def diff_weights(s, lam):
    p = jax.nn.softmax(s, axis=-1)
    return p[:, :, 0] - lam * p[:, :, 1]


def prompt_attention(q, k, v, lam, rel_bias):
    B, S = q.shape[:2]
    nb = S // Q_BLOCK
    qb = jnp.swapaxes(q.reshape(B, nb, Q_BLOCK, H_A, 2, DK_A), 0, 1)
    k_pos = jnp.arange(S)

    def block(args):
        i, qi = args
        q_pos = i * Q_BLOCK + jnp.arange(Q_BLOCK)
        a = diff_weights(diff_logits(qi, k, q_pos, k_pos, rel_bias), lam)
        return jnp.einsum('bhqk,bkhd->bqhd', a.astype(v.dtype), v)

    o = lax.map(block, (jnp.arange(nb), qb))
    return jnp.swapaxes(o, 0, 1).reshape(B, S, H_A, DV_A)


def sample_attention(q, k_new, v_new, lam, cache_k_l, cache_v_l, page_table, rel_bias):
    DB, T = q.shape[:2]
    k_past = cache_k_l[page_table].reshape(DB, -1, H_A, 2, DK_A).astype(q.dtype)
    v_past = cache_v_l[page_table].reshape(DB, -1, H_A, DV_A).astype(v_new.dtype)
    P = k_past.shape[1]
    q_pos = P + jnp.arange(T)
    s = jnp.concatenate([diff_logits(q, k_past, q_pos, jnp.arange(P), rel_bias),
                         diff_logits(q, k_new, q_pos, P + jnp.arange(T), rel_bias)], axis=-1)
    a = diff_weights(s, lam).astype(v_new.dtype)
    return (jnp.einsum('bhqk,bkhd->bqhd', a[..., :P], v_past)
            + jnp.einsum('bhqk,bkhd->bqhd', a[..., P:], v_new))


def rotary(x, pos):
    half = x.shape[-1] // 2
    inv = 1.0 / (10000.0 ** (jnp.arange(half, dtype=jnp.float32) / half))
    ang = pos.astype(jnp.float32)[:, None] * inv[None, :]
    cos, sin = jnp.cos(ang)[:, None, :], jnp.sin(ang)[:, None, :]
    x1, x2 = jnp.split(x.astype(jnp.float32), 2, axis=-1)
    return jnp.concatenate([x1 * cos - x2 * sin, x1 * sin + x2 * cos], axis=-1).astype(x.dtype)


def retention_chunk(S0, qkv):
    q, k, v = qkv
    C = q.shape[1]
    log_g = jnp.log1p(-(2.0 ** (-5.0 - jnp.arange(H_B, dtype=jnp.float32))))
    i = jnp.arange(C, dtype=jnp.float32)
    diff = i[:, None] - i[None, :]
    D = jnp.where(diff >= 0, jnp.exp(jnp.maximum(diff, 0.0)[None] * log_g[:, None, None]), 0.0)
    qf, kf, vf = q.astype(jnp.float32), k.astype(jnp.float32), v.astype(jnp.float32)
    inner = jnp.einsum('bihd,bjhd->bhij', qf, kf) * D[None]
    o_inner = jnp.einsum('bhij,bjhe->bihe', inner, vf)
    dec_q = jnp.exp((i + 1.0)[:, None] * log_g[None, :])
    o_cross = jnp.einsum('bihd,bhde->bihe', qf, S0) * dec_q[None, :, :, None]
    dec_k = jnp.exp((C - 1.0 - i)[:, None] * log_g[None, :])
    S1 = (S0 * jnp.exp(C * log_g)[None, :, None, None]
          + jnp.einsum('bjhd,bjhe->bhde', kf * dec_k[None, :, :, None], vf))
    return S1, o_inner + o_cross


def retention(state0, q, k, v):
    B, S = q.shape[:2]
    C = RET_CHUNK if S % RET_CHUNK == 0 else S
    n = S // C

    def to_chunks(t):
        return jnp.swapaxes(t.reshape(B, n, C, *t.shape[2:]), 0, 1)

    state, o = lax.scan(retention_chunk, state0.astype(jnp.float32),
                        (to_chunks(q), to_chunks(k), to_chunks(v)))
    return jnp.swapaxes(o, 0, 1).reshape(B, S, H_B, DV_B).astype(v.dtype), state


def causal_dwconv(u, prev, w, b):
    K, Cc = w.shape
    T = u.shape[1]
    full = jnp.concatenate([prev.astype(u.dtype), u], axis=1)
    out = lax.conv_general_dilated(full, w.reshape(K, 1, Cc).astype(u.dtype), window_strides=(1,),
                                   padding='VALID', dimension_numbers=('NWC', 'WIO', 'NWC'),
                                   feature_group_count=Cc)
    return out + b.astype(u.dtype), full[:, T:]


def layer(x, pos, attn_fn, ret0, conv0, ffn0, lam_init, lw):
    B, S, _ = x.shape
    h = rms_norm(x, lw['norm1_g'])
    proj = h @ lw['w_in']
    pa, pb, pc, pg = jnp.split(proj, [A_COLS, A_COLS + B_COLS, A_COLS + B_COLS + C_COLS], axis=-1)
    qa, ka, va = jnp.split(pa, [A_Q, 2 * A_Q], axis=-1)
    qa = rms_norm(qa.reshape(B, S, H_A, 2, DK_A), lw['q_norm_g'])
    ka = rms_norm(ka.reshape(B, S, H_A, 2, DK_A), lw['k_norm_g'])
    va = va.reshape(B, S, H_A, DV_A)
    lv = lw['lam_vec'].astype(jnp.float32)
    lam = jnp.exp(jnp.sum(lv[0] * lv[1])) - jnp.exp(jnp.sum(lv[2] * lv[3])) + lam_init
    oa = attn_fn(qa, ka, va, lam)
    ya = (rms_norm(oa, lw['subln_g']) * (1.0 - lam_init)).reshape(B, S, H_A * DV_A) @ lw['w_o_a']
    qb, kb, vb, gb = jnp.split(pb, [B_QK, 2 * B_QK, 2 * B_QK + H_B * DV_B], axis=-1)
    qb = rotary(qb.reshape(B, S, H_B, DK_B), pos)
    kb = rotary(kb.reshape(B, S, H_B, DK_B), pos) * (DK_B ** -0.5)
    ob, ret_new = retention(ret0, qb, kb, vb.reshape(B, S, H_B, DV_B))
    yb = ((rms_norm(ob, lw['ret_norm_g']) * jax.nn.silu(gb).reshape(B, S, H_B, DV_B))
          .reshape(B, S, H_B * DV_B) @ lw['w_o_b'])
    u = pc[..., :C_CONV] * jax.nn.sigmoid(pc[..., C_CONV:])
    uc, conv_new = causal_dwconv(u, conv0, lw['conv_w'], lw['conv_b'])
    yc = jax.nn.silu(layer_norm(uc, lw['conv_ln_g'], lw['conv_ln_b'])) @ lw['w_o_c']
    ga, gbr, gc = jnp.split(jax.nn.sigmoid(pg + lw['b_gate']), 3, axis=-1)
    x = x + (ga * ya + gbr * yb + gc * yc) @ lw['w_out']
    h2 = rms_norm(x, lw['norm2_g'])
    g, up = jnp.split(h2 @ lw['w_ffn_in'], 2, axis=-1)
    gconv, ffn_new = causal_dwconv(g, ffn0, lw['ffn_conv_w'], lw['ffn_conv_b'])
    x = x + (jax.nn.silu(gconv) * up) @ lw['w_ffn_down']
    return x, (ka.reshape(B, S, H_A, 2 * DK_A), va, ret_new, conv_new, ffn_new)


def setup_inputs(seed: int = 0) -> dict:
    key = jax.random.key(seed)
    ks = iter(jax.random.split(key, 40))
    f32 = jnp.float32
    n_pages = PAST_LEN // PAGE_SIZE
    n_pool = (DEC_BATCH * n_pages * 5) // 4

    def nrm(shape, scale):
        return jax.random.normal(next(ks), shape, f32) * scale

    def gain(shape):
        return 1.0 + nrm(shape, 0.02)

    x_prompt = nrm((BATCH, SEQ, D_MODEL), 1.0)
    x_sample = nrm((DEC_BATCH, DEC_SEQ, D_MODEL), 1.0)
    cache_k = nrm((DEPTH, n_pool, PAGE_SIZE, H_A, 2 * DK_A), 1.0)
    cache_v = nrm((DEPTH, n_pool, PAGE_SIZE, H_A, DV_A), 1.0)
    state_ret = nrm((DEPTH, DEC_BATCH, H_B, DK_B, DV_B), 0.5)
    state_conv = nrm((DEPTH, DEC_BATCH, K_CONV - 1, C_CONV), 0.5)
    state_ffn = nrm((DEPTH, DEC_BATCH, K_FFN - 1, D_FF), 1.0)
    perm = jax.random.permutation(next(ks), n_pool)
    page_table = perm[:DEC_BATCH * n_pages].reshape(DEC_BATCH, n_pages).astype(jnp.int32)
    return {
        'x_prompt': x_prompt, 'x_sample': x_sample,
        'cache_k': cache_k, 'cache_v': cache_v, 'state_ret': state_ret,
        'state_conv': state_conv, 'state_ffn': state_ffn, 'page_table': page_table,
        'rel_bias': nrm((N_BUCKETS, H_A), 0.5),
        'norm1_g': gain((DEPTH, D_MODEL)),
        'w_in': nrm((DEPTH, D_MODEL, IN_COLS), D_MODEL ** -0.5),
        'b_gate': nrm((DEPTH, G_COLS), 0.02),
        'q_norm_g': gain((DEPTH, DK_A)),
        'k_norm_g': gain((DEPTH, DK_A)),
        'lam_vec': nrm((DEPTH, 4, DK_A), 0.1),
        'subln_g': gain((DEPTH, DV_A)),
        'w_o_a': nrm((DEPTH, H_A * DV_A, D_MODEL), (H_A * DV_A) ** -0.5),
        'ret_norm_g': gain((DEPTH, DV_B)),
        'w_o_b': nrm((DEPTH, H_B * DV_B, D_MODEL), (H_B * DV_B) ** -0.5),
        'conv_w': nrm((DEPTH, K_CONV, C_CONV), K_CONV ** -0.5),
        'conv_b': nrm((DEPTH, C_CONV), 0.02),
        'conv_ln_g': gain((DEPTH, C_CONV)),
        'conv_ln_b': nrm((DEPTH, C_CONV), 0.02),
        'w_o_c': nrm((DEPTH, C_CONV, D_MODEL), C_CONV ** -0.5),
        'w_out': nrm((DEPTH, D_MODEL, D_MODEL), D_MODEL ** -0.5),
        'norm2_g': gain((DEPTH, D_MODEL)),
        'w_ffn_in': nrm((DEPTH, D_MODEL, 2 * D_FF), D_MODEL ** -0.5),
        'ffn_conv_w': nrm((DEPTH, K_FFN, D_FF), K_FFN ** -0.5),
        'ffn_conv_b': nrm((DEPTH, D_FF), 0.02),
        'w_ffn_down': nrm((DEPTH, D_FF, D_MODEL), D_FF ** -0.5),
    }


def reference(x_prompt, x_sample, cache_k, cache_v, state_ret, state_conv, state_ffn, page_table,
              rel_bias, norm1_g, w_in, b_gate, q_norm_g, k_norm_g, lam_vec, subln_g, w_o_a,
              ret_norm_g, w_o_b, conv_w, conv_b, conv_ln_g, conv_ln_b, w_o_c, w_out, norm2_g,
              w_ffn_in, ffn_conv_w, ffn_conv_b, w_ffn_down):
    B, S = x_prompt.shape[:2]
    past = page_table.shape[1] * cache_k.shape[2]
    pos_p = jnp.arange(S)
    pos_s = past + jnp.arange(x_sample.shape[1])
    prompt_attn = functools.partial(prompt_attention, rel_bias=rel_bias)
    yp, ys = x_prompt, x_sample
    sp, ss = [], []
    for l in range(DEPTH):
        lw = {'norm1_g': norm1_g[l], 'w_in': w_in[l], 'b_gate': b_gate[l],
              'q_norm_g': q_norm_g[l], 'k_norm_g': k_norm_g[l], 'lam_vec': lam_vec[l],
              'subln_g': subln_g[l], 'w_o_a': w_o_a[l], 'ret_norm_g': ret_norm_g[l],
              'w_o_b': w_o_b[l], 'conv_w': conv_w[l], 'conv_b': conv_b[l],
              'conv_ln_g': conv_ln_g[l], 'conv_ln_b': conv_ln_b[l], 'w_o_c': w_o_c[l],
              'w_out': w_out[l], 'norm2_g': norm2_g[l], 'w_ffn_in': w_ffn_in[l],
              'ffn_conv_w': ffn_conv_w[l], 'ffn_conv_b': ffn_conv_b[l], 'w_ffn_down': w_ffn_down[l]}
        lam_init = 0.8 - 0.6 * math.exp(-0.3 * l)
        yp, st = layer(yp, pos_p, prompt_attn,
                       jnp.zeros((B, H_B, DK_B, DV_B), jnp.float32),
                       jnp.zeros((B, K_CONV - 1, C_CONV), x_prompt.dtype),
                       jnp.zeros((B, K_FFN - 1, D_FF), x_prompt.dtype), lam_init, lw)
        sp.append(st)
        sample_attn = functools.partial(sample_attention, cache_k_l=cache_k[l], cache_v_l=cache_v[l],
                                        page_table=page_table, rel_bias=rel_bias)
        ys, st = layer(ys, pos_s, sample_attn, state_ret[l], state_conv[l], state_ffn[l], lam_init, lw)
        ss.append(st)

    def stk(sts, i):
        return jnp.stack([st[i] for st in sts], axis=0)

    return (yp, ys, stk(sp, 0), stk(sp, 1), stk(sp, 2), stk(sp, 3), stk(sp, 4),
            stk(ss, 0), stk(ss, 1), stk(ss, 2), stk(ss, 3), stk(ss, 4))
```

```python
import functools
import math

import numpy as np
import jax
import jax.numpy as jnp
from jax import lax
from jax.experimental import pallas as pl
from jax.experimental.pallas import tpu as pltpu

F32 = jnp.float32
BF16 = jnp.bfloat16

H_A = 8
DK_A = 64
DV_A = 128
H_B = 8
DK_B = 64
DV_B = 128
K_CONV = 31
K_FFN = 3
N_BUCKETS = 32
MAX_DIST = 128
EPS = 1e-6
NEG = -1e30

LANES = 128
SUBLANES = 8
VMEM_LIMIT = 56 * 1024 * 1024

_NT = (((1,), (1,)), ((), ()))
_TN = (((0,), (0,)), ((), ()))


def _cparams(sem):
    return pltpu.CompilerParams(dimension_semantics=sem, vmem_limit_bytes=VMEM_LIMIT)


def _pick(n, cands):
    for c in cands:
        if n % c == 0:
            return c
    return n


def _t5_bucket_np(d):
    max_exact = N_BUCKETS // 2
    d = np.maximum(d, 0)
    df = np.maximum(d, 1).astype(np.float32)
    large = max_exact + (np.log(df / np.float32(max_exact)) / np.float32(math.log(MAX_DIST / max_exact))
                         * np.float32(N_BUCKETS - max_exact)).astype(np.int32)
    return np.where(d < max_exact, d, np.minimum(large, N_BUCKETS - 1)).astype(np.int32)


def _bias_from_buckets(idx, rb_ref, h):
    acc = jnp.zeros(idx.shape, F32)
    for b in range(N_BUCKETS):
        acc = jnp.where(idx == b, rb_ref[b, h], acc)
    return acc


def _lam(lv_ref, lam_init):
    lv = lv_ref[...]
    a = jnp.sum(lv[0:1] * lv[1:2], axis=-1, keepdims=True)
    b = jnp.sum(lv[2:3] * lv[3:4], axis=-1, keepdims=True)
    return jnp.exp(a) - jnp.exp(b) + lam_init


def _silu(x):
    return x * jax.nn.sigmoid(x)


def _norm_mm_body(x_ref, g_ref, w_ref, o_ref, h_ref):
    @pl.when(pl.program_id(1) == 0)
    def _():
        x = x_ref[...]
        ms = jnp.mean(x * x, axis=-1, keepdims=True)
        h_ref[...] = ((x * lax.rsqrt(ms + EPS)) * g_ref[...]).astype(BF16)

    o_ref[...] = jnp.dot(h_ref[...], w_ref[...], preferred_element_type=F32)


def norm_matmul(x, g, w):
    n, d = x.shape
    c = w.shape[1]
    tm = _pick(n, (1024, 512, 256, 128))
    tn = _pick(c, (1024, 512, 256, 128))
    return pl.pallas_call(
        _norm_mm_body,
        grid=(n // tm, c // tn),
        in_specs=[pl.BlockSpec((tm, d), lambda i, j: (i, 0)),
                  pl.BlockSpec((1, d), lambda i, j: (0, 0)),
                  pl.BlockSpec((d, tn), lambda i, j: (0, j))],
        out_specs=pl.BlockSpec((tm, tn), lambda i, j: (i, j)),
        out_shape=jax.ShapeDtypeStruct((n, c), F32),
        scratch_shapes=[pltpu.VMEM((tm, d), BF16)],
        compiler_params=_cparams(("parallel", "arbitrary")),
        name="norm_matmul",
    )(x, g.reshape(1, d), w)


def _mm_res_body(a_ref, w_ref, r_ref, o_ref, acc_ref, *, nk):
    k = pl.program_id(2)
    part = jnp.dot(a_ref[...], w_ref[...], preferred_element_type=F32)
    if nk == 1:
        o_ref[...] = r_ref[...] + part
    else:
        @pl.when(k == 0)
        def _():
            acc_ref[...] = part

        @pl.when(jnp.logical_and(k > 0, k < nk - 1))
        def _():
            acc_ref[...] += part

        @pl.when(k == nk - 1)
        def _():
            o_ref[...] = r_ref[...] + (acc_ref[...] + part)


def matmul_residual(a, w, res):
    n, kd = a.shape
    c = w.shape[1]
    tm = _pick(n, (1024, 512, 256, 128))
    tn = _pick(c, (1024, 512, 256, 128))
    tk = kd if kd <= 2048 else _pick(kd, (2816, 2048, 1024, 512))
    nk = kd // tk
    return pl.pallas_call(
        functools.partial(_mm_res_body, nk=nk),
        grid=(n // tm, c // tn, nk),
        in_specs=[pl.BlockSpec((tm, tk), lambda i, j, k: (i, k)),
                  pl.BlockSpec((tk, tn), lambda i, j, k: (k, j)),
                  pl.BlockSpec((tm, tn), lambda i, j, k: (i, j))],
        out_specs=pl.BlockSpec((tm, tn), lambda i, j, k: (i, j)),
        out_shape=jax.ShapeDtypeStruct((n, c), F32),
        scratch_shapes=[pltpu.VMEM((tm, tn), F32)],
        compiler_params=_cparams(("parallel", "parallel", "arbitrary")),
        name="matmul_residual",
    )(a, w, res)


def _group_rms(x, gain, gm):
    s = x * x
    hi = s.astype(BF16)
    lo = (s - hi.astype(F32)).astype(BF16)
    parts = []
    for c in range(x.shape[1] // LANES):
        sl = slice(c * LANES, (c + 1) * LANES)
        parts.append(jnp.dot(hi[:, sl], gm, preferred_element_type=F32)
                     + jnp.dot(lo[:, sl], gm, preferred_element_type=F32))
    ms = jnp.concatenate(parts, axis=-1) * (1.0 / DK_A)
    return (x * lax.rsqrt(ms + EPS)) * gain


def _attn_prep_body(q_ref, k_ref, v_ref, qg_ref, kg_ref, gm_ref, qo_ref, kf_ref, kb_ref, vf_ref, vb_ref):
    gm = gm_ref[...]
    qn = _group_rms(q_ref[...], qg_ref[...], gm)
    qo_ref[...] = qn * (DK_A ** -0.5)
    kn = _group_rms(k_ref[...], kg_ref[...], gm)
    kf_ref[...] = kn
    kb_ref[...] = kn.astype(BF16)
    v = v_ref[...]
    vf_ref[...] = v
    vb_ref[...] = v.astype(BF16)


def attn_prep(proj, q_gain, k_gain):
    n = proj.shape[0]
    w = H_A * 2 * DK_A
    tm = _pick(n, (512, 256, 128))
    gm = np.kron(np.eye(LANES // DK_A, dtype=np.float32), np.ones((DK_A, DK_A), np.float32))
    gm = jnp.asarray(gm, BF16)
    qg = jnp.tile(q_gain, w // DK_A).reshape(1, w)
    kg = jnp.tile(k_gain, w // DK_A).reshape(1, w)
    col = lambda c: pl.BlockSpec((tm, w), lambda i, c=c: (i, c))
    cst = lambda shape: pl.BlockSpec(shape, lambda i: (0, 0))
    out = lambda: pl.BlockSpec((tm, w), lambda i: (i, 0))
    return pl.pallas_call(
        _attn_prep_body,
        grid=(n // tm,),
        in_specs=[col(0), col(1), col(2), cst((1, w)), cst((1, w)), cst((LANES, LANES))],
        out_specs=[out(), out(), out(), out(), out()],
        out_shape=[jax.ShapeDtypeStruct((n, w), F32), jax.ShapeDtypeStruct((n, w), F32),
                   jax.ShapeDtypeStruct((n, w), BF16), jax.ShapeDtypeStruct((n, w), F32),
                   jax.ShapeDtypeStruct((n, w), BF16)],
        compiler_params=_cparams(("parallel",)),
        name="attn_prep",
    )(proj, proj, proj, qg, kg, gm)


def _softmax_tile(s, vt, m_ref, l_ref, acc_ref):
    m_old = m_ref[...]
    m_new = jnp.maximum(m_old, jnp.max(s, axis=-1, keepdims=True))
    alpha = jnp.exp(m_old - m_new)
    p = jnp.exp(s - m_new)
    l_ref[...] = alpha * l_ref[...] + jnp.sum(p, axis=-1, keepdims=True)
    acc_ref[...] = alpha * acc_ref[...] + jnp.dot(p.astype(BF16), vt, preferred_element_type=F32)
    m_ref[...] = m_new


def _prompt_attn_body(rb_ref, ib_ref, lv_ref, sg_ref, q_ref, k_ref, v_ref, o_ref,
                      bias_ref, m_ref, l_ref, acc_ref, *, t, lam_init):
    h = pl.program_id(1)
    qi = pl.program_id(2)

    @pl.when(qi == 0)
    def _():
        for o in range(2):
            tile = _bias_from_buckets(ib_ref[o], rb_ref, h)
            bias_ref[o, 0:t, :] = tile
            bias_ref[o, t:2 * t, :] = tile

    q = q_ref[0]
    lane = lax.broadcasted_iota(jnp.int32, q.shape, 1)
    q2 = jnp.concatenate([jnp.where(lane < DK_A, q, 0.0), jnp.where(lane >= DK_A, q, 0.0)],
                         axis=0).astype(BF16)

    m_ref[...] = jnp.full(m_ref.shape, NEG, F32)
    l_ref[...] = jnp.zeros(l_ref.shape, F32)
    acc_ref[...] = jnp.zeros(acc_ref.shape, F32)

    def logits(kj):
        off = pl.multiple_of(kj * t, t)
        kt = k_ref[0, pl.ds(off, t), :]
        vt = v_ref[0, pl.ds(off, t), :]
        return lax.dot_general(q2, kt, _NT, preferred_element_type=F32), vt

    far_bias = rb_ref[N_BUCKETS - 1, h]

    def far(kj, carry):
        s, vt = logits(kj)
        _softmax_tile(s + far_bias, vt, m_ref, l_ref, acc_ref)
        return carry

    lax.fori_loop(0, jnp.maximum(qi - 1, 0), far, 0)

    @pl.when(qi >= 1)
    def _():
        s, vt = logits(qi - 1)
        _softmax_tile(s + bias_ref[1], vt, m_ref, l_ref, acc_ref)

    s, vt = logits(qi)
    row = lax.broadcasted_iota(jnp.int32, (2 * t, t), 0)
    col = lax.broadcasted_iota(jnp.int32, (2 * t, t), 1)
    causal = jnp.where(row >= t, row - t, row) >= col
    _softmax_tile(jnp.where(causal, s + bias_ref[0], NEG), vt, m_ref, l_ref, acc_ref)

    o2 = acc_ref[...] / l_ref[...]
    o = o2[0:t] - _lam(lv_ref, lam_init) * o2[t:2 * t]
    ms = jnp.mean(o * o, axis=-1, keepdims=True)
    y = ((o * lax.rsqrt(ms + EPS)) * sg_ref[...]) * (1.0 - lam_init)
    o_ref[0] = y.astype(BF16)


def prompt_attn(qn, kb, vb, rel_bias, lam_vec, subln_g, lam_init, batch, seq):
    w = H_A * DV_A
    t = _pick(seq, (256, 128))
    assert t >= MAX_DIST and seq % t == 0
    nq = seq // t
    r = np.arange(t)
    d0 = r[:, None] - r[None, :]
    ib = jnp.asarray(np.stack([_t5_bucket_np(d0), _t5_bucket_np(d0 + t)]), jnp.int32)
    assert int(_t5_bucket_np(np.array([t + 1]))[0]) == N_BUCKETS - 1
    q3 = qn.reshape(batch, seq, w)
    k3 = kb.reshape(batch, seq, w)
    v3 = vb.reshape(batch, seq, w)
    out = pl.pallas_call(
        functools.partial(_prompt_attn_body, t=t, lam_init=lam_init),
        grid=(batch, H_A, nq),
        in_specs=[pl.BlockSpec(memory_space=pltpu.SMEM),
                  pl.BlockSpec((2, t, t), lambda b, h, i: (0, 0, 0)),
                  pl.BlockSpec((4, DK_A), lambda b, h, i: (0, 0)),
                  pl.BlockSpec((1, DV_A), lambda b, h, i: (0, 0)),
                  pl.BlockSpec((1, t, LANES), lambda b, h, i: (b, i, h)),
                  pl.BlockSpec((1, seq, LANES), lambda b, h, i: (b, 0, h)),
                  pl.BlockSpec((1, seq, LANES), lambda b, h, i: (b, 0, h))],
        out_specs=pl.BlockSpec((1, t, LANES), lambda b, h, i: (b, i, h)),
        out_shape=jax.ShapeDtypeStruct((batch, seq, w), BF16),
        scratch_shapes=[pltpu.VMEM((2, 2 * t, t), F32),
                        pltpu.VMEM((2 * t, 1), F32),
                        pltpu.VMEM((2 * t, 1), F32),
                        pltpu.VMEM((2 * t, DV_A), F32)],
        compiler_params=_cparams(("arbitrary", "arbitrary", "arbitrary")),
        name="prompt_attn",
    )(rel_bias, ib, lam_vec, subln_g.reshape(1, DV_A), q3, k3, v3)
    return out.reshape(batch * seq, w)


_TQ = SUBLANES
_RH = 2 * _TQ


def _sample_attn_body(pt_ref, rb_ref, ib_ref, lv_ref, sg_ref, q_ref, kn_ref, vn_ref, *rest,
                      pps, nsteps, t_new, lam_init):
    k_refs = rest[0:pps]
    v_refs = rest[pps:2 * pps]
    o_ref = rest[2 * pps]
    bias_ref, m_ref, l_ref, acc_ref = rest[2 * pps + 1:]
    b = pl.program_id(0)
    s_id = pl.program_id(1)
    rows = H_A * _RH
    page = k_refs[0].shape[2]

    @pl.when(jnp.logical_and(b == 0, s_id == 0))
    def _():
        for h in range(H_A):
            sl = slice(h * _RH, (h + 1) * _RH)
            bias_ref[0, sl, :] = jnp.full((_RH, page), rb_ref[N_BUCKETS - 1, h], F32)
            bias_ref[1, sl, :] = _bias_from_buckets(ib_ref[0, sl, :], rb_ref, h)
            bias_ref[2, sl, :] = _bias_from_buckets(ib_ref[1, sl, :], rb_ref, h)

    @pl.when(s_id == 0)
    def _():
        m_ref[...] = jnp.full(m_ref.shape, NEG, F32)
        l_ref[...] = jnp.zeros(l_ref.shape, F32)
        acc_ref[...] = jnp.zeros(acc_ref.shape, F32)

    q = q_ref[0]
    lane = lax.broadcasted_iota(jnp.int32, q.shape, 1)
    first = (lane % LANES) < DK_A
    q2 = jnp.concatenate([jnp.where(first, q, 0.0), jnp.where(first, 0.0, q)], axis=0).astype(BF16)

    def attend(k_tiles, v_tiles, bias_tiles, mask):
        s_heads = []
        for h in range(H_A):
            hs = slice(h * LANES, (h + 1) * LANES)
            kh = jnp.concatenate([kt[:, hs] for kt in k_tiles], axis=0) if len(k_tiles) > 1 else k_tiles[0][:, hs]
            s_heads.append(lax.dot_general(q2[:, hs], kh, _NT, preferred_element_type=F32))
        s = jnp.concatenate(s_heads, axis=0)
        bias = jnp.concatenate(bias_tiles, axis=1) if len(bias_tiles) > 1 else bias_tiles[0]
        s = s + bias
        if mask is not None:
            s = jnp.where(mask, s, NEG)
        m_old = m_ref[...]
        m_new = jnp.maximum(m_old, jnp.max(s, axis=-1, keepdims=True))
        alpha = jnp.exp(m_old - m_new)
        p = jnp.exp(s - m_new)
        l_ref[...] = alpha * l_ref[...] + jnp.sum(p, axis=-1, keepdims=True)
        p = p.astype(BF16)
        pv = []
        for h in range(H_A):
            hs = slice(h * LANES, (h + 1) * LANES)
            vh = jnp.concatenate([vt[:, hs] for vt in v_tiles], axis=0) if len(v_tiles) > 1 else v_tiles[0][:, hs]
            pv.append(jnp.dot(p[h * _RH:(h + 1) * _RH, :], vh, preferred_element_type=F32))
        acc_ref[...] = alpha * acc_ref[...] + jnp.concatenate(pv, axis=0)
        m_ref[...] = m_new

    k_tiles = [r[0, 0].astype(BF16) for r in k_refs]
    v_tiles = [r[0, 0].astype(BF16) for r in v_refs]
    last = s_id == nsteps - 1
    bias_tiles = [bias_ref[0]] * (pps - 1) + [bias_ref[jnp.where(last, 1, 0)]]
    attend(k_tiles, v_tiles, bias_tiles, None)

    @pl.when(last)
    def _():
        zpad = jnp.zeros((page - _TQ, kn_ref.shape[2]), F32)
        kn = jnp.concatenate([kn_ref[0], zpad], axis=0).astype(BF16)
        vn = jnp.concatenate([vn_ref[0], zpad], axis=0).astype(BF16)
        attend([kn], [vn], [bias_ref[2]], ib_ref[1] >= 0)

        o2 = acc_ref[...] / l_ref[...]
        lam = _lam(lv_ref, lam_init)
        for h in range(H_A):
            o = o2[h * _RH:h * _RH + _TQ] - lam * o2[h * _RH + _TQ:(h + 1) * _RH]
            ms = jnp.mean(o * o, axis=-1, keepdims=True)
            y = ((o * lax.rsqrt(ms + EPS)) * sg_ref[...]) * (1.0 - lam_init)
            o_ref[0, :, h * LANES:(h + 1) * LANES] = y


def sample_attn(qn, kf, vf, cache_k, cache_v, layer, page_table, rel_bias, lam_vec, subln_g, lam_init, db, t_new):
    w = H_A * DV_A
    depth, n_pool, page = cache_k.shape[:3]
    n_pages = page_table.shape[1]
    past = n_pages * page
    assert t_new <= _TQ and page >= MAX_DIST and page % LANES == 0
    pps = _pick(n_pages, (8, 4, 2, 1))
    nsteps = n_pages // pps
    rows = H_A * _RH
    tq = np.minimum(np.arange(rows) % _TQ, t_new - 1)
    key = np.arange(page)
    ib_last = _t5_bucket_np(page + tq[:, None] - key[None, :])
    d_new = tq[:, None] - key[None, :]
    ib_new = np.where((d_new >= 0) & (key[None, :] < t_new), _t5_bucket_np(d_new), -1)
    ib = jnp.asarray(np.stack([ib_last, ib_new]), jnp.int32)
    assert int(_t5_bucket_np(np.array([page + 1]))[0]) == N_BUCKETS - 1

    def pad_rows(x):
        return jnp.pad(x.reshape(db, t_new, w), ((0, 0), (0, _TQ - t_new), (0, 0)))

    ck = cache_k.reshape(depth, n_pool, page, w)
    cv = cache_v.reshape(depth, n_pool, page, w)
    seq_spec = pl.BlockSpec((1, _TQ, w), lambda b, s, pt: (b, 0, 0))

    def page_spec(i):
        return pl.BlockSpec((1, 1, page, w), lambda b, s, pt, i=i: (layer, pt[b, s * pps + i], 0, 0))

    grid_spec = pltpu.PrefetchScalarGridSpec(
        num_scalar_prefetch=1,
        grid=(db, nsteps),
        in_specs=[pl.BlockSpec(memory_space=pltpu.SMEM),
                  pl.BlockSpec((2, rows, page), lambda b, s, pt: (0, 0, 0)),
                  pl.BlockSpec((4, DK_A), lambda b, s, pt: (0, 0)),
                  pl.BlockSpec((1, DV_A), lambda b, s, pt: (0, 0)),
                  seq_spec, seq_spec, seq_spec]
                 + [page_spec(i) for i in range(pps)] + [page_spec(i) for i in range(pps)],
        out_specs=pl.BlockSpec((1, _TQ, w), lambda b, s, pt: (b, 0, 0)),
        scratch_shapes=[pltpu.VMEM((3, rows, page), F32),
                        pltpu.VMEM((rows, 1), F32),
                        pltpu.VMEM((rows, 1), F32),
                        pltpu.VMEM((rows, DV_A), F32)],
    )
    out = pl.pallas_call(
        functools.partial(_sample_attn_body, pps=pps, nsteps=nsteps, t_new=t_new, lam_init=lam_init),
        grid_spec=grid_spec,
        out_shape=jax.ShapeDtypeStruct((db, _TQ, w), F32),
        compiler_params=_cparams(("arbitrary", "arbitrary")),
        name="sample_attn",
    )(page_table, rel_bias, ib, lam_vec, subln_g.reshape(1, DV_A), pad_rows(qn), pad_rows(kf), pad_rows(vf),
      *([ck] * pps), *([cv] * pps))
    return out[:, :t_new].reshape(db * t_new, w).astype(BF16)


def _log_gamma(h):
    return float(np.log1p(-np.exp2(np.float32(-5.0 - h), dtype=np.float32), dtype=np.float32))


def _swap_halves(x):
    n = x.shape[1]
    half = DK_B // 2
    lane = lax.broadcasted_iota(jnp.int32, x.shape, 1)
    return jnp.where((lane % DK_B) < half, pltpu.roll(x, n - half, 1), pltpu.roll(x, half, 1))


def _rotary(x, cos, sin):
    return x * cos + _swap_halves(x) * sin


def _ret_tile(q, k, v, c):
    r = q.shape[0]
    shift = int(math.log2(c))
    assert 1 << shift == c
    ri = lax.broadcasted_iota(jnp.int32, (r, r), 0)
    ci = lax.broadcasted_iota(jnp.int32, (r, r), 1)
    same = (ri >> shift) == (ci >> shift)
    diff = ((ri & (c - 1)) - (ci & (c - 1))).astype(F32)
    valid = jnp.logical_and(same, diff >= 0.0)
    ti = (lax.broadcasted_iota(jnp.int32, (r, 1), 0) & (c - 1)).astype(F32)
    lane = lax.broadcasted_iota(jnp.int32, (r, LANES), 1)
    first = lane < DK_B
    o_inner, q_m, dec_q, k_dec = [], [], [], []
    for p in range(H_B // 2):
        ps = slice(p * LANES, (p + 1) * LANES)
        qp, kp = q[:, ps], k[:, ps]
        kpb = kp.astype(BF16)
        lg0, lg1 = _log_gamma(2 * p), _log_gamma(2 * p + 1)
        k_dec.append((kp * jnp.where(first, jnp.exp((c - 1.0 - ti) * lg0), jnp.exp((c - 1.0 - ti) * lg1))).astype(BF16))
        for hh in range(2):
            h = 2 * p + hh
            lg = lg1 if hh else lg0
            qm = jnp.where(first if hh == 0 else jnp.logical_not(first), qp, 0.0).astype(BF16)
            dmat = jnp.where(valid, jnp.exp(jnp.maximum(diff, 0.0) * lg), 0.0)
            inner = lax.dot_general(qm, kpb, _NT, preferred_element_type=F32) * dmat
            vh = v[:, h * DV_B:(h + 1) * DV_B].astype(BF16)
            o_inner.append(jnp.dot(inner.astype(BF16), vh, preferred_element_type=F32))
            q_m.append(qm)
            dec_q.append(jnp.exp((ti + 1.0) * lg))
    return o_inner, q_m, dec_q, k_dec


def _state_update(s_pair, kd, v, p, c):
    row = lax.broadcasted_iota(jnp.int32, (LANES, DV_B), 0)
    top = row < DK_B
    u0 = lax.dot_general(kd, v[:, (2 * p) * DV_B:(2 * p + 1) * DV_B].astype(BF16), _TN, preferred_element_type=F32)
    u1 = lax.dot_general(kd, v[:, (2 * p + 1) * DV_B:(2 * p + 2) * DV_B].astype(BF16), _TN, preferred_element_type=F32)
    gc = jnp.where(top, math.exp(c * _log_gamma(2 * p)), math.exp(c * _log_gamma(2 * p + 1)))
    return s_pair * gc + jnp.where(top, u0, u1)


def _ret_epilogue(o, g, ng):
    ms = jnp.mean(o * o, axis=-1, keepdims=True)
    return ((o * lax.rsqrt(ms + EPS)) * ng) * _silu(g)


def _ret_prompt_body(q_ref, k_ref, v_ref, g_ref, cos_ref, sin_ref, ng_ref, y_ref, so_ref, s_ref, *, c):
    ci = pl.program_id(1)

    @pl.when(ci == 0)
    def _():
        s_ref[...] = jnp.zeros(s_ref.shape, F32)

    cos, sin = cos_ref[...], sin_ref[...]
    q = _rotary(q_ref[...], cos, sin)
    k = _rotary(k_ref[...], cos, sin) * (DK_B ** -0.5)
    v = v_ref[...]
    o_inner, q_m, dec_q, k_dec = _ret_tile(q, k, v, c)
    ng = ng_ref[...]
    for p in range(H_B // 2):
        s_pair = s_ref[p]
        sb = s_pair.astype(BF16)
        for hh in range(2):
            h = 2 * p + hh
            o = o_inner[h] + jnp.dot(q_m[h], sb, preferred_element_type=F32) * dec_q[h]
            hs = slice(h * DV_B, (h + 1) * DV_B)
            y_ref[:, hs] = _ret_epilogue(o, g_ref[:, hs], ng).astype(BF16)
        s_ref[p] = _state_update(s_pair, k_dec[p], v, p, c)

    @pl.when(ci == pl.num_programs(1) - 1)
    def _():
        so_ref[0] = s_ref[...]


def retention_prompt(proj, cos, sin, ret_norm_g, batch, seq, col0):
    n = batch * seq
    c = 128 if seq % 128 == 0 else seq
    nc = seq // c
    wq = H_B * DK_B
    wv = H_B * DV_B
    assert col0 % wv == 0
    qc, vc = col0 // wq, col0 // wv
    row = lambda b, i: b * nc + i
    y, st = pl.pallas_call(
        functools.partial(_ret_prompt_body, c=c),
        grid=(batch, nc),
        in_specs=[pl.BlockSpec((c, wq), lambda b, i: (row(b, i), qc)),
                  pl.BlockSpec((c, wq), lambda b, i: (row(b, i), qc + 1)),
                  pl.BlockSpec((c, wv), lambda b, i: (row(b, i), vc + 1)),
                  pl.BlockSpec((c, wv), lambda b, i: (row(b, i), vc + 2)),
                  pl.BlockSpec((c, wq), lambda b, i: (i, 0)),
                  pl.BlockSpec((c, wq), lambda b, i: (i, 0)),
                  pl.BlockSpec((1, DV_B), lambda b, i: (0, 0))],
        out_specs=[pl.BlockSpec((c, wv), lambda b, i: (row(b, i), 0)),
                   pl.BlockSpec((1, H_B // 2, LANES, DV_B), lambda b, i: (b, 0, 0, 0))],
        out_shape=[jax.ShapeDtypeStruct((n, wv), BF16),
                   jax.ShapeDtypeStruct((batch, H_B // 2, LANES, DV_B), F32)],
        scratch_shapes=[pltpu.VMEM((H_B // 2, LANES, DV_B), F32)],
        compiler_params=_cparams(("arbitrary", "arbitrary")),
        name="retention_prompt",
    )(proj, proj, proj, proj, cos, sin, ret_norm_g.reshape(1, DV_B))
    return y, st.reshape(batch, H_B, DK_B, DV_B)


def _ret_sample_body(q_ref, k_ref, v_ref, g_ref, cos_ref, sin_ref, ng_ref, s0_ref, y_ref, so_ref,
                     o_ref, qm_ref, kd_ref, *, c):
    gi = pl.program_id(0)
    r = q_ref.shape[0]
    shift = int(math.log2(c))
    v = v_ref[...]

    @pl.when(gi == 0)
    def _():
        cos, sin = cos_ref[...], sin_ref[...]
        q = _rotary(q_ref[...], cos, sin)
        k = _rotary(k_ref[...], cos, sin) * (DK_B ** -0.5)
        o_inner, q_m, dec_q, k_dec = _ret_tile(q, k, v, c)
        for h in range(H_B):
            o_ref[:, h * DV_B:(h + 1) * DV_B] = o_inner[h]
            qm_ref[h] = q_m[h]
        for p in range(H_B // 2):
            kd_ref[p] = k_dec[p]

    rsel = (lax.broadcasted_iota(jnp.int32, (r, LANES), 0) >> shift) == gi
    ti = (lax.broadcasted_iota(jnp.int32, (r, 1), 0) & (c - 1)).astype(F32)
    zero = jnp.zeros((r, LANES), BF16)
    for p in range(H_B // 2):
        s_pair = s0_ref[0, p]
        sb = s_pair.astype(BF16)
        for hh in range(2):
            h = 2 * p + hh
            qm = jnp.where(rsel, qm_ref[h], zero)
            hs = slice(h * DV_B, (h + 1) * DV_B)
            o_ref[:, hs] += jnp.dot(qm, sb, preferred_element_type=F32) * jnp.exp((ti + 1.0) * _log_gamma(h))
        kd = jnp.where(rsel, kd_ref[p], zero)
        so_ref[0, p] = _state_update(s_pair, kd, v, p, c)

    @pl.when(gi == pl.num_programs(0) - 1)
    def _():
        ng = ng_ref[...]
        for h in range(H_B):
            hs = slice(h * DV_B, (h + 1) * DV_B)
            y_ref[:, hs] = _ret_epilogue(o_ref[:, hs], g_ref[:, hs], ng).astype(BF16)


def retention_sample(proj, cos, sin, ret_norm_g, state0, db, t_new, col0):
    n = db * t_new
    wq = H_B * DK_B
    wv = H_B * DV_B
    qc, vc = col0 // wq, col0 // wv
    s0 = state0.reshape(db, H_B // 2, LANES, DV_B)
    st_spec = pl.BlockSpec((1, H_B // 2, LANES, DV_B), lambda g: (g, 0, 0, 0))
    y, st = pl.pallas_call(
        functools.partial(_ret_sample_body, c=t_new),
        grid=(db,),
        in_specs=[pl.BlockSpec((n, wq), lambda g: (0, qc)),
                  pl.BlockSpec((n, wq), lambda g: (0, qc + 1)),
                  pl.BlockSpec((n, wv), lambda g: (0, vc + 1)),
                  pl.BlockSpec((n, wv), lambda g: (0, vc + 2)),
                  pl.BlockSpec((n, wq), lambda g: (0, 0)),
                  pl.BlockSpec((n, wq), lambda g: (0, 0)),
                  pl.BlockSpec((1, DV_B), lambda g: (0, 0)),
                  st_spec],
        out_specs=[pl.BlockSpec((n, wv), lambda g: (0, 0)), st_spec],
        out_shape=[jax.ShapeDtypeStruct((n, wv), BF16),
                   jax.ShapeDtypeStruct((db, H_B // 2, LANES, DV_B), F32)],
        scratch_shapes=[pltpu.VMEM((n, wv), F32),
                        pltpu.VMEM((H_B, n, LANES), BF16),
                        pltpu.VMEM((H_B // 2, n, LANES), BF16)],
        compiler_params=_cparams(("arbitrary",)),
        name="retention_sample",
    )(proj, proj, proj, proj, cos, sin, ret_norm_g.reshape(1, DV_B), s0)
    return y, st.reshape(db, H_B, DK_B, DV_B)


def _rotary_tables(pos):
    half = DK_B // 2
    inv = 1.0 / (10000.0 ** (jnp.arange(half, dtype=F32) / half))
    ang = pos.astype(F32)[:, None] * inv[None, :]
    cos, sin = jnp.cos(ang), jnp.sin(ang)
    cos_t = jnp.tile(jnp.concatenate([cos, cos], axis=-1), (1, H_B))
    sin_t = jnp.tile(jnp.concatenate([-sin, sin], axis=-1), (1, H_B))
    return cos_t, sin_t


_HALO = 32


def _conv_body(a_ref, b_ref, st_ref, w_ref, cb_ref, lg_ref, lb_ref, y_ref, so_ref, full_ref, *, ts):
    ti = pl.program_id(1)
    lo = _HALO - (K_CONV - 1)

    @pl.when(ti == 0)
    def _():
        full_ref[0:lo, :] = jnp.zeros((lo, full_ref.shape[1]), F32)
        full_ref[lo:_HALO, :] = st_ref[0]

    full_ref[_HALO:_HALO + ts, :] = a_ref[0] * jax.nn.sigmoid(b_ref[0])
    acc = jnp.zeros((ts, full_ref.shape[1]), F32)
    for j in range(K_CONV):
        acc = acc + full_ref[lo + j:lo + j + ts, :] * w_ref[j:j + 1, :]
    uc = acc + cb_ref[...]
    mu = jnp.mean(uc, axis=-1, keepdims=True)
    dev = uc - mu
    var = jnp.mean(dev * dev, axis=-1, keepdims=True)
    y = (dev * lax.rsqrt(var + EPS)) * lg_ref[...] + lb_ref[...]
    y_ref[0] = _silu(y).astype(BF16)
    so_ref[0] = full_ref[lo + ts:_HALO + ts, :]
    if ts >= _HALO:
        full_ref[0:_HALO, :] = full_ref[ts:ts + _HALO, :]


def conv_module(proj3, state, conv_w, conv_b, ln_g, ln_b, col0):
    b, s, _ = proj3.shape
    c = conv_w.shape[1]
    ts = _pick(s, (256, 128)) if s >= 128 else s
    nt = s // ts
    assert nt == 1 or ts >= _HALO
    cc = col0 // c
    vec = lambda: pl.BlockSpec((1, c), lambda i, t: (0, 0))
    return pl.pallas_call(
        functools.partial(_conv_body, ts=ts),
        grid=(b, nt),
        in_specs=[pl.BlockSpec((1, ts, c), lambda i, t: (i, t, cc)),
                  pl.BlockSpec((1, ts, c), lambda i, t: (i, t, cc + 1)),
                  pl.BlockSpec((1, K_CONV - 1, c), lambda i, t: (i, 0, 0)),
                  pl.BlockSpec((K_CONV, c), lambda i, t: (0, 0)),
                  vec(), vec(), vec()],
        out_specs=[pl.BlockSpec((1, ts, c), lambda i, t: (i, t, 0)),
                   pl.BlockSpec((1, K_CONV - 1, c), lambda i, t: (i, 0, 0))],
        out_shape=[jax.ShapeDtypeStruct((b, s, c), BF16),
                   jax.ShapeDtypeStruct((b, K_CONV - 1, c), F32)],
        scratch_shapes=[pltpu.VMEM((_HALO + ts, c), F32)],
        compiler_params=_cparams(("arbitrary", "arbitrary")),
        name="conv_module",
    )(proj3, proj3, state, conv_w, conv_b.reshape(1, c), ln_g.reshape(1, c), ln_b.reshape(1, c))


def _merge_body(a_ref, b_ref, c_ref, wa_ref, wb_ref, wc_ref, ga_ref, gb_ref, gc_ref,
                ba_ref, bb_ref, bc_ref, o_ref):
    ya = jnp.dot(a_ref[...], wa_ref[...], preferred_element_type=F32)
    m = jax.nn.sigmoid(ga_ref[...] + ba_ref[...]) * ya
    yb = jnp.dot(b_ref[...], wb_ref[...], preferred_element_type=F32)
    m = m + jax.nn.sigmoid(gb_ref[...] + bb_ref[...]) * yb
    yc = jnp.dot(c_ref[...], wc_ref[...], preferred_element_type=F32)
    m = m + jax.nn.sigmoid(gc_ref[...] + bc_ref[...]) * yc
    o_ref[...] = m.astype(BF16)


def branch_merge(a, b, c, wa, wb, wc, proj, b_gate, col0):
    n, kd = a.shape
    d = wa.shape[1]
    tm = _pick(n, (512, 256, 128))
    tn = _pick(d, (1024, 512, 256, 128))
    nj = d // tn
    g0 = col0 // tn
    bg = b_gate.reshape(1, 3 * d)
    x_spec = lambda: pl.BlockSpec((tm, kd), lambda i, j: (i, 0))
    w_spec = lambda: pl.BlockSpec((kd, tn), lambda i, j: (0, j))
    g_spec = lambda k: pl.BlockSpec((tm, tn), lambda i, j, k=k: (i, g0 + k * nj + j))
    bias_spec = lambda k: pl.BlockSpec((1, tn), lambda i, j, k=k: (0, k * nj + j))
    return pl.pallas_call(
        _merge_body,
        grid=(n // tm, nj),
        in_specs=[x_spec(), x_spec(), x_spec(), w_spec(), w_spec(), w_spec(),
                  g_spec(0), g_spec(1), g_spec(2), bias_spec(0), bias_spec(1), bias_spec(2)],
        out_specs=pl.BlockSpec((tm, tn), lambda i, j: (i, j)),
        out_shape=jax.ShapeDtypeStruct((n, d), BF16),
        compiler_params=_cparams(("parallel", "arbitrary")),
        name="branch_merge",
    )(a, b, c, wa, wb, wc, proj, proj, proj, bg, bg, bg)


def _ffn_act_body(g_ref, u_ref, st_ref, w_ref, cb_ref, y_ref, so_ref, full_ref, *, ts):
    ti = pl.program_id(2)
    lo = SUBLANES - (K_FFN - 1)

    @pl.when(ti == 0)
    def _():
        full_ref[0:lo, :] = jnp.zeros((lo, full_ref.shape[1]), F32)
        full_ref[lo:SUBLANES, :] = st_ref[0]

    full_ref[SUBLANES:SUBLANES + ts, :] = g_ref[0]
    acc = jnp.zeros((ts, full_ref.shape[1]), F32)
    for j in range(K_FFN):
        acc = acc + full_ref[lo + j:lo + j + ts, :] * w_ref[j:j + 1, :]
    y_ref[0] = (_silu(acc + cb_ref[...]) * u_ref[0]).astype(BF16)
    so_ref[0] = full_ref[lo + ts:SUBLANES + ts, :]
    if ts >= SUBLANES:
        full_ref[0:SUBLANES, :] = full_ref[ts:ts + SUBLANES, :]


def ffn_act(h3, state, conv_w, conv_b):
    b, s, f2 = h3.shape
    f = f2 // 2
    tc = _pick(f, (512, 256, 128))
    nc = f // tc
    ts = _pick(s, (512, 256, 128)) if s >= 128 else s
    nt = s // ts
    assert nt == 1 or ts >= SUBLANES
    return pl.pallas_call(
        functools.partial(_ffn_act_body, ts=ts),
        grid=(b, nc, nt),
        in_specs=[pl.BlockSpec((1, ts, tc), lambda i, c, t: (i, t, c)),
                  pl.BlockSpec((1, ts, tc), lambda i, c, t: (i, t, nc + c)),
                  pl.BlockSpec((1, K_FFN - 1, tc), lambda i, c, t: (i, 0, c)),
                  pl.BlockSpec((K_FFN, tc), lambda i, c, t: (0, c)),
                  pl.BlockSpec((1, tc), lambda i, c, t: (0, c))],
        out_specs=[pl.BlockSpec((1, ts, tc), lambda i, c, t: (i, t, c)),
                   pl.BlockSpec((1, K_FFN - 1, tc), lambda i, c, t: (i, 0, c))],
        out_shape=[jax.ShapeDtypeStruct((b, s, f), BF16),
                   jax.ShapeDtypeStruct((b, K_FFN - 1, f), F32)],
        scratch_shapes=[pltpu.VMEM((SUBLANES + ts, tc), F32)],
        compiler_params=_cparams(("arbitrary", "arbitrary", "arbitrary")),
        name="ffn_act",
    )(h3, h3, state, conv_w, conv_b.reshape(1, f))


def _layer(x, batch, seq, pos, attn_fn, ret_fn, conv0, ffn0, lw):
    n, d = x.shape
    a_cols = H_A * (4 * DK_A + DV_A)
    b_cols = H_B * (2 * DK_B + 2 * DV_B)
    c_conv = lw['conv_w'].shape[1]
    proj = norm_matmul(x, lw['norm1_g'], lw['w_in'])
    qn, kf, kb, vf, vb = attn_prep(proj, lw['q_norm_g'], lw['k_norm_g'])
    a_in = attn_fn(qn, kf, kb, vf, vb)
    cos, sin = _rotary_tables(pos)
    b_in, ret_new = ret_fn(proj, cos, sin, a_cols)
    c_in, conv_new = conv_module(proj.reshape(batch, seq, -1), conv0, lw['conv_w'], lw['conv_b'],
                                 lw['conv_ln_g'], lw['conv_ln_b'], a_cols + b_cols)
    c_in = c_in.reshape(n, c_conv)
    m = branch_merge(a_in, b_in, c_in, lw['w_o_a'], lw['w_o_b'], lw['w_o_c'], proj, lw['b_gate'],
                     a_cols + b_cols + 2 * c_conv)
    x = matmul_residual(m, lw['w_out'], x)
    h2 = norm_matmul(x, lw['norm2_g'], lw['w_ffn_in'])
    act, ffn_new = ffn_act(h2.reshape(batch, seq, -1), ffn0, lw['ffn_conv_w'], lw['ffn_conv_b'])
    x = matmul_residual(act.reshape(n, -1), lw['w_ffn_down'], x)
    states = (kf.reshape(batch, seq, H_A, 2 * DK_A), vf.reshape(batch, seq, H_A, DV_A), ret_new, conv_new, ffn_new)
    return x, states


def kernel(x_prompt, x_sample, cache_k, cache_v, state_ret, state_conv, state_ffn, page_table, rel_bias, norm1_g, w_in, b_gate, q_norm_g, k_norm_g, lam_vec, subln_g, w_o_a, ret_norm_g, w_o_b, conv_w, conv_b, conv_ln_g, conv_ln_b, w_o_c, w_out, norm2_g, w_ffn_in, ffn_conv_w, ffn_conv_b, w_ffn_down):
    bp, sp, d = x_prompt.shape
    db, t_new, _ = x_sample.shape
    depth = w_in.shape[0]
    past = page_table.shape[1] * cache_k.shape[2]
    pos_p = jnp.arange(sp)
    pos_s = jnp.tile(past + jnp.arange(t_new), db)
    yp = x_prompt.reshape(bp * sp, d)
    ys = x_sample.reshape(db * t_new, d)
    c_conv = conv_w.shape[2]
    d_ff = ffn_conv_w.shape[2]
    zero_conv = jnp.zeros((bp, K_CONV - 1, c_conv), F32)
    zero_ffn = jnp.zeros((bp, K_FFN - 1, d_ff), F32)
    sts_p, sts_s = [], []
    for l in range(depth):
        lw = {'norm1_g': norm1_g[l], 'w_in': w_in[l].astype(BF16), 'b_gate': b_gate[l],
              'q_norm_g': q_norm_g[l], 'k_norm_g': k_norm_g[l],
              'w_o_a': w_o_a[l].astype(BF16), 'w_o_b': w_o_b[l].astype(BF16), 'w_o_c': w_o_c[l].astype(BF16),
              'conv_w': conv_w[l], 'conv_b': conv_b[l], 'conv_ln_g': conv_ln_g[l], 'conv_ln_b': conv_ln_b[l],
              'w_out': w_out[l].astype(BF16), 'norm2_g': norm2_g[l], 'w_ffn_in': w_ffn_in[l].astype(BF16),
              'ffn_conv_w': ffn_conv_w[l], 'ffn_conv_b': ffn_conv_b[l], 'w_ffn_down': w_ffn_down[l].astype(BF16)}
        lam_init = 0.8 - 0.6 * math.exp(-0.3 * l)

        def p_attn(qn, kf, kb, vf, vb, l=l, lam_init=lam_init):
            return prompt_attn(qn, kb, vb, rel_bias, lam_vec[l], subln_g[l], lam_init, bp, sp)

        def p_ret(proj, cos, sin, col0, l=l):
            return retention_prompt(proj, cos, sin, ret_norm_g[l], bp, sp, col0)

        yp, st = _layer(yp, bp, sp, pos_p, p_attn, p_ret, zero_conv, zero_ffn, lw)
        sts_p.append(st)

        def s_attn(qn, kf, kb, vf, vb, l=l, lam_init=lam_init):
            return sample_attn(qn, kf, vf, cache_k, cache_v, l, page_table, rel_bias, lam_vec[l], subln_g[l],
                               lam_init, db, t_new)

        def s_ret(proj, cos, sin, col0, l=l):
            return retention_sample(proj, cos, sin, ret_norm_g[l], state_ret[l], db, t_new, col0)

        ys, st = _layer(ys, db, t_new, pos_s, s_attn, s_ret, state_conv[l], state_ffn[l], lw)
        sts_s.append(st)

    def stk(sts, i):
        return jnp.stack([st[i] for st in sts], axis=0)

    return (yp.reshape(bp, sp, d), ys.reshape(db, t_new, d),
            stk(sts_p, 0), stk(sts_p, 1), stk(sts_p, 2), stk(sts_p, 3), stk(sts_p, 4),
            stk(sts_s, 0), stk(sts_s, 1), stk(sts_s, 2), stk(sts_s, 3), stk(sts_s, 4))
```

```python
import functools
import math

import numpy as np
import jax
import jax.numpy as jnp
from jax import lax
from jax.experimental import pallas as pl
from jax.experimental.pallas import tpu as pltpu

F32 = jnp.float32
BF16 = jnp.bfloat16

H_A = 8
DK_A = 64
DV_A = 128
H_B = 8
DK_B = 64
DV_B = 128
K_CONV = 31
K_FFN = 3
N_BUCKETS = 32
MAX_DIST = 128
EPS = 1e-6
NEG = -1e30

LANES = 128
SUBLANES = 8
VMEM_LIMIT = 56 * 1024 * 1024

_NT = (((1,), (1,)), ((), ()))
_TN = (((0,), (0,)), ((), ()))


def _cparams(sem):
    return pltpu.CompilerParams(dimension_semantics=sem, vmem_limit_bytes=VMEM_LIMIT)


def _pick(n, cands):
    for c in cands:
        if n % c == 0:
            return c
    return n


def _t5_bucket_np(d):
    max_exact = N_BUCKETS // 2
    d = np.maximum(d, 0)
    df = np.maximum(d, 1).astype(np.float32)
    large = max_exact + (np.log(df / np.float32(max_exact)) / np.float32(math.log(MAX_DIST / max_exact))
                         * np.float32(N_BUCKETS - max_exact)).astype(np.int32)
    return np.where(d < max_exact, d, np.minimum(large, N_BUCKETS - 1)).astype(np.int32)


def _bias_from_buckets(idx, rb_ref, h):
    acc = jnp.zeros(idx.shape, F32)
    for b in range(N_BUCKETS):
        acc = jnp.where(idx == b, rb_ref[b, h], acc)
    return acc


def _lam(lv_ref, lam_init):
    lv = lv_ref[...]
    a = jnp.sum(lv[0:1] * lv[1:2], axis=-1, keepdims=True)
    b = jnp.sum(lv[2:3] * lv[3:4], axis=-1, keepdims=True)
    return jnp.exp(a) - jnp.exp(b) + lam_init


def _silu(x):
    return x * jax.nn.sigmoid(x)


def _norm_mm_body(x_ref, g_ref, w_ref, o_ref, h_ref):
    @pl.when(pl.program_id(1) == 0)
    def _():
        x = x_ref[...]
        ms = jnp.mean(x * x, axis=-1, keepdims=True)
        h_ref[...] = ((x * lax.rsqrt(ms + EPS)) * g_ref[...]).astype(BF16)

    o_ref[...] = jnp.dot(h_ref[...], w_ref[...], preferred_element_type=F32)


def norm_matmul(x, g, w):
    n, d = x.shape
    c = w.shape[1]
    tm = _pick(n, (1024, 512, 256, 128))
    tn = _pick(c, (1024, 512, 256, 128))
    return pl.pallas_call(
        _norm_mm_body,
        grid=(n // tm, c // tn),
        in_specs=[pl.BlockSpec((tm, d), lambda i, j: (i, 0)),
                  pl.BlockSpec((1, d), lambda i, j: (0, 0)),
                  pl.BlockSpec((d, tn), lambda i, j: (0, j))],
        out_specs=pl.BlockSpec((tm, tn), lambda i, j: (i, j)),
        out_shape=jax.ShapeDtypeStruct((n, c), F32),
        scratch_shapes=[pltpu.VMEM((tm, d), BF16)],
        compiler_params=_cparams(("parallel", "arbitrary")),
        name="norm_matmul",
    )(x, g.reshape(1, d), w)


def _mm_res_body(a_ref, w_ref, r_ref, o_ref, acc_ref, *, nk):
    k = pl.program_id(2)
    part = jnp.dot(a_ref[...], w_ref[...], preferred_element_type=F32)
    if nk == 1:
        o_ref[...] = r_ref[...] + part
    else:
        @pl.when(k == 0)
        def _():
            acc_ref[...] = part

        @pl.when(jnp.logical_and(k > 0, k < nk - 1))
        def _():
            acc_ref[...] += part

        @pl.when(k == nk - 1)
        def _():
            o_ref[...] = r_ref[...] + (acc_ref[...] + part)


def matmul_residual(a, w, res):
    n, kd = a.shape
    c = w.shape[1]
    tm = _pick(n, (1024, 512, 256, 128))
    tn = _pick(c, (1024, 512, 256, 128))
    tk = kd if kd <= 2048 else _pick(kd, (2816, 2048, 1024, 512))
    nk = kd // tk
    return pl.pallas_call(
        functools.partial(_mm_res_body, nk=nk),
        grid=(n // tm, c // tn, nk),
        in_specs=[pl.BlockSpec((tm, tk), lambda i, j, k: (i, k)),
                  pl.BlockSpec((tk, tn), lambda i, j, k: (k, j)),
                  pl.BlockSpec((tm, tn), lambda i, j, k: (i, j))],
        out_specs=pl.BlockSpec((tm, tn), lambda i, j, k: (i, j)),
        out_shape=jax.ShapeDtypeStruct((n, c), F32),
        scratch_shapes=[pltpu.VMEM((tm, tn), F32)],
        compiler_params=_cparams(("parallel", "parallel", "arbitrary")),
        name="matmul_residual",
    )(a, w, res)


def _group_rms(x, gain, gm):
    s = x * x
    hi = s.astype(BF16)
    lo = (s - hi.astype(F32)).astype(BF16)
    parts = []
    for c in range(x.shape[1] // LANES):
        sl = slice(c * LANES, (c + 1) * LANES)
        parts.append(jnp.dot(hi[:, sl], gm, preferred_element_type=F32)
                     + jnp.dot(lo[:, sl], gm, preferred_element_type=F32))
    ms = jnp.concatenate(parts, axis=-1) * (1.0 / DK_A)
    return (x * lax.rsqrt(ms + EPS)) * gain


def _attn_prep_sample_body(q_ref, k_ref, v_ref, qg_ref, kg_ref, gm_ref, qo_ref, kf_ref, vf_ref):
    gm = gm_ref[...]
    qo_ref[...] = _group_rms(q_ref[...], qg_ref[...], gm) * (DK_A ** -0.5)
    kf_ref[...] = _group_rms(k_ref[...], kg_ref[...], gm)
    vf_ref[...] = v_ref[...]


def _attn_prep_prompt_body(q_ref, k_ref, v_ref, qg_ref, kg_ref, gm_ref, qt_ref, kf_ref, kb_ref, vf_ref, vt_ref):
    gm = gm_ref[...]
    qn = _group_rms(q_ref[...], qg_ref[...], gm) * (DK_A ** -0.5)
    kn = _group_rms(k_ref[...], kg_ref[...], gm)
    kf_ref[...] = kn
    kb_ref[...] = kn.astype(BF16)
    v = v_ref[...]
    vf_ref[...] = v
    for h in range(H_A):
        hs = slice(h * LANES, (h + 1) * LANES)
        qt_ref[0, h, 0] = qn[:, hs].T.astype(BF16)
        vt_ref[0, h, 0] = v[:, hs].T.astype(BF16)


def _attn_prep_consts(q_gain, k_gain):
    w = H_A * 2 * DK_A
    gm = np.kron(np.eye(LANES // DK_A, dtype=np.float32), np.ones((DK_A, DK_A), np.float32))
    return (jnp.tile(q_gain, w // DK_A).reshape(1, w), jnp.tile(k_gain, w // DK_A).reshape(1, w),
            jnp.asarray(gm, BF16))


def attn_prep_sample(proj, q_gain, k_gain):
    n = proj.shape[0]
    w = H_A * 2 * DK_A
    tm = _pick(n, (512, 256, 128))
    col = lambda c: pl.BlockSpec((tm, w), lambda i, c=c: (i, c))
    cst = lambda shape: pl.BlockSpec(shape, lambda i: (0, 0))
    out = lambda: pl.BlockSpec((tm, w), lambda i: (i, 0))
    return pl.pallas_call(
        _attn_prep_sample_body,
        grid=(n // tm,),
        in_specs=[col(0), col(1), col(2), cst((1, w)), cst((1, w)), cst((LANES, LANES))],
        out_specs=[out(), out(), out()],
        out_shape=[jax.ShapeDtypeStruct((n, w), F32)] * 3,
        compiler_params=_cparams(("parallel",)),
        name="attn_prep_sample",
    )(proj, proj, proj, *_attn_prep_consts(q_gain, k_gain))


def attn_prep_prompt(proj, q_gain, k_gain, batch, seq, t):
    n = batch * seq
    w = H_A * 2 * DK_A
    nt = seq // t
    col = lambda c: pl.BlockSpec((t, w), lambda b, i, c=c: (b * nt + i, c))
    cst = lambda shape: pl.BlockSpec(shape, lambda b, i: (0, 0))
    out = lambda: pl.BlockSpec((t, w), lambda b, i: (b * nt + i, 0))
    tr = lambda: pl.BlockSpec((1, H_A, 1, LANES, t), lambda b, i: (b, 0, i, 0, 0))
    tr_shape = jax.ShapeDtypeStruct((batch, H_A, nt, LANES, t), BF16)
    return pl.pallas_call(
        _attn_prep_prompt_body,
        grid=(batch, nt),
        in_specs=[col(0), col(1), col(2), cst((1, w)), cst((1, w)), cst((LANES, LANES))],
        out_specs=[tr(), out(), out(), out(), tr()],
        out_shape=[tr_shape, jax.ShapeDtypeStruct((n, w), F32), jax.ShapeDtypeStruct((n, w), BF16),
                   jax.ShapeDtypeStruct((n, w), F32), tr_shape],
        compiler_params=_cparams(("parallel", "parallel")),
        name="attn_prep_prompt",
    )(proj, proj, proj, *_attn_prep_consts(q_gain, k_gain))


def _prompt_attn_body(rb_ref, ib_ref, lv_ref, sg_ref, qt_ref, k_ref, vt_ref, o_ref,
                      bias_ref, m_ref, l_ref, acc_ref, *, t, lam_init):
    h = pl.program_id(1)
    qi = pl.program_id(2)

    @pl.when(qi == 0)
    def _():
        far_bias = rb_ref[N_BUCKETS - 1, h]
        for o in range(2):
            idx = ib_ref[o]
            bias_ref[o] = jnp.where(idx >= 0, _bias_from_buckets(idx, rb_ref, h) - far_bias, NEG)

    qt = qt_ref[0, 0, 0]
    row = lax.broadcasted_iota(jnp.int32, qt.shape, 0)
    zero = jnp.zeros_like(qt)
    q2t = jnp.concatenate([jnp.where(row < DK_A, qt, zero), jnp.where(row >= DK_A, qt, zero)], axis=1)

    m_ref[...] = jnp.full(m_ref.shape, NEG, F32)
    l_ref[...] = jnp.zeros(l_ref.shape, F32)
    acc_ref[...] = jnp.zeros(acc_ref.shape, F32)

    def tile(kj, bias):
        off = pl.multiple_of(kj * t, t)
        s = jnp.dot(k_ref[0, pl.ds(off, t), :], q2t, preferred_element_type=F32)
        if bias is not None:
            s = s + bias
        m_old = m_ref[...]
        m_new = jnp.maximum(m_old, jnp.max(s, axis=0, keepdims=True))
        alpha = jnp.exp(m_old - m_new)
        p = jnp.exp(s - m_new)
        l_ref[...] = alpha * l_ref[...] + jnp.sum(p, axis=0, keepdims=True)
        acc_ref[...] = alpha * acc_ref[...] + jnp.dot(vt_ref[0, 0, kj], p.astype(BF16), preferred_element_type=F32)
        m_ref[...] = m_new

    def far(kj, carry):
        tile(kj, None)
        return carry

    lax.fori_loop(0, jnp.maximum(qi - 1, 0), far, 0)

    @pl.when(qi >= 1)
    def _():
        tile(qi - 1, bias_ref[1])

    tile(qi, bias_ref[0])

    o2t = acc_ref[...] / l_ref[...]
    o = (o2t[:, 0:t] - _lam(lv_ref, lam_init) * o2t[:, t:2 * t]).T
    ms = jnp.mean(o * o, axis=-1, keepdims=True)
    y = ((o * lax.rsqrt(ms + EPS)) * sg_ref[...]) * (1.0 - lam_init)
    o_ref[0] = y.astype(BF16)


_ATTN_TILE = 256


def prompt_attn(qt, kb, vt, rel_bias, lam_vec, subln_g, lam_init, batch, seq, t):
    w = H_A * DV_A
    assert t >= MAX_DIST and seq % t == 0
    nq = seq // t
    key = np.arange(t)[:, None]
    qry = (np.arange(2 * t) % t)[None, :]
    d0 = qry - key
    ib = jnp.asarray(np.stack([np.where(d0 >= 0, _t5_bucket_np(d0), -1), _t5_bucket_np(d0 + t)]), jnp.int32)
    assert int(_t5_bucket_np(np.array([t + 1]))[0]) == N_BUCKETS - 1
    out = pl.pallas_call(
        functools.partial(_prompt_attn_body, t=t, lam_init=lam_init),
        grid=(batch, H_A, nq),
        in_specs=[pl.BlockSpec(memory_space=pltpu.SMEM),
                  pl.BlockSpec((2, t, 2 * t), lambda b, h, i: (0, 0, 0)),
                  pl.BlockSpec((4, DK_A), lambda b, h, i: (0, 0)),
                  pl.BlockSpec((1, DV_A), lambda b, h, i: (0, 0)),
                  pl.BlockSpec((1, 1, 1, LANES, t), lambda b, h, i: (b, h, i, 0, 0)),
                  pl.BlockSpec((1, seq, LANES), lambda b, h, i: (b, 0, h)),
                  pl.BlockSpec((1, 1, nq, LANES, t), lambda b, h, i: (b, h, 0, 0, 0))],
        out_specs=pl.BlockSpec((1, t, LANES), lambda b, h, i: (b, i, h)),
        out_shape=jax.ShapeDtypeStruct((batch, seq, w), BF16),
        scratch_shapes=[pltpu.VMEM((2, t, 2 * t), F32),
                        pltpu.VMEM((1, 2 * t), F32),
                        pltpu.VMEM((1, 2 * t), F32),
                        pltpu.VMEM((DV_A, 2 * t), F32)],
        compiler_params=_cparams(("arbitrary", "arbitrary", "arbitrary")),
        name="prompt_attn",
    )(rel_bias, ib, lam_vec, subln_g.reshape(1, DV_A), qt, kb.reshape(batch, seq, w), vt)
    return out.reshape(batch * seq, w)


_TQ = SUBLANES
_RH = 2 * _TQ


_KPAD = LANES // H_A


def _sample_attn_body(pt_ref, rb_ref, ibp_ref, ibn_ref, lv_ref, sg_ref, q_ref, kn_ref, vn_ref, *rest,
                      pps, nsteps, lam_init):
    k_refs = rest[0:pps]
    v_refs = rest[pps:2 * pps]
    o_ref = rest[2 * pps]
    biasp_ref, biasn_ref, m_ref, l_ref, acc_ref = rest[2 * pps + 1:]
    b = pl.program_id(0)
    s_id = pl.program_id(1)

    @pl.when(jnp.logical_and(b == 0, s_id == 0))
    def _():
        for h in range(H_A):
            sl = slice(h * _RH, (h + 1) * _RH)
            far_bias = rb_ref[N_BUCKETS - 1, h]
            for o in range(2):
                idx = ibp_ref[o, sl, :]
                biasp_ref[o, sl, :] = jnp.where(idx >= 0, _bias_from_buckets(idx, rb_ref, h) - far_bias, NEG)
            idx = ibn_ref[sl, :]
            biasn_ref[sl, :] = jnp.where(idx >= 0, _bias_from_buckets(idx, rb_ref, h) - far_bias, NEG)

    @pl.when(s_id == 0)
    def _():
        m_ref[...] = jnp.full(m_ref.shape, NEG, F32)
        l_ref[...] = jnp.zeros(l_ref.shape, F32)
        acc_ref[...] = jnp.zeros(acc_ref.shape, F32)

    q = q_ref[0]
    lane = lax.broadcasted_iota(jnp.int32, (_TQ, LANES), 1)
    blocks = []
    for h in range(H_A):
        qh = q[:, h * LANES:(h + 1) * LANES]
        blocks += [jnp.where(lane < DK_A, qh, 0.0), jnp.where(lane >= DK_A, qh, 0.0)]
    q2 = jnp.concatenate(blocks, axis=0).astype(BF16)

    def attend(kt, vt, bias):
        s = lax.dot_general(q2, kt, _NT, preferred_element_type=F32) + bias
        m_old = m_ref[...]
        m_new = jnp.maximum(m_old, jnp.max(s, axis=-1, keepdims=True))
        alpha = jnp.exp(m_old - m_new)
        p = jnp.exp(s - m_new)
        l_ref[...] = alpha * l_ref[...] + jnp.sum(p, axis=-1, keepdims=True)
        acc_ref[...] = alpha * acc_ref[...] + jnp.dot(p.astype(BF16), vt, preferred_element_type=F32)
        m_ref[...] = m_new

    kt = jnp.concatenate([r[0, 0].astype(BF16) for r in k_refs], axis=0)
    vt = jnp.concatenate([r[0, 0].astype(BF16) for r in v_refs], axis=0)
    last = s_id == nsteps - 1
    bias = jnp.concatenate([biasp_ref[0]] * (pps - 1) + [biasp_ref[jnp.where(last, 1, 0)]], axis=1)
    attend(kt, vt, bias)

    @pl.when(last)
    def _():
        attend(kn_ref[0].astype(BF16), vn_ref[0].astype(BF16), biasn_ref[...])
        o2 = acc_ref[...] / l_ref[...]
        lam = _lam(lv_ref, lam_init)
        for h in range(H_A):
            o = o2[h * _RH:h * _RH + _TQ] - lam * o2[h * _RH + _TQ:(h + 1) * _RH]
            ms = jnp.mean(o * o, axis=-1, keepdims=True)
            y = ((o * lax.rsqrt(ms + EPS)) * sg_ref[...]) * (1.0 - lam_init)
            o_ref[0, :, h * LANES:(h + 1) * LANES] = y


def sample_attn(qn, kf, vf, cache_k, cache_v, layer, page_table, rel_bias, lam_vec, subln_g, lam_init, db, t_new):
    w = H_A * DV_A
    depth, n_pool, page = cache_k.shape[:3]
    n_pages = page_table.shape[1]
    assert t_new <= _TQ and t_new <= _KPAD and page >= MAX_DIST
    pps = _pick(n_pages, (8, 4, 2, 1))
    nsteps = n_pages // pps
    rows = H_A * _RH
    kw = page * H_A
    rh = (np.arange(rows) // _RH)[:, None]
    rt = np.minimum(np.arange(rows) % _TQ, t_new - 1)[:, None]
    ck_key, ck_head = (np.arange(kw) // H_A)[None, :], (np.arange(kw) % H_A)[None, :]
    ib_far = np.where(rh == ck_head, N_BUCKETS - 1, -1)
    ib_last = np.where(rh == ck_head, _t5_bucket_np(page + rt - ck_key), -1)
    nk_key, nk_head = (np.arange(_KPAD * H_A) // H_A)[None, :], (np.arange(_KPAD * H_A) % H_A)[None, :]
    d_new = rt - nk_key
    ib_new = np.where((rh == nk_head) & (d_new >= 0) & (nk_key < t_new), _t5_bucket_np(d_new), -1)
    ibp = jnp.asarray(np.stack([ib_far, ib_last]), jnp.int32)
    ibn = jnp.asarray(ib_new, jnp.int32)
    assert int(_t5_bucket_np(np.array([page + 1]))[0]) == N_BUCKETS - 1

    q_pad = jnp.pad(qn.reshape(db, t_new, w), ((0, 0), (0, _TQ - t_new), (0, 0)))

    def new_rows(x):
        return jnp.pad(x.reshape(db, t_new * H_A, LANES), ((0, 0), (0, (_KPAD - t_new) * H_A), (0, 0)))

    ck = cache_k.reshape(depth, n_pool, kw, LANES)
    cv = cache_v.reshape(depth, n_pool, kw, LANES)
    new_spec = pl.BlockSpec((1, _KPAD * H_A, LANES), lambda b, s, pt: (b, 0, 0))

    def page_spec(i):
        return pl.BlockSpec((1, 1, kw, LANES), lambda b, s, pt, i=i: (layer, pt[b, s * pps + i], 0, 0))

    grid_spec = pltpu.PrefetchScalarGridSpec(
        num_scalar_prefetch=1,
        grid=(db, nsteps),
        in_specs=[pl.BlockSpec(memory_space=pltpu.SMEM),
                  pl.BlockSpec((2, rows, kw), lambda b, s, pt: (0, 0, 0)),
                  pl.BlockSpec((rows, _KPAD * H_A), lambda b, s, pt: (0, 0)),
                  pl.BlockSpec((4, DK_A), lambda b, s, pt: (0, 0)),
                  pl.BlockSpec((1, DV_A), lambda b, s, pt: (0, 0)),
                  pl.BlockSpec((1, _TQ, w), lambda b, s, pt: (b, 0, 0)),
                  new_spec, new_spec]
                 + [page_spec(i) for i in range(pps)] + [page_spec(i) for i in range(pps)],
        out_specs=pl.BlockSpec((1, _TQ, w), lambda b, s, pt: (b, 0, 0)),
        scratch_shapes=[pltpu.VMEM((2, rows, kw), F32),
                        pltpu.VMEM((rows, _KPAD * H_A), F32),
                        pltpu.VMEM((rows, 1), F32),
                        pltpu.VMEM((rows, 1), F32),
                        pltpu.VMEM((rows, DV_A), F32)],
    )
    out = pl.pallas_call(
        functools.partial(_sample_attn_body, pps=pps, nsteps=nsteps, lam_init=lam_init),
        grid_spec=grid_spec,
        out_shape=jax.ShapeDtypeStruct((db, _TQ, w), F32),
        compiler_params=_cparams(("arbitrary", "arbitrary")),
        name="sample_attn",
    )(page_table, rel_bias, ibp, ibn, lam_vec, subln_g.reshape(1, DV_A), q_pad, new_rows(kf), new_rows(vf),
      *([ck] * pps), *([cv] * pps))
    return out[:, :t_new].reshape(db * t_new, w).astype(BF16)


def _log_gamma(h):
    return float(np.log1p(-np.exp2(np.float32(-5.0 - h), dtype=np.float32), dtype=np.float32))


def _swap_halves(x):
    n = x.shape[1]
    half = DK_B // 2
    lane = lax.broadcasted_iota(jnp.int32, x.shape, 1)
    return jnp.where((lane % DK_B) < half, pltpu.roll(x, n - half, 1), pltpu.roll(x, half, 1))


def _rotary(x, cos, sin):
    return x * cos + _swap_halves(x) * sin


def _ret_tile(q, k, v, c):
    r = q.shape[0]
    shift = int(math.log2(c))
    assert 1 << shift == c
    ri = lax.broadcasted_iota(jnp.int32, (r, r), 0)
    ci = lax.broadcasted_iota(jnp.int32, (r, r), 1)
    same = (ri >> shift) == (ci >> shift)
    diff = ((ri & (c - 1)) - (ci & (c - 1))).astype(F32)
    valid = jnp.logical_and(same, diff >= 0.0)
    ti = (lax.broadcasted_iota(jnp.int32, (r, 1), 0) & (c - 1)).astype(F32)
    lane = lax.broadcasted_iota(jnp.int32, (r, LANES), 1)
    first = lane < DK_B
    o_inner, q_m, dec_q, k_dec = [], [], [], []
    for p in range(H_B // 2):
        ps = slice(p * LANES, (p + 1) * LANES)
        qp, kp = q[:, ps], k[:, ps]
        kpb = kp.astype(BF16)
        lg0, lg1 = _log_gamma(2 * p), _log_gamma(2 * p + 1)
        k_dec.append((kp * jnp.where(first, jnp.exp((c - 1.0 - ti) * lg0), jnp.exp((c - 1.0 - ti) * lg1))).astype(BF16))
        for hh in range(2):
            h = 2 * p + hh
            lg = lg1 if hh else lg0
            qm = jnp.where(first if hh == 0 else jnp.logical_not(first), qp, 0.0).astype(BF16)
            dmat = jnp.where(valid, jnp.exp(jnp.maximum(diff, 0.0) * lg), 0.0)
            inner = lax.dot_general(qm, kpb, _NT, preferred_element_type=F32) * dmat
            vh = v[:, h * DV_B:(h + 1) * DV_B].astype(BF16)
            o_inner.append(jnp.dot(inner.astype(BF16), vh, preferred_element_type=F32))
            q_m.append(qm)
            dec_q.append(jnp.exp((ti + 1.0) * lg))
    return o_inner, q_m, dec_q, k_dec


def _state_update(s_pair, kd, v, p, c):
    row = lax.broadcasted_iota(jnp.int32, (LANES, DV_B), 0)
    top = row < DK_B
    u0 = lax.dot_general(kd, v[:, (2 * p) * DV_B:(2 * p + 1) * DV_B].astype(BF16), _TN, preferred_element_type=F32)
    u1 = lax.dot_general(kd, v[:, (2 * p + 1) * DV_B:(2 * p + 2) * DV_B].astype(BF16), _TN, preferred_element_type=F32)
    gc = jnp.where(top, math.exp(c * _log_gamma(2 * p)), math.exp(c * _log_gamma(2 * p + 1)))
    return s_pair * gc + jnp.where(top, u0, u1)


def _ret_epilogue(o, g, ng):
    ms = jnp.mean(o * o, axis=-1, keepdims=True)
    return ((o * lax.rsqrt(ms + EPS)) * ng) * _silu(g)


def _ret_prompt_body(q_ref, k_ref, v_ref, g_ref, cos_ref, sin_ref, ng_ref, y_ref, so_ref, s_ref, *, c):
    ci = pl.program_id(1)

    @pl.when(ci == 0)
    def _():
        s_ref[...] = jnp.zeros(s_ref.shape, F32)

    cos, sin = cos_ref[...], sin_ref[...]
    q = _rotary(q_ref[...], cos, sin)
    k = _rotary(k_ref[...], cos, sin) * (DK_B ** -0.5)
    v = v_ref[...]
    o_inner, q_m, dec_q, k_dec = _ret_tile(q, k, v, c)
    ng = ng_ref[...]
    for p in range(H_B // 2):
        s_pair = s_ref[p]
        sb = s_pair.astype(BF16)
        for hh in range(2):
            h = 2 * p + hh
            o = o_inner[h] + jnp.dot(q_m[h], sb, preferred_element_type=F32) * dec_q[h]
            hs = slice(h * DV_B, (h + 1) * DV_B)
            y_ref[:, hs] = _ret_epilogue(o, g_ref[:, hs], ng).astype(BF16)
        s_ref[p] = _state_update(s_pair, k_dec[p], v, p, c)

    @pl.when(ci == pl.num_programs(1) - 1)
    def _():
        so_ref[0] = s_ref[...]


def retention_prompt(proj, cos, sin, ret_norm_g, batch, seq, col0):
    n = batch * seq
    c = 128 if seq % 128 == 0 else seq
    nc = seq // c
    wq = H_B * DK_B
    wv = H_B * DV_B
    assert col0 % wv == 0
    qc, vc = col0 // wq, col0 // wv
    row = lambda b, i: b * nc + i
    y, st = pl.pallas_call(
        functools.partial(_ret_prompt_body, c=c),
        grid=(batch, nc),
        in_specs=[pl.BlockSpec((c, wq), lambda b, i: (row(b, i), qc)),
                  pl.BlockSpec((c, wq), lambda b, i: (row(b, i), qc + 1)),
                  pl.BlockSpec((c, wv), lambda b, i: (row(b, i), vc + 1)),
                  pl.BlockSpec((c, wv), lambda b, i: (row(b, i), vc + 2)),
                  pl.BlockSpec((c, wq), lambda b, i: (i, 0)),
                  pl.BlockSpec((c, wq), lambda b, i: (i, 0)),
                  pl.BlockSpec((1, DV_B), lambda b, i: (0, 0))],
        out_specs=[pl.BlockSpec((c, wv), lambda b, i: (row(b, i), 0)),
                   pl.BlockSpec((1, H_B // 2, LANES, DV_B), lambda b, i: (b, 0, 0, 0))],
        out_shape=[jax.ShapeDtypeStruct((n, wv), BF16),
                   jax.ShapeDtypeStruct((batch, H_B // 2, LANES, DV_B), F32)],
        scratch_shapes=[pltpu.VMEM((H_B // 2, LANES, DV_B), F32)],
        compiler_params=_cparams(("arbitrary", "arbitrary")),
        name="retention_prompt",
    )(proj, proj, proj, proj, cos, sin, ret_norm_g.reshape(1, DV_B))
    return y, st.reshape(batch, H_B, DK_B, DV_B)


def _ret_sample_body(q_ref, k_ref, v_ref, g_ref, cos_ref, sin_ref, ng_ref, s0_ref, y_ref, so_ref,
                     o_ref, qm_ref, kd_ref, *, c):
    gi = pl.program_id(0)
    r = q_ref.shape[0]
    shift = int(math.log2(c))
    v = v_ref[...]

    @pl.when(gi == 0)
    def _():
        cos, sin = cos_ref[...], sin_ref[...]
        q = _rotary(q_ref[...], cos, sin)
        k = _rotary(k_ref[...], cos, sin) * (DK_B ** -0.5)
        o_inner, q_m, dec_q, k_dec = _ret_tile(q, k, v, c)
        for h in range(H_B):
            o_ref[:, h * DV_B:(h + 1) * DV_B] = o_inner[h]
            qm_ref[h] = q_m[h]
        for p in range(H_B // 2):
            kd_ref[p] = k_dec[p]

    rsel = (lax.broadcasted_iota(jnp.int32, (r, LANES), 0) >> shift) == gi
    ti = (lax.broadcasted_iota(jnp.int32, (r, 1), 0) & (c - 1)).astype(F32)
    zero = jnp.zeros((r, LANES), BF16)
    for p in range(H_B // 2):
        s_pair = s0_ref[0, p]
        sb = s_pair.astype(BF16)
        for hh in range(2):
            h = 2 * p + hh
            qm = jnp.where(rsel, qm_ref[h], zero)
            hs = slice(h * DV_B, (h + 1) * DV_B)
            o_ref[:, hs] += jnp.dot(qm, sb, preferred_element_type=F32) * jnp.exp((ti + 1.0) * _log_gamma(h))
        kd = jnp.where(rsel, kd_ref[p], zero)
        so_ref[0, p] = _state_update(s_pair, kd, v, p, c)

    @pl.when(gi == pl.num_programs(0) - 1)
    def _():
        ng = ng_ref[...]
        for h in range(H_B):
            hs = slice(h * DV_B, (h + 1) * DV_B)
            y_ref[:, hs] = _ret_epilogue(o_ref[:, hs], g_ref[:, hs], ng).astype(BF16)


def retention_sample(proj, cos, sin, ret_norm_g, state0, db, t_new, col0):
    n = db * t_new
    wq = H_B * DK_B
    wv = H_B * DV_B
    qc, vc = col0 // wq, col0 // wv
    s0 = state0.reshape(db, H_B // 2, LANES, DV_B)
    st_spec = pl.BlockSpec((1, H_B // 2, LANES, DV_B), lambda g: (g, 0, 0, 0))
    y, st = pl.pallas_call(
        functools.partial(_ret_sample_body, c=t_new),
        grid=(db,),
        in_specs=[pl.BlockSpec((n, wq), lambda g: (0, qc)),
                  pl.BlockSpec((n, wq), lambda g: (0, qc + 1)),
                  pl.BlockSpec((n, wv), lambda g: (0, vc + 1)),
                  pl.BlockSpec((n, wv), lambda g: (0, vc + 2)),
                  pl.BlockSpec((n, wq), lambda g: (0, 0)),
                  pl.BlockSpec((n, wq), lambda g: (0, 0)),
                  pl.BlockSpec((1, DV_B), lambda g: (0, 0)),
                  st_spec],
        out_specs=[pl.BlockSpec((n, wv), lambda g: (0, 0)), st_spec],
        out_shape=[jax.ShapeDtypeStruct((n, wv), BF16),
                   jax.ShapeDtypeStruct((db, H_B // 2, LANES, DV_B), F32)],
        scratch_shapes=[pltpu.VMEM((n, wv), F32),
                        pltpu.VMEM((H_B, n, LANES), BF16),
                        pltpu.VMEM((H_B // 2, n, LANES), BF16)],
        compiler_params=_cparams(("arbitrary",)),
        name="retention_sample",
    )(proj, proj, proj, proj, cos, sin, ret_norm_g.reshape(1, DV_B), s0)
    return y, st.reshape(db, H_B, DK_B, DV_B)


def _rotary_tables(pos):
    half = DK_B // 2
    inv = 1.0 / (10000.0 ** (jnp.arange(half, dtype=F32) / half))
    ang = pos.astype(F32)[:, None] * inv[None, :]
    cos, sin = jnp.cos(ang), jnp.sin(ang)
    cos_t = jnp.tile(jnp.concatenate([cos, cos], axis=-1), (1, H_B))
    sin_t = jnp.tile(jnp.concatenate([-sin, sin], axis=-1), (1, H_B))
    return cos_t, sin_t


_HALO = 32


def _conv_body(a_ref, b_ref, st_ref, w_ref, cb_ref, lg_ref, lb_ref, y_ref, so_ref, full_ref, *, ts):
    ti = pl.program_id(1)
    lo = _HALO - (K_CONV - 1)

    @pl.when(ti == 0)
    def _():
        full_ref[0:lo, :] = jnp.zeros((lo, full_ref.shape[1]), F32)
        full_ref[lo:_HALO, :] = st_ref[0]

    full_ref[_HALO:_HALO + ts, :] = a_ref[0] * jax.nn.sigmoid(b_ref[0])
    acc = jnp.zeros((ts, full_ref.shape[1]), F32)
    for j in range(K_CONV):
        acc = acc + full_ref[lo + j:lo + j + ts, :] * w_ref[j:j + 1, :]
    uc = acc + cb_ref[...]
    mu = jnp.mean(uc, axis=-1, keepdims=True)
    dev = uc - mu
    var = jnp.mean(dev * dev, axis=-1, keepdims=True)
    y = (dev * lax.rsqrt(var + EPS)) * lg_ref[...] + lb_ref[...]
    y_ref[0] = _silu(y).astype(BF16)
    so_ref[0] = full_ref[lo + ts:_HALO + ts, :]
    if ts >= _HALO:
        full_ref[0:_HALO, :] = full_ref[ts:ts + _HALO, :]


def conv_module(proj3, state, conv_w, conv_b, ln_g, ln_b, col0):
    b, s, _ = proj3.shape
    c = conv_w.shape[1]
    ts = _pick(s, (256, 128)) if s >= 128 else s
    nt = s // ts
    assert nt == 1 or ts >= _HALO
    cc = col0 // c
    vec = lambda: pl.BlockSpec((1, c), lambda i, t: (0, 0))
    return pl.pallas_call(
        functools.partial(_conv_body, ts=ts),
        grid=(b, nt),
        in_specs=[pl.BlockSpec((1, ts, c), lambda i, t: (i, t, cc)),
                  pl.BlockSpec((1, ts, c), lambda i, t: (i, t, cc + 1)),
                  pl.BlockSpec((1, K_CONV - 1, c), lambda i, t: (i, 0, 0)),
                  pl.BlockSpec((K_CONV, c), lambda i, t: (0, 0)),
                  vec(), vec(), vec()],
        out_specs=[pl.BlockSpec((1, ts, c), lambda i, t: (i, t, 0)),
                   pl.BlockSpec((1, K_CONV - 1, c), lambda i, t: (i, 0, 0))],
        out_shape=[jax.ShapeDtypeStruct((b, s, c), BF16),
                   jax.ShapeDtypeStruct((b, K_CONV - 1, c), F32)],
        scratch_shapes=[pltpu.VMEM((_HALO + ts, c), F32)],
        compiler_params=_cparams(("arbitrary", "arbitrary")),
        name="conv_module",
    )(proj3, proj3, state, conv_w, conv_b.reshape(1, c), ln_g.reshape(1, c), ln_b.reshape(1, c))


def _merge_body(a_ref, b_ref, c_ref, wa_ref, wb_ref, wc_ref, ga_ref, gb_ref, gc_ref,
                ba_ref, bb_ref, bc_ref, o_ref):
    ya = jnp.dot(a_ref[...], wa_ref[...], preferred_element_type=F32)
    m = jax.nn.sigmoid(ga_ref[...] + ba_ref[...]) * ya
    yb = jnp.dot(b_ref[...], wb_ref[...], preferred_element_type=F32)
    m = m + jax.nn.sigmoid(gb_ref[...] + bb_ref[...]) * yb
    yc = jnp.dot(c_ref[...], wc_ref[...], preferred_element_type=F32)
    m = m + jax.nn.sigmoid(gc_ref[...] + bc_ref[...]) * yc
    o_ref[...] = m.astype(BF16)


def branch_merge(a, b, c, wa, wb, wc, proj, b_gate, col0):
    n, kd = a.shape
    d = wa.shape[1]
    tm = _pick(n, (512, 256, 128))
    tn = _pick(d, (1024, 512, 256, 128))
    nj = d // tn
    g0 = col0 // tn
    bg = b_gate.reshape(1, 3 * d)
    x_spec = lambda: pl.BlockSpec((tm, kd), lambda i, j: (i, 0))
    w_spec = lambda: pl.BlockSpec((kd, tn), lambda i, j: (0, j))
    g_spec = lambda k: pl.BlockSpec((tm, tn), lambda i, j, k=k: (i, g0 + k * nj + j))
    bias_spec = lambda k: pl.BlockSpec((1, tn), lambda i, j, k=k: (0, k * nj + j))
    return pl.pallas_call(
        _merge_body,
        grid=(n // tm, nj),
        in_specs=[x_spec(), x_spec(), x_spec(), w_spec(), w_spec(), w_spec(),
                  g_spec(0), g_spec(1), g_spec(2), bias_spec(0), bias_spec(1), bias_spec(2)],
        out_specs=pl.BlockSpec((tm, tn), lambda i, j: (i, j)),
        out_shape=jax.ShapeDtypeStruct((n, d), BF16),
        compiler_params=_cparams(("parallel", "arbitrary")),
        name="branch_merge",
    )(a, b, c, wa, wb, wc, proj, proj, proj, bg, bg, bg)


def _ffn_act_body(g_ref, u_ref, st_ref, w_ref, cb_ref, y_ref, so_ref, full_ref, *, ts):
    ti = pl.program_id(2)
    lo = SUBLANES - (K_FFN - 1)

    @pl.when(ti == 0)
    def _():
        full_ref[0:lo, :] = jnp.zeros((lo, full_ref.shape[1]), F32)
        full_ref[lo:SUBLANES, :] = st_ref[0]

    full_ref[SUBLANES:SUBLANES + ts, :] = g_ref[0]
    acc = jnp.zeros((ts, full_ref.shape[1]), F32)
    for j in range(K_FFN):
        acc = acc + full_ref[lo + j:lo + j + ts, :] * w_ref[j:j + 1, :]
    y_ref[0] = (_silu(acc + cb_ref[...]) * u_ref[0]).astype(BF16)
    so_ref[0] = full_ref[lo + ts:SUBLANES + ts, :]
    if ts >= SUBLANES:
        full_ref[0:SUBLANES, :] = full_ref[ts:ts + SUBLANES, :]


def ffn_act(h3, state, conv_w, conv_b):
    b, s, f2 = h3.shape
    f = f2 // 2
    ts = _pick(s, (512, 256, 128)) if s >= 128 else s
    tc = _pick(f, (512, 256, 128)) if ts >= 128 else f
    nc = f // tc
    nt = s // ts
    assert nt == 1 or ts >= SUBLANES
    return pl.pallas_call(
        functools.partial(_ffn_act_body, ts=ts),
        grid=(b, nc, nt),
        in_specs=[pl.BlockSpec((1, ts, tc), lambda i, c, t: (i, t, c)),
                  pl.BlockSpec((1, ts, tc), lambda i, c, t: (i, t, nc + c)),
                  pl.BlockSpec((1, K_FFN - 1, tc), lambda i, c, t: (i, 0, c)),
                  pl.BlockSpec((K_FFN, tc), lambda i, c, t: (0, c)),
                  pl.BlockSpec((1, tc), lambda i, c, t: (0, c))],
        out_specs=[pl.BlockSpec((1, ts, tc), lambda i, c, t: (i, t, c)),
                   pl.BlockSpec((1, K_FFN - 1, tc), lambda i, c, t: (i, 0, c))],
        out_shape=[jax.ShapeDtypeStruct((b, s, f), BF16),
                   jax.ShapeDtypeStruct((b, K_FFN - 1, f), F32)],
        scratch_shapes=[pltpu.VMEM((SUBLANES + ts, tc), F32)],
        compiler_params=_cparams(("arbitrary", "arbitrary", "arbitrary")),
        name="ffn_act",
    )(h3, h3, state, conv_w, conv_b.reshape(1, f))


def _layer(x, batch, seq, pos, attn_fn, ret_fn, conv0, ffn0, lw):
    n, d = x.shape
    a_cols = H_A * (4 * DK_A + DV_A)
    b_cols = H_B * (2 * DK_B + 2 * DV_B)
    c_conv = lw['conv_w'].shape[1]
    proj = norm_matmul(x, lw['norm1_g'], lw['w_in'])
    a_in, kf, vf = attn_fn(proj)
    cos, sin = _rotary_tables(pos)
    b_in, ret_new = ret_fn(proj, cos, sin, a_cols)
    c_in, conv_new = conv_module(proj.reshape(batch, seq, -1), conv0, lw['conv_w'], lw['conv_b'],
                                 lw['conv_ln_g'], lw['conv_ln_b'], a_cols + b_cols)
    c_in = c_in.reshape(n, c_conv)
    m = branch_merge(a_in, b_in, c_in, lw['w_o_a'], lw['w_o_b'], lw['w_o_c'], proj, lw['b_gate'],
                     a_cols + b_cols + 2 * c_conv)
    x = matmul_residual(m, lw['w_out'], x)
    h2 = norm_matmul(x, lw['norm2_g'], lw['w_ffn_in'])
    act, ffn_new = ffn_act(h2.reshape(batch, seq, -1), ffn0, lw['ffn_conv_w'], lw['ffn_conv_b'])
    x = matmul_residual(act.reshape(n, -1), lw['w_ffn_down'], x)
    states = (kf.reshape(batch, seq, H_A, 2 * DK_A), vf.reshape(batch, seq, H_A, DV_A), ret_new, conv_new, ffn_new)
    return x, states


def kernel(x_prompt, x_sample, cache_k, cache_v, state_ret, state_conv, state_ffn, page_table, rel_bias, norm1_g, w_in, b_gate, q_norm_g, k_norm_g, lam_vec, subln_g, w_o_a, ret_norm_g, w_o_b, conv_w, conv_b, conv_ln_g, conv_ln_b, w_o_c, w_out, norm2_g, w_ffn_in, ffn_conv_w, ffn_conv_b, w_ffn_down):
    bp, sp, d = x_prompt.shape
    db, t_new, _ = x_sample.shape
    depth = w_in.shape[0]
    past = page_table.shape[1] * cache_k.shape[2]
    pos_p = jnp.arange(sp)
    pos_s = jnp.tile(past + jnp.arange(t_new), db)
    yp = x_prompt.reshape(bp * sp, d)
    ys = x_sample.reshape(db * t_new, d)
    c_conv = conv_w.shape[2]
    d_ff = ffn_conv_w.shape[2]
    zero_conv = jnp.zeros((bp, K_CONV - 1, c_conv), F32)
    zero_ffn = jnp.zeros((bp, K_FFN - 1, d_ff), F32)
    sts_p, sts_s = [], []
    for l in range(depth):
        lw = {'norm1_g': norm1_g[l], 'w_in': w_in[l].astype(BF16), 'b_gate': b_gate[l],
              'q_norm_g': q_norm_g[l], 'k_norm_g': k_norm_g[l],
              'w_o_a': w_o_a[l].astype(BF16), 'w_o_b': w_o_b[l].astype(BF16), 'w_o_c': w_o_c[l].astype(BF16),
              'conv_w': conv_w[l], 'conv_b': conv_b[l], 'conv_ln_g': conv_ln_g[l], 'conv_ln_b': conv_ln_b[l],
              'w_out': w_out[l].astype(BF16), 'norm2_g': norm2_g[l], 'w_ffn_in': w_ffn_in[l].astype(BF16),
              'ffn_conv_w': ffn_conv_w[l], 'ffn_conv_b': ffn_conv_b[l], 'w_ffn_down': w_ffn_down[l].astype(BF16)}
        lam_init = 0.8 - 0.6 * math.exp(-0.3 * l)

        def p_attn(proj, l=l, lam_init=lam_init, lw=lw):
            t = _pick(sp, (_ATTN_TILE, LANES))
            qt, kf, kb, vf, vt = attn_prep_prompt(proj, lw['q_norm_g'], lw['k_norm_g'], bp, sp, t)
            return prompt_attn(qt, kb, vt, rel_bias, lam_vec[l], subln_g[l], lam_init, bp, sp, t), kf, vf

        def p_ret(proj, cos, sin, col0, l=l):
            return retention_prompt(proj, cos, sin, ret_norm_g[l], bp, sp, col0)

        yp, st = _layer(yp, bp, sp, pos_p, p_attn, p_ret, zero_conv, zero_ffn, lw)
        sts_p.append(st)

        def s_attn(proj, l=l, lam_init=lam_init, lw=lw):
            qn, kf, vf = attn_prep_sample(proj, lw['q_norm_g'], lw['k_norm_g'])
            return sample_attn(qn, kf, vf, cache_k, cache_v, l, page_table, rel_bias, lam_vec[l], subln_g[l],
                               lam_init, db, t_new), kf, vf

        def s_ret(proj, cos, sin, col0, l=l):
            return retention_sample(proj, cos, sin, ret_norm_g[l], state_ret[l], db, t_new, col0)

        ys, st = _layer(ys, db, t_new, pos_s, s_attn, s_ret, state_conv[l], state_ffn[l], lw)
        sts_s.append(st)

    def stk(sts, i):
        return jnp.stack([st[i] for st in sts], axis=0)

    return (yp.reshape(bp, sp, d), ys.reshape(db, t_new, d),
            stk(sts_p, 0), stk(sts_p, 1), stk(sts_p, 2), stk(sts_p, 3), stk(sts_p, 4),
            stk(sts_s, 0), stk(sts_s, 1), stk(sts_s, 2), stk(sts_s, 3), stk(sts_s, 4))
```

```python
import functools
import math

import numpy as np
import jax
import jax.numpy as jnp
from jax import lax
from jax.experimental import pallas as pl
from jax.experimental.pallas import tpu as pltpu

F32 = jnp.float32
BF16 = jnp.bfloat16

H_A = 8
DK_A = 64
DV_A = 128
H_B = 8
DK_B = 64
DV_B = 128
K_CONV = 31
K_FFN = 3
N_BUCKETS = 32
MAX_DIST = 128
EPS = 1e-6
NEG = -1e30

LANES = 128
SUBLANES = 8
VMEM_LIMIT = 56 * 1024 * 1024

_NT = (((1,), (1,)), ((), ()))
_TN = (((0,), (0,)), ((), ()))


def _cparams(sem):
    return pltpu.CompilerParams(dimension_semantics=sem, vmem_limit_bytes=VMEM_LIMIT)


def _pick(n, cands):
    for c in cands:
        if n % c == 0:
            return c
    return n


def _t5_bucket_np(d):
    max_exact = N_BUCKETS // 2
    d = np.maximum(d, 0)
    df = np.maximum(d, 1).astype(np.float32)
    large = max_exact + (np.log(df / np.float32(max_exact)) / np.float32(math.log(MAX_DIST / max_exact))
                         * np.float32(N_BUCKETS - max_exact)).astype(np.int32)
    return np.where(d < max_exact, d, np.minimum(large, N_BUCKETS - 1)).astype(np.int32)


def _bias_from_buckets(idx, rb_ref, h):
    acc = jnp.zeros(idx.shape, F32)
    for b in range(N_BUCKETS):
        acc = jnp.where(idx == b, rb_ref[b, h], acc)
    return acc


def _lam(lv_ref, lam_init):
    lv = lv_ref[...]
    a = jnp.sum(lv[0:1] * lv[1:2], axis=-1, keepdims=True)
    b = jnp.sum(lv[2:3] * lv[3:4], axis=-1, keepdims=True)
    return jnp.exp(a) - jnp.exp(b) + lam_init


def _silu(x):
    return x * jax.nn.sigmoid(x)


def _norm_mm_body(x_ref, g_ref, w_ref, o_ref, h_ref):
    @pl.when(pl.program_id(1) == 0)
    def _():
        x = x_ref[...]
        ms = jnp.mean(x * x, axis=-1, keepdims=True)
        h_ref[...] = ((x * lax.rsqrt(ms + EPS)) * g_ref[...]).astype(BF16)

    o_ref[...] = jnp.dot(h_ref[...], w_ref[...], preferred_element_type=F32)


def norm_matmul(x, g, w, l):
    n, d = x.shape
    c = w.shape[2]
    tm = _pick(n, (1024, 512, 256, 128))
    tn = _pick(c, (1024, 512, 256, 128))
    return pl.pallas_call(
        _norm_mm_body,
        grid=(n // tm, c // tn),
        in_specs=[pl.BlockSpec((tm, d), lambda i, j: (i, 0)),
                  pl.BlockSpec((1, d), lambda i, j: (0, 0)),
                  pl.BlockSpec((None, d, tn), lambda i, j: (l, 0, j))],
        out_specs=pl.BlockSpec((tm, tn), lambda i, j: (i, j)),
        out_shape=jax.ShapeDtypeStruct((n, c), F32),
        scratch_shapes=[pltpu.VMEM((tm, d), BF16)],
        compiler_params=_cparams(("parallel", "arbitrary")),
        name="norm_matmul",
    )(x, g.reshape(1, d), w)


def _ffn_up_body(x_ref, g_ref, wg_ref, wu_ref, cw_ref, cb_ref, st_ref, y_ref, so_ref, h_ref, gbuf_ref, carry_ref,
                 *, tiles_per_seq):
    i = pl.program_id(0)
    j = pl.program_id(1)
    tm = x_ref.shape[0]
    lo = SUBLANES - (K_FFN - 1)

    @pl.when(j == 0)
    def _():
        x = x_ref[...]
        ms = jnp.mean(x * x, axis=-1, keepdims=True)
        h_ref[...] = ((x * lax.rsqrt(ms + EPS)) * g_ref[...]).astype(BF16)

    h = h_ref[...]
    first = (i % tiles_per_seq) == 0

    @pl.when(first)
    def _():
        gbuf_ref[0:lo, :] = jnp.zeros((lo, gbuf_ref.shape[1]), F32)
        gbuf_ref[lo:SUBLANES, :] = st_ref[0]

    @pl.when(jnp.logical_not(first))
    def _():
        gbuf_ref[0:SUBLANES, :] = carry_ref[j]

    gbuf_ref[SUBLANES:SUBLANES + tm, :] = jnp.dot(h, wg_ref[...], preferred_element_type=F32)
    acc = jnp.zeros((tm, gbuf_ref.shape[1]), F32)
    for k in range(K_FFN):
        acc = acc + gbuf_ref[lo + k:lo + k + tm, :] * cw_ref[k:k + 1, :]
    up = jnp.dot(h, wu_ref[...], preferred_element_type=F32)
    y_ref[...] = (_silu(acc + cb_ref[...]) * up).astype(BF16)
    carry_ref[j] = gbuf_ref[tm:tm + SUBLANES, :]
    so_ref[0] = gbuf_ref[lo + tm:SUBLANES + tm, :]


def ffn_up_prompt(x, g, w, l, state, conv_w, conv_b, batch, seq):
    n, d = x.shape
    f = w.shape[2] // 2
    tm = _pick(seq, (1024, 512, 256, 128))
    tn = _pick(f, (512, 256, 128))
    nj = f // tn
    tps = seq // tm
    act, tails = pl.pallas_call(
        functools.partial(_ffn_up_body, tiles_per_seq=tps),
        grid=(n // tm, nj),
        in_specs=[pl.BlockSpec((tm, d), lambda i, j: (i, 0)),
                  pl.BlockSpec((1, d), lambda i, j: (0, 0)),
                  pl.BlockSpec((None, d, tn), lambda i, j: (l, 0, j)),
                  pl.BlockSpec((None, d, tn), lambda i, j: (l, 0, nj + j)),
                  pl.BlockSpec((K_FFN, tn), lambda i, j: (0, j)),
                  pl.BlockSpec((1, tn), lambda i, j: (0, j)),
                  pl.BlockSpec((1, K_FFN - 1, tn), lambda i, j: (i // tps, 0, j))],
        out_specs=[pl.BlockSpec((tm, tn), lambda i, j: (i, j)),
                   pl.BlockSpec((1, K_FFN - 1, tn), lambda i, j: (i, 0, j))],
        out_shape=[jax.ShapeDtypeStruct((n, f), BF16),
                   jax.ShapeDtypeStruct((n // tm, K_FFN - 1, f), F32)],
        scratch_shapes=[pltpu.VMEM((tm, d), BF16),
                        pltpu.VMEM((SUBLANES + tm, tn), F32),
                        pltpu.VMEM((nj, SUBLANES, tn), F32)],
        compiler_params=_cparams(("arbitrary", "arbitrary")),
        name="ffn_up_prompt",
    )(x, g.reshape(1, d), w, w, conv_w, conv_b.reshape(1, f), state)
    return act, tails[tps - 1::tps]


def _mm_res_body(a_ref, w_ref, r_ref, o_ref, acc_ref, *, nk):
    k = pl.program_id(2)
    part = jnp.dot(a_ref[...], w_ref[...], preferred_element_type=F32)
    if nk == 1:
        o_ref[...] = r_ref[...] + part
    else:
        @pl.when(k == 0)
        def _():
            acc_ref[...] = part

        @pl.when(jnp.logical_and(k > 0, k < nk - 1))
        def _():
            acc_ref[...] += part

        @pl.when(k == nk - 1)
        def _():
            o_ref[...] = r_ref[...] + (acc_ref[...] + part)


def matmul_residual(a, w, l, res):
    n, kd = a.shape
    c = w.shape[2]
    tm = _pick(n, (1024, 512, 256, 128))
    tn = _pick(c, (1024, 512, 256, 128))
    tk = kd if kd <= 2048 else _pick(kd, (2816, 2048, 1024, 512))
    nk = kd // tk
    return pl.pallas_call(
        functools.partial(_mm_res_body, nk=nk),
        grid=(n // tm, c // tn, nk),
        in_specs=[pl.BlockSpec((tm, tk), lambda i, j, k: (i, k)),
                  pl.BlockSpec((None, tk, tn), lambda i, j, k: (l, k, j)),
                  pl.BlockSpec((tm, tn), lambda i, j, k: (i, j))],
        out_specs=pl.BlockSpec((tm, tn), lambda i, j, k: (i, j)),
        out_shape=jax.ShapeDtypeStruct((n, c), F32),
        scratch_shapes=[pltpu.VMEM((tm, tn), F32)],
        compiler_params=_cparams(("parallel", "parallel", "arbitrary")),
        name="matmul_residual",
    )(a, w, res)


def _group_rms(x, gain, gm):
    s = x * x
    hi = s.astype(BF16)
    lo = (s - hi.astype(F32)).astype(BF16)
    parts = []
    for c in range(x.shape[1] // LANES):
        sl = slice(c * LANES, (c + 1) * LANES)
        parts.append(jnp.dot(hi[:, sl], gm, preferred_element_type=F32)
                     + jnp.dot(lo[:, sl], gm, preferred_element_type=F32))
    ms = jnp.concatenate(parts, axis=-1) * (1.0 / DK_A)
    return (x * lax.rsqrt(ms + EPS)) * gain


def _attn_prep_sample_body(q_ref, k_ref, v_ref, qg_ref, kg_ref, gm_ref, qo_ref, kf_ref, vf_ref):
    gm = gm_ref[...]
    qo_ref[...] = _group_rms(q_ref[...], qg_ref[...], gm) * (DK_A ** -0.5)
    kf_ref[...] = _group_rms(k_ref[...], kg_ref[...], gm)
    vf_ref[...] = v_ref[...]


def _attn_prep_prompt_body(q_ref, k_ref, v_ref, qg_ref, kg_ref, gm_ref, qt_ref, kf_ref, kb_ref, vf_ref, vt_ref):
    gm = gm_ref[...]
    qn = _group_rms(q_ref[...], qg_ref[...], gm) * (DK_A ** -0.5)
    kn = _group_rms(k_ref[...], kg_ref[...], gm)
    kf_ref[...] = kn
    kb_ref[...] = kn.astype(BF16)
    v = v_ref[...]
    vf_ref[...] = v
    for h in range(H_A):
        hs = slice(h * LANES, (h + 1) * LANES)
        qt_ref[0, h, 0] = qn[:, hs].T.astype(BF16)
        vt_ref[0, h, 0] = v[:, hs].T.astype(BF16)


def _attn_prep_consts(q_gain, k_gain):
    w = H_A * 2 * DK_A
    gm = np.kron(np.eye(LANES // DK_A, dtype=np.float32), np.ones((DK_A, DK_A), np.float32))
    return (jnp.tile(q_gain, w // DK_A).reshape(1, w), jnp.tile(k_gain, w // DK_A).reshape(1, w),
            jnp.asarray(gm, BF16))


def attn_prep_sample(proj, q_gain, k_gain):
    n = proj.shape[0]
    w = H_A * 2 * DK_A
    tm = _pick(n, (512, 256, 128))
    col = lambda c: pl.BlockSpec((tm, w), lambda i, c=c: (i, c))
    cst = lambda shape: pl.BlockSpec(shape, lambda i: (0, 0))
    out = lambda: pl.BlockSpec((tm, w), lambda i: (i, 0))
    return pl.pallas_call(
        _attn_prep_sample_body,
        grid=(n // tm,),
        in_specs=[col(0), col(1), col(2), cst((1, w)), cst((1, w)), cst((LANES, LANES))],
        out_specs=[out(), out(), out()],
        out_shape=[jax.ShapeDtypeStruct((n, w), F32)] * 3,
        compiler_params=_cparams(("parallel",)),
        name="attn_prep_sample",
    )(proj, proj, proj, *_attn_prep_consts(q_gain, k_gain))


def attn_prep_prompt(proj, q_gain, k_gain, batch, seq, t):
    n = batch * seq
    w = H_A * 2 * DK_A
    nt = seq // t
    col = lambda c: pl.BlockSpec((t, w), lambda b, i, c=c: (b * nt + i, c))
    cst = lambda shape: pl.BlockSpec(shape, lambda b, i: (0, 0))
    out = lambda: pl.BlockSpec((t, w), lambda b, i: (b * nt + i, 0))
    tr = lambda: pl.BlockSpec((1, H_A, 1, LANES, t), lambda b, i: (b, 0, i, 0, 0))
    tr_shape = jax.ShapeDtypeStruct((batch, H_A, nt, LANES, t), BF16)
    return pl.pallas_call(
        _attn_prep_prompt_body,
        grid=(batch, nt),
        in_specs=[col(0), col(1), col(2), cst((1, w)), cst((1, w)), cst((LANES, LANES))],
        out_specs=[tr(), out(), out(), out(), tr()],
        out_shape=[tr_shape, jax.ShapeDtypeStruct((n, w), F32), jax.ShapeDtypeStruct((n, w), BF16),
                   jax.ShapeDtypeStruct((n, w), F32), tr_shape],
        compiler_params=_cparams(("parallel", "parallel")),
        name="attn_prep_prompt",
    )(proj, proj, proj, *_attn_prep_consts(q_gain, k_gain))


def _prompt_attn_body(rb_ref, ib_ref, lv_ref, sg_ref, qt_ref, k_ref, vt_ref, o_ref,
                      bias_ref, m_ref, l_ref, acc_ref, *, t, lam_init):
    hg = pl.program_id(1)
    qi = pl.program_id(2)
    heads = range(_ATTN_HEADS)

    @pl.when(qi == 0)
    def _():
        for g in heads:
            h = hg * _ATTN_HEADS + g
            far_bias = rb_ref[N_BUCKETS - 1, h]
            for o in range(2):
                idx = ib_ref[o]
                bias_ref[g, o] = jnp.where(idx >= 0, _bias_from_buckets(idx, rb_ref, h) - far_bias, NEG)

    q2t = []
    for g in heads:
        qt = qt_ref[0, g, 0]
        row = lax.broadcasted_iota(jnp.int32, qt.shape, 0)
        zero = jnp.zeros_like(qt)
        q2t.append(jnp.concatenate([jnp.where(row < DK_A, qt, zero), jnp.where(row >= DK_A, qt, zero)], axis=1))

    m_ref[...] = jnp.full(m_ref.shape, NEG, F32)
    l_ref[...] = jnp.zeros(l_ref.shape, F32)
    acc_ref[...] = jnp.zeros(acc_ref.shape, F32)

    def tile(kj, o):
        off = pl.multiple_of(kj * t, t)
        s = [jnp.dot(k_ref[0, pl.ds(off, t), g * LANES:(g + 1) * LANES], q2t[g], preferred_element_type=F32)
             for g in heads]
        if o is not None:
            s = [s[g] + bias_ref[g, o] for g in heads]
        m_old = [m_ref[g] for g in heads]
        m_new = [jnp.maximum(m_old[g], jnp.max(s[g], axis=0, keepdims=True)) for g in heads]
        alpha = [jnp.exp(m_old[g] - m_new[g]) for g in heads]
        p = [jnp.exp(s[g] - m_new[g]) for g in heads]
        for g in heads:
            l_ref[g] = alpha[g] * l_ref[g] + jnp.sum(p[g], axis=0, keepdims=True)
            m_ref[g] = m_new[g]
        pv = [jnp.dot(vt_ref[0, g, kj], p[g].astype(BF16), preferred_element_type=F32) for g in heads]
        for g in heads:
            acc_ref[g] = alpha[g] * acc_ref[g] + pv[g]

    def far(kj, carry):
        tile(kj, None)
        return carry

    lax.fori_loop(0, jnp.maximum(qi - 1, 0), far, 0)

    @pl.when(qi >= 1)
    def _():
        tile(qi - 1, 1)

    tile(qi, 0)

    lam = _lam(lv_ref, lam_init)
    for g in heads:
        o2t = acc_ref[g] / l_ref[g]
        o = (o2t[:, 0:t] - lam * o2t[:, t:2 * t]).T
        ms = jnp.mean(o * o, axis=-1, keepdims=True)
        y = ((o * lax.rsqrt(ms + EPS)) * sg_ref[...]) * (1.0 - lam_init)
        o_ref[0, :, g * LANES:(g + 1) * LANES] = y.astype(BF16)


_ATTN_TILE = 256
_ATTN_HEADS = 4


def prompt_attn(qt, kb, vt, rel_bias, lam_vec, subln_g, lam_init, batch, seq, t):
    w = H_A * DV_A
    assert t >= MAX_DIST and seq % t == 0
    nq = seq // t
    hgs = _ATTN_HEADS
    key = np.arange(t)[:, None]
    qry = (np.arange(2 * t) % t)[None, :]
    d0 = qry - key
    ib = jnp.asarray(np.stack([np.where(d0 >= 0, _t5_bucket_np(d0), -1), _t5_bucket_np(d0 + t)]), jnp.int32)
    assert int(_t5_bucket_np(np.array([t + 1]))[0]) == N_BUCKETS - 1
    out = pl.pallas_call(
        functools.partial(_prompt_attn_body, t=t, lam_init=lam_init),
        grid=(batch, H_A // hgs, nq),
        in_specs=[pl.BlockSpec(memory_space=pltpu.SMEM),
                  pl.BlockSpec((2, t, 2 * t), lambda b, h, i: (0, 0, 0)),
                  pl.BlockSpec((4, DK_A), lambda b, h, i: (0, 0)),
                  pl.BlockSpec((1, DV_A), lambda b, h, i: (0, 0)),
                  pl.BlockSpec((1, hgs, 1, LANES, t), lambda b, h, i: (b, h, i, 0, 0)),
                  pl.BlockSpec((1, seq, hgs * LANES), lambda b, h, i: (b, 0, h)),
                  pl.BlockSpec((1, hgs, nq, LANES, t), lambda b, h, i: (b, h, 0, 0, 0))],
        out_specs=pl.BlockSpec((1, t, hgs * LANES), lambda b, h, i: (b, i, h)),
        out_shape=jax.ShapeDtypeStruct((batch, seq, w), BF16),
        scratch_shapes=[pltpu.VMEM((hgs, 2, t, 2 * t), F32),
                        pltpu.VMEM((hgs, 1, 2 * t), F32),
                        pltpu.VMEM((hgs, 1, 2 * t), F32),
                        pltpu.VMEM((hgs, DV_A, 2 * t), F32)],
        compiler_params=_cparams(("arbitrary", "arbitrary", "arbitrary")),
        name="prompt_attn",
    )(rel_bias, ib, lam_vec, subln_g.reshape(1, DV_A), qt, kb.reshape(batch, seq, w), vt)
    return out.reshape(batch * seq, w)


_TQ = SUBLANES
_RH = 2 * _TQ


_KPAD = LANES // H_A


def _sample_attn_body(pt_ref, rb_ref, ibp_ref, ibn_ref, lv_ref, sg_ref, q_ref, kn_ref, vn_ref, *rest,
                      pps, nsteps, lam_init):
    k_refs = rest[0:pps]
    v_refs = rest[pps:2 * pps]
    o_ref = rest[2 * pps]
    biasp_ref, biasn_ref, m_ref, l_ref, acc_ref = rest[2 * pps + 1:]
    b = pl.program_id(0)
    s_id = pl.program_id(1)

    @pl.when(jnp.logical_and(b == 0, s_id == 0))
    def _():
        for h in range(H_A):
            sl = slice(h * _RH, (h + 1) * _RH)
            far_bias = rb_ref[N_BUCKETS - 1, h]
            for o in range(2):
                idx = ibp_ref[o, sl, :]
                biasp_ref[o, sl, :] = jnp.where(idx >= 0, _bias_from_buckets(idx, rb_ref, h) - far_bias, NEG)
            idx = ibn_ref[sl, :]
            biasn_ref[sl, :] = jnp.where(idx >= 0, _bias_from_buckets(idx, rb_ref, h) - far_bias, NEG)

    @pl.when(s_id == 0)
    def _():
        m_ref[...] = jnp.full(m_ref.shape, NEG, F32)
        l_ref[...] = jnp.zeros(l_ref.shape, F32)
        acc_ref[...] = jnp.zeros(acc_ref.shape, F32)

    q = q_ref[0]
    lane = lax.broadcasted_iota(jnp.int32, (_TQ, LANES), 1)
    blocks = []
    for h in range(H_A):
        qh = q[:, h * LANES:(h + 1) * LANES]
        blocks += [jnp.where(lane < DK_A, qh, 0.0), jnp.where(lane >= DK_A, qh, 0.0)]
    q2 = jnp.concatenate(blocks, axis=0).astype(BF16)

    def attend(kt, vt, bias):
        s = lax.dot_general(q2, kt, _NT, preferred_element_type=F32) + bias
        m_old = m_ref[...]
        m_new = jnp.maximum(m_old, jnp.max(s, axis=-1, keepdims=True))
        alpha = jnp.exp(m_old - m_new)
        p = jnp.exp(s - m_new)
        l_ref[...] = alpha * l_ref[...] + jnp.sum(p, axis=-1, keepdims=True)
        acc_ref[...] = alpha * acc_ref[...] + jnp.dot(p.astype(BF16), vt, preferred_element_type=F32)
        m_ref[...] = m_new

    kt = jnp.concatenate([r[0, 0].astype(BF16) for r in k_refs], axis=0)
    vt = jnp.concatenate([r[0, 0].astype(BF16) for r in v_refs], axis=0)
    last = s_id == nsteps - 1
    bias = jnp.concatenate([biasp_ref[0]] * (pps - 1) + [biasp_ref[jnp.where(last, 1, 0)]], axis=1)
    attend(kt, vt, bias)

    @pl.when(last)
    def _():
        attend(kn_ref[0].astype(BF16), vn_ref[0].astype(BF16), biasn_ref[...])
        o2 = acc_ref[...] / l_ref[...]
        lam = _lam(lv_ref, lam_init)
        for h in range(H_A):
            o = o2[h * _RH:h * _RH + _TQ] - lam * o2[h * _RH + _TQ:(h + 1) * _RH]
            ms = jnp.mean(o * o, axis=-1, keepdims=True)
            y = ((o * lax.rsqrt(ms + EPS)) * sg_ref[...]) * (1.0 - lam_init)
            o_ref[0, :, h * LANES:(h + 1) * LANES] = y


def sample_attn(qn, kf, vf, cache_k, cache_v, layer, page_table, rel_bias, lam_vec, subln_g, lam_init, db, t_new):
    w = H_A * DV_A
    depth, n_pool, page = cache_k.shape[:3]
    n_pages = page_table.shape[1]
    assert t_new <= _TQ and t_new <= _KPAD and page >= MAX_DIST
    pps = _pick(n_pages, (8, 4, 2, 1))
    nsteps = n_pages // pps
    rows = H_A * _RH
    kw = page * H_A
    rh = (np.arange(rows) // _RH)[:, None]
    rt = np.minimum(np.arange(rows) % _TQ, t_new - 1)[:, None]
    ck_key, ck_head = (np.arange(kw) // H_A)[None, :], (np.arange(kw) % H_A)[None, :]
    ib_far = np.where(rh == ck_head, N_BUCKETS - 1, -1)
    ib_last = np.where(rh == ck_head, _t5_bucket_np(page + rt - ck_key), -1)
    nk_key, nk_head = (np.arange(_KPAD * H_A) // H_A)[None, :], (np.arange(_KPAD * H_A) % H_A)[None, :]
    d_new = rt - nk_key
    ib_new = np.where((rh == nk_head) & (d_new >= 0) & (nk_key < t_new), _t5_bucket_np(d_new), -1)
    ibp = jnp.asarray(np.stack([ib_far, ib_last]), jnp.int32)
    ibn = jnp.asarray(ib_new, jnp.int32)
    assert int(_t5_bucket_np(np.array([page + 1]))[0]) == N_BUCKETS - 1

    q_pad = jnp.pad(qn.reshape(db, t_new, w), ((0, 0), (0, _TQ - t_new), (0, 0)))

    def new_rows(x):
        return jnp.pad(x.reshape(db, t_new * H_A, LANES), ((0, 0), (0, (_KPAD - t_new) * H_A), (0, 0)))

    ck = cache_k.reshape(depth, n_pool, kw, LANES)
    cv = cache_v.reshape(depth, n_pool, kw, LANES)
    new_spec = pl.BlockSpec((1, _KPAD * H_A, LANES), lambda b, s, pt: (b, 0, 0))

    def page_spec(i):
        return pl.BlockSpec((1, 1, kw, LANES), lambda b, s, pt, i=i: (layer, pt[b, s * pps + i], 0, 0))

    grid_spec = pltpu.PrefetchScalarGridSpec(
        num_scalar_prefetch=1,
        grid=(db, nsteps),
        in_specs=[pl.BlockSpec(memory_space=pltpu.SMEM),
                  pl.BlockSpec((2, rows, kw), lambda b, s, pt: (0, 0, 0)),
                  pl.BlockSpec((rows, _KPAD * H_A), lambda b, s, pt: (0, 0)),
                  pl.BlockSpec((4, DK_A), lambda b, s, pt: (0, 0)),
                  pl.BlockSpec((1, DV_A), lambda b, s, pt: (0, 0)),
                  pl.BlockSpec((1, _TQ, w), lambda b, s, pt: (b, 0, 0)),
                  new_spec, new_spec]
                 + [page_spec(i) for i in range(pps)] + [page_spec(i) for i in range(pps)],
        out_specs=pl.BlockSpec((1, _TQ, w), lambda b, s, pt: (b, 0, 0)),
        scratch_shapes=[pltpu.VMEM((2, rows, kw), F32),
                        pltpu.VMEM((rows, _KPAD * H_A), F32),
                        pltpu.VMEM((rows, 1), F32),
                        pltpu.VMEM((rows, 1), F32),
                        pltpu.VMEM((rows, DV_A), F32)],
    )
    out = pl.pallas_call(
        functools.partial(_sample_attn_body, pps=pps, nsteps=nsteps, lam_init=lam_init),
        grid_spec=grid_spec,
        out_shape=jax.ShapeDtypeStruct((db, _TQ, w), F32),
        compiler_params=_cparams(("arbitrary", "arbitrary")),
        name="sample_attn",
    )(page_table, rel_bias, ibp, ibn, lam_vec, subln_g.reshape(1, DV_A), q_pad, new_rows(kf), new_rows(vf),
      *([ck] * pps), *([cv] * pps))
    return out[:, :t_new].reshape(db * t_new, w).astype(BF16)


def _log_gamma(h):
    return float(np.log1p(-np.exp2(np.float32(-5.0 - h), dtype=np.float32), dtype=np.float32))


def _swap_halves(x):
    n = x.shape[1]
    half = DK_B // 2
    lane = lax.broadcasted_iota(jnp.int32, x.shape, 1)
    return jnp.where((lane % DK_B) < half, pltpu.roll(x, n - half, 1), pltpu.roll(x, half, 1))


def _rotary(x, cos, sin):
    return x * cos + _swap_halves(x) * sin


def _ret_tile(q, k, v, c):
    r = q.shape[0]
    shift = int(math.log2(c))
    assert 1 << shift == c
    ri = lax.broadcasted_iota(jnp.int32, (r, r), 0)
    ci = lax.broadcasted_iota(jnp.int32, (r, r), 1)
    same = (ri >> shift) == (ci >> shift)
    diff = ((ri & (c - 1)) - (ci & (c - 1))).astype(F32)
    valid = jnp.logical_and(same, diff >= 0.0)
    ti = (lax.broadcasted_iota(jnp.int32, (r, 1), 0) & (c - 1)).astype(F32)
    lane = lax.broadcasted_iota(jnp.int32, (r, LANES), 1)
    first = lane < DK_B
    o_inner, q_m, dec_q, k_dec = [], [], [], []
    for p in range(H_B // 2):
        ps = slice(p * LANES, (p + 1) * LANES)
        qp, kp = q[:, ps], k[:, ps]
        kpb = kp.astype(BF16)
        lg0, lg1 = _log_gamma(2 * p), _log_gamma(2 * p + 1)
        k_dec.append((kp * jnp.where(first, jnp.exp((c - 1.0 - ti) * lg0), jnp.exp((c - 1.0 - ti) * lg1))).astype(BF16))
        for hh in range(2):
            h = 2 * p + hh
            lg = lg1 if hh else lg0
            qm = jnp.where(first if hh == 0 else jnp.logical_not(first), qp, 0.0).astype(BF16)
            dmat = jnp.where(valid, jnp.exp(jnp.maximum(diff, 0.0) * lg), 0.0)
            inner = lax.dot_general(qm, kpb, _NT, preferred_element_type=F32) * dmat
            vh = v[:, h * DV_B:(h + 1) * DV_B].astype(BF16)
            o_inner.append(jnp.dot(inner.astype(BF16), vh, preferred_element_type=F32))
            q_m.append(qm)
            dec_q.append(jnp.exp((ti + 1.0) * lg))
    return o_inner, q_m, dec_q, k_dec


def _state_update(s_pair, kd, v, p, c):
    row = lax.broadcasted_iota(jnp.int32, (LANES, DV_B), 0)
    top = row < DK_B
    u0 = lax.dot_general(kd, v[:, (2 * p) * DV_B:(2 * p + 1) * DV_B].astype(BF16), _TN, preferred_element_type=F32)
    u1 = lax.dot_general(kd, v[:, (2 * p + 1) * DV_B:(2 * p + 2) * DV_B].astype(BF16), _TN, preferred_element_type=F32)
    gc = jnp.where(top, math.exp(c * _log_gamma(2 * p)), math.exp(c * _log_gamma(2 * p + 1)))
    return s_pair * gc + jnp.where(top, u0, u1)


def _ret_epilogue(o, g, ng):
    ms = jnp.mean(o * o, axis=-1, keepdims=True)
    return ((o * lax.rsqrt(ms + EPS)) * ng) * _silu(g)


def _ret_prompt_body(q_ref, k_ref, v_ref, g_ref, cos_ref, sin_ref, ng_ref, y_ref, so_ref, s_ref, *, c):
    ci = pl.program_id(1)

    @pl.when(ci == 0)
    def _():
        s_ref[...] = jnp.zeros(s_ref.shape, F32)

    cos, sin = cos_ref[...], sin_ref[...]
    q = _rotary(q_ref[...], cos, sin)
    k = _rotary(k_ref[...], cos, sin) * (DK_B ** -0.5)
    v = v_ref[...]
    o_inner, q_m, dec_q, k_dec = _ret_tile(q, k, v, c)
    ng = ng_ref[...]
    for p in range(H_B // 2):
        s_pair = s_ref[p]
        sb = s_pair.astype(BF16)
        for hh in range(2):
            h = 2 * p + hh
            o = o_inner[h] + jnp.dot(q_m[h], sb, preferred_element_type=F32) * dec_q[h]
            hs = slice(h * DV_B, (h + 1) * DV_B)
            y_ref[:, hs] = _ret_epilogue(o, g_ref[:, hs], ng).astype(BF16)
        s_ref[p] = _state_update(s_pair, k_dec[p], v, p, c)

    @pl.when(ci == pl.num_programs(1) - 1)
    def _():
        so_ref[0] = s_ref[...]


def retention_prompt(proj, cos, sin, ret_norm_g, batch, seq, col0):
    n = batch * seq
    c = 128 if seq % 128 == 0 else seq
    nc = seq // c
    wq = H_B * DK_B
    wv = H_B * DV_B
    assert col0 % wv == 0
    qc, vc = col0 // wq, col0 // wv
    row = lambda b, i: b * nc + i
    y, st = pl.pallas_call(
        functools.partial(_ret_prompt_body, c=c),
        grid=(batch, nc),
        in_specs=[pl.BlockSpec((c, wq), lambda b, i: (row(b, i), qc)),
                  pl.BlockSpec((c, wq), lambda b, i: (row(b, i), qc + 1)),
                  pl.BlockSpec((c, wv), lambda b, i: (row(b, i), vc + 1)),
                  pl.BlockSpec((c, wv), lambda b, i: (row(b, i), vc + 2)),
                  pl.BlockSpec((c, wq), lambda b, i: (i, 0)),
                  pl.BlockSpec((c, wq), lambda b, i: (i, 0)),
                  pl.BlockSpec((1, DV_B), lambda b, i: (0, 0))],
        out_specs=[pl.BlockSpec((c, wv), lambda b, i: (row(b, i), 0)),
                   pl.BlockSpec((1, H_B // 2, LANES, DV_B), lambda b, i: (b, 0, 0, 0))],
        out_shape=[jax.ShapeDtypeStruct((n, wv), BF16),
                   jax.ShapeDtypeStruct((batch, H_B // 2, LANES, DV_B), F32)],
        scratch_shapes=[pltpu.VMEM((H_B // 2, LANES, DV_B), F32)],
        compiler_params=_cparams(("arbitrary", "arbitrary")),
        name="retention_prompt",
    )(proj, proj, proj, proj, cos, sin, ret_norm_g.reshape(1, DV_B))
    return y, st.reshape(batch, H_B, DK_B, DV_B)


def _ret_sample_body(q_ref, k_ref, v_ref, g_ref, cos_ref, sin_ref, ng_ref, s0_ref, y_ref, so_ref,
                     o_ref, qm_ref, kd_ref, *, c):
    gi = pl.program_id(0)
    r = q_ref.shape[0]
    shift = int(math.log2(c))
    v = v_ref[...]

    @pl.when(gi == 0)
    def _():
        cos, sin = cos_ref[...], sin_ref[...]
        q = _rotary(q_ref[...], cos, sin)
        k = _rotary(k_ref[...], cos, sin) * (DK_B ** -0.5)
        o_inner, q_m, dec_q, k_dec = _ret_tile(q, k, v, c)
        for h in range(H_B):
            o_ref[:, h * DV_B:(h + 1) * DV_B] = o_inner[h]
            qm_ref[h] = q_m[h]
        for p in range(H_B // 2):
            kd_ref[p] = k_dec[p]

    rsel = (lax.broadcasted_iota(jnp.int32, (r, LANES), 0) >> shift) == gi
    ti = (lax.broadcasted_iota(jnp.int32, (r, 1), 0) & (c - 1)).astype(F32)
    zero = jnp.zeros((r, LANES), BF16)
    for p in range(H_B // 2):
        s_pair = s0_ref[0, p]
        sb = s_pair.astype(BF16)
        for hh in range(2):
            h = 2 * p + hh
            qm = jnp.where(rsel, qm_ref[h], zero)
            hs = slice(h * DV_B, (h + 1) * DV_B)
            o_ref[:, hs] += jnp.dot(qm, sb, preferred_element_type=F32) * jnp.exp((ti + 1.0) * _log_gamma(h))
        kd = jnp.where(rsel, kd_ref[p], zero)
        so_ref[0, p] = _state_update(s_pair, kd, v, p, c)

    @pl.when(gi == pl.num_programs(0) - 1)
    def _():
        ng = ng_ref[...]
        for h in range(H_B):
            hs = slice(h * DV_B, (h + 1) * DV_B)
            y_ref[:, hs] = _ret_epilogue(o_ref[:, hs], g_ref[:, hs], ng).astype(BF16)


def retention_sample(proj, cos, sin, ret_norm_g, state0, db, t_new, col0):
    n = db * t_new
    wq = H_B * DK_B
    wv = H_B * DV_B
    qc, vc = col0 // wq, col0 // wv
    s0 = state0.reshape(db, H_B // 2, LANES, DV_B)
    st_spec = pl.BlockSpec((1, H_B // 2, LANES, DV_B), lambda g: (g, 0, 0, 0))
    y, st = pl.pallas_call(
        functools.partial(_ret_sample_body, c=t_new),
        grid=(db,),
        in_specs=[pl.BlockSpec((n, wq), lambda g: (0, qc)),
                  pl.BlockSpec((n, wq), lambda g: (0, qc + 1)),
                  pl.BlockSpec((n, wv), lambda g: (0, vc + 1)),
                  pl.BlockSpec((n, wv), lambda g: (0, vc + 2)),
                  pl.BlockSpec((n, wq), lambda g: (0, 0)),
                  pl.BlockSpec((n, wq), lambda g: (0, 0)),
                  pl.BlockSpec((1, DV_B), lambda g: (0, 0)),
                  st_spec],
        out_specs=[pl.BlockSpec((n, wv), lambda g: (0, 0)), st_spec],
        out_shape=[jax.ShapeDtypeStruct((n, wv), BF16),
                   jax.ShapeDtypeStruct((db, H_B // 2, LANES, DV_B), F32)],
        scratch_shapes=[pltpu.VMEM((n, wv), F32),
                        pltpu.VMEM((H_B, n, LANES), BF16),
                        pltpu.VMEM((H_B // 2, n, LANES), BF16)],
        compiler_params=_cparams(("arbitrary",)),
        name="retention_sample",
    )(proj, proj, proj, proj, cos, sin, ret_norm_g.reshape(1, DV_B), s0)
    return y, st.reshape(db, H_B, DK_B, DV_B)


def _rotary_tables(pos):
    half = DK_B // 2
    inv = 1.0 / (10000.0 ** (jnp.arange(half, dtype=F32) / half))
    ang = pos.astype(F32)[:, None] * inv[None, :]
    cos, sin = jnp.cos(ang), jnp.sin(ang)
    cos_t = jnp.tile(jnp.concatenate([cos, cos], axis=-1), (1, H_B))
    sin_t = jnp.tile(jnp.concatenate([-sin, sin], axis=-1), (1, H_B))
    return cos_t, sin_t


_HALO = 32


def _conv_body(a_ref, b_ref, st_ref, w_ref, cb_ref, lg_ref, lb_ref, y_ref, so_ref, full_ref, *, ts):
    ti = pl.program_id(1)
    lo = _HALO - (K_CONV - 1)

    @pl.when(ti == 0)
    def _():
        full_ref[0:lo, :] = jnp.zeros((lo, full_ref.shape[1]), F32)
        full_ref[lo:_HALO, :] = st_ref[0]

    full_ref[_HALO:_HALO + ts, :] = a_ref[0] * jax.nn.sigmoid(b_ref[0])
    acc = jnp.zeros((ts, full_ref.shape[1]), F32)
    for j in range(K_CONV):
        acc = acc + full_ref[lo + j:lo + j + ts, :] * w_ref[j:j + 1, :]
    uc = acc + cb_ref[...]
    mu = jnp.mean(uc, axis=-1, keepdims=True)
    dev = uc - mu
    var = jnp.mean(dev * dev, axis=-1, keepdims=True)
    y = (dev * lax.rsqrt(var + EPS)) * lg_ref[...] + lb_ref[...]
    y_ref[0] = _silu(y).astype(BF16)
    so_ref[0] = full_ref[lo + ts:_HALO + ts, :]
    if ts >= _HALO:
        full_ref[0:_HALO, :] = full_ref[ts:ts + _HALO, :]


def conv_module(proj3, state, conv_w, conv_b, ln_g, ln_b, col0):
    b, s, _ = proj3.shape
    c = conv_w.shape[1]
    ts = _pick(s, (256, 128)) if s >= 128 else s
    nt = s // ts
    assert nt == 1 or ts >= _HALO
    cc = col0 // c
    vec = lambda: pl.BlockSpec((1, c), lambda i, t: (0, 0))
    return pl.pallas_call(
        functools.partial(_conv_body, ts=ts),
        grid=(b, nt),
        in_specs=[pl.BlockSpec((1, ts, c), lambda i, t: (i, t, cc)),
                  pl.BlockSpec((1, ts, c), lambda i, t: (i, t, cc + 1)),
                  pl.BlockSpec((1, K_CONV - 1, c), lambda i, t: (i, 0, 0)),
                  pl.BlockSpec((K_CONV, c), lambda i, t: (0, 0)),
                  vec(), vec(), vec()],
        out_specs=[pl.BlockSpec((1, ts, c), lambda i, t: (i, t, 0)),
                   pl.BlockSpec((1, K_CONV - 1, c), lambda i, t: (i, 0, 0))],
        out_shape=[jax.ShapeDtypeStruct((b, s, c), BF16),
                   jax.ShapeDtypeStruct((b, K_CONV - 1, c), F32)],
        scratch_shapes=[pltpu.VMEM((_HALO + ts, c), F32)],
        compiler_params=_cparams(("arbitrary", "arbitrary")),
        name="conv_module",
    )(proj3, proj3, state, conv_w, conv_b.reshape(1, c), ln_g.reshape(1, c), ln_b.reshape(1, c))


def _merge_body(a_ref, b_ref, c_ref, wa_ref, wb_ref, wc_ref, ga_ref, gb_ref, gc_ref,
                ba_ref, bb_ref, bc_ref, o_ref):
    ya = jnp.dot(a_ref[...], wa_ref[...], preferred_element_type=F32)
    m = jax.nn.sigmoid(ga_ref[...] + ba_ref[...]) * ya
    yb = jnp.dot(b_ref[...], wb_ref[...], preferred_element_type=F32)
    m = m + jax.nn.sigmoid(gb_ref[...] + bb_ref[...]) * yb
    yc = jnp.dot(c_ref[...], wc_ref[...], preferred_element_type=F32)
    m = m + jax.nn.sigmoid(gc_ref[...] + bc_ref[...]) * yc
    o_ref[...] = m.astype(BF16)


def branch_merge(a, b, c, wa, wb, wc, l, proj, b_gate, col0):
    n, kd = a.shape
    d = wa.shape[2]
    tm = _pick(n, (512, 256, 128))
    tn = _pick(d, (1024, 512, 256, 128))
    nj = d // tn
    g0 = col0 // tn
    bg = b_gate.reshape(1, 3 * d)
    x_spec = lambda: pl.BlockSpec((tm, kd), lambda i, j: (i, 0))
    w_spec = lambda: pl.BlockSpec((None, kd, tn), lambda i, j: (l, 0, j))
    g_spec = lambda k: pl.BlockSpec((tm, tn), lambda i, j, k=k: (i, g0 + k * nj + j))
    bias_spec = lambda k: pl.BlockSpec((1, tn), lambda i, j, k=k: (0, k * nj + j))
    return pl.pallas_call(
        _merge_body,
        grid=(n // tm, nj),
        in_specs=[x_spec(), x_spec(), x_spec(), w_spec(), w_spec(), w_spec(),
                  g_spec(0), g_spec(1), g_spec(2), bias_spec(0), bias_spec(1), bias_spec(2)],
        out_specs=pl.BlockSpec((tm, tn), lambda i, j: (i, j)),
        out_shape=jax.ShapeDtypeStruct((n, d), BF16),
        compiler_params=_cparams(("parallel", "arbitrary")),
        name="branch_merge",
    )(a, b, c, wa, wb, wc, proj, proj, proj, bg, bg, bg)


def _ffn_act_body(g_ref, u_ref, st_ref, w_ref, cb_ref, y_ref, so_ref, full_ref, *, ts):
    ti = pl.program_id(2)
    lo = SUBLANES - (K_FFN - 1)

    @pl.when(ti == 0)
    def _():
        full_ref[0:lo, :] = jnp.zeros((lo, full_ref.shape[1]), F32)
        full_ref[lo:SUBLANES, :] = st_ref[0]

    full_ref[SUBLANES:SUBLANES + ts, :] = g_ref[0]
    acc = jnp.zeros((ts, full_ref.shape[1]), F32)
    for j in range(K_FFN):
        acc = acc + full_ref[lo + j:lo + j + ts, :] * w_ref[j:j + 1, :]
    y_ref[0] = (_silu(acc + cb_ref[...]) * u_ref[0]).astype(BF16)
    so_ref[0] = full_ref[lo + ts:SUBLANES + ts, :]
    if ts >= SUBLANES:
        full_ref[0:SUBLANES, :] = full_ref[ts:ts + SUBLANES, :]


def ffn_act(h3, state, conv_w, conv_b):
    b, s, f2 = h3.shape
    f = f2 // 2
    ts = _pick(s, (512, 256, 128)) if s >= 128 else s
    tc = _pick(f, (512, 256, 128)) if ts >= 128 else f
    nc = f // tc
    nt = s // ts
    assert nt == 1 or ts >= SUBLANES
    return pl.pallas_call(
        functools.partial(_ffn_act_body, ts=ts),
        grid=(b, nc, nt),
        in_specs=[pl.BlockSpec((1, ts, tc), lambda i, c, t: (i, t, c)),
                  pl.BlockSpec((1, ts, tc), lambda i, c, t: (i, t, nc + c)),
                  pl.BlockSpec((1, K_FFN - 1, tc), lambda i, c, t: (i, 0, c)),
                  pl.BlockSpec((K_FFN, tc), lambda i, c, t: (0, c)),
                  pl.BlockSpec((1, tc), lambda i, c, t: (0, c))],
        out_specs=[pl.BlockSpec((1, ts, tc), lambda i, c, t: (i, t, c)),
                   pl.BlockSpec((1, K_FFN - 1, tc), lambda i, c, t: (i, 0, c))],
        out_shape=[jax.ShapeDtypeStruct((b, s, f), BF16),
                   jax.ShapeDtypeStruct((b, K_FFN - 1, f), F32)],
        scratch_shapes=[pltpu.VMEM((SUBLANES + ts, tc), F32)],
        compiler_params=_cparams(("arbitrary", "arbitrary", "arbitrary")),
        name="ffn_act",
    )(h3, h3, state, conv_w, conv_b.reshape(1, f))


def _layer(x, batch, seq, pos, attn_fn, ret_fn, conv0, ffn0, lw, l, fuse_ffn):
    n, d = x.shape
    a_cols = H_A * (4 * DK_A + DV_A)
    b_cols = H_B * (2 * DK_B + 2 * DV_B)
    c_conv = lw['conv_w'].shape[1]
    proj = norm_matmul(x, lw['norm1_g'], lw['w_in'], l)
    a_in, kf, vf = attn_fn(proj)
    cos, sin = _rotary_tables(pos)
    b_in, ret_new = ret_fn(proj, cos, sin, a_cols)
    c_in, conv_new = conv_module(proj.reshape(batch, seq, -1), conv0, lw['conv_w'], lw['conv_b'],
                                 lw['conv_ln_g'], lw['conv_ln_b'], a_cols + b_cols)
    c_in = c_in.reshape(n, c_conv)
    m = branch_merge(a_in, b_in, c_in, lw['w_o_a'], lw['w_o_b'], lw['w_o_c'], l, proj, lw['b_gate'],
                     a_cols + b_cols + 2 * c_conv)
    x = matmul_residual(m, lw['w_out'], l, x)
    if fuse_ffn:
        act, ffn_new = ffn_up_prompt(x, lw['norm2_g'], lw['w_ffn_in'], l, ffn0, lw['ffn_conv_w'], lw['ffn_conv_b'],
                                     batch, seq)
    else:
        h2 = norm_matmul(x, lw['norm2_g'], lw['w_ffn_in'], l)
        act, ffn_new = ffn_act(h2.reshape(batch, seq, -1), ffn0, lw['ffn_conv_w'], lw['ffn_conv_b'])
        act = act.reshape(n, -1)
    x = matmul_residual(act, lw['w_ffn_down'], l, x)
    states = (kf.reshape(batch, seq, H_A, 2 * DK_A), vf.reshape(batch, seq, H_A, DV_A), ret_new, conv_new, ffn_new)
    return x, states


def kernel(x_prompt, x_sample, cache_k, cache_v, state_ret, state_conv, state_ffn, page_table, rel_bias, norm1_g, w_in, b_gate, q_norm_g, k_norm_g, lam_vec, subln_g, w_o_a, ret_norm_g, w_o_b, conv_w, conv_b, conv_ln_g, conv_ln_b, w_o_c, w_out, norm2_g, w_ffn_in, ffn_conv_w, ffn_conv_b, w_ffn_down):
    bp, sp, d = x_prompt.shape
    db, t_new, _ = x_sample.shape
    depth = w_in.shape[0]
    past = page_table.shape[1] * cache_k.shape[2]
    pos_p = jnp.arange(sp)
    pos_s = jnp.tile(past + jnp.arange(t_new), db)
    yp = x_prompt.reshape(bp * sp, d)
    ys = x_sample.reshape(db * t_new, d)
    c_conv = conv_w.shape[2]
    d_ff = ffn_conv_w.shape[2]
    zero_conv = jnp.zeros((bp, K_CONV - 1, c_conv), F32)
    zero_ffn = jnp.zeros((bp, K_FFN - 1, d_ff), F32)
    sts_p, sts_s = [], []
    wb = {'w_in': w_in.astype(BF16), 'w_o_a': w_o_a.astype(BF16), 'w_o_b': w_o_b.astype(BF16),
          'w_o_c': w_o_c.astype(BF16), 'w_out': w_out.astype(BF16), 'w_ffn_in': w_ffn_in.astype(BF16),
          'w_ffn_down': w_ffn_down.astype(BF16)}
    for l in range(depth):
        lw = dict(wb)
        lw.update({'norm1_g': norm1_g[l], 'b_gate': b_gate[l], 'q_norm_g': q_norm_g[l], 'k_norm_g': k_norm_g[l],
                   'conv_w': conv_w[l], 'conv_b': conv_b[l], 'conv_ln_g': conv_ln_g[l], 'conv_ln_b': conv_ln_b[l],
                   'norm2_g': norm2_g[l], 'ffn_conv_w': ffn_conv_w[l], 'ffn_conv_b': ffn_conv_b[l]})
        lam_init = 0.8 - 0.6 * math.exp(-0.3 * l)

        def p_attn(proj, l=l, lam_init=lam_init, lw=lw):
            t = _pick(sp, (_ATTN_TILE, LANES))
            qt, kf, kb, vf, vt = attn_prep_prompt(proj, lw['q_norm_g'], lw['k_norm_g'], bp, sp, t)
            return prompt_attn(qt, kb, vt, rel_bias, lam_vec[l], subln_g[l], lam_init, bp, sp, t), kf, vf

        def p_ret(proj, cos, sin, col0, l=l):
            return retention_prompt(proj, cos, sin, ret_norm_g[l], bp, sp, col0)

        yp, st = _layer(yp, bp, sp, pos_p, p_attn, p_ret, zero_conv, zero_ffn, lw, l, True)
        sts_p.append(st)

        def s_attn(proj, l=l, lam_init=lam_init, lw=lw):
            qn, kf, vf = attn_prep_sample(proj, lw['q_norm_g'], lw['k_norm_g'])
            return sample_attn(qn, kf, vf, cache_k, cache_v, l, page_table, rel_bias, lam_vec[l], subln_g[l],
                               lam_init, db, t_new), kf, vf

        def s_ret(proj, cos, sin, col0, l=l):
            return retention_sample(proj, cos, sin, ret_norm_g[l], state_ret[l], db, t_new, col0)

        ys, st = _layer(ys, db, t_new, pos_s, s_attn, s_ret, state_conv[l], state_ffn[l], lw, l, False)
        sts_s.append(st)

    def stk(sts, i):
        return jnp.stack([st[i] for st in sts], axis=0)

    return (yp.reshape(bp, sp, d), ys.reshape(db, t_new, d),
            stk(sts_p, 0), stk(sts_p, 1), stk(sts_p, 2), stk(sts_p, 3), stk(sts_p, 4),
            stk(sts_s, 0), stk(sts_s, 1), stk(sts_s, 2), stk(sts_s, 3), stk(sts_s, 4))
```

```python
import functools
import math

import numpy as np
import jax
import jax.numpy as jnp
from jax import lax
from jax.experimental import pallas as pl
from jax.experimental.pallas import tpu as pltpu

F32 = jnp.float32
BF16 = jnp.bfloat16

H_A = 8
DK_A = 64
DV_A = 128
H_B = 8
DK_B = 64
DV_B = 128
K_CONV = 31
K_FFN = 3
N_BUCKETS = 32
MAX_DIST = 128
EPS = 1e-6
NEG = -1e30

LANES = 128
SUBLANES = 8
VMEM_LIMIT = 56 * 1024 * 1024

_NT = (((1,), (1,)), ((), ()))
_TN = (((0,), (0,)), ((), ()))


def _cparams(sem):
    return pltpu.CompilerParams(dimension_semantics=sem, vmem_limit_bytes=VMEM_LIMIT)


def _pick(n, cands):
    for c in cands:
        if n % c == 0:
            return c
    return n


def _t5_bucket_np(d):
    max_exact = N_BUCKETS // 2
    d = np.maximum(d, 0)
    df = np.maximum(d, 1).astype(np.float32)
    large = max_exact + (np.log(df / np.float32(max_exact)) / np.float32(math.log(MAX_DIST / max_exact))
                         * np.float32(N_BUCKETS - max_exact)).astype(np.int32)
    return np.where(d < max_exact, d, np.minimum(large, N_BUCKETS - 1)).astype(np.int32)


def _bias_from_buckets(idx, rb_ref, h):
    acc = jnp.zeros(idx.shape, F32)
    for b in range(N_BUCKETS):
        acc = jnp.where(idx == b, rb_ref[b, h], acc)
    return acc


def _lam(lv_ref, lam_init):
    lv = lv_ref[...]
    a = jnp.sum(lv[0:1] * lv[1:2], axis=-1, keepdims=True)
    b = jnp.sum(lv[2:3] * lv[3:4], axis=-1, keepdims=True)
    return jnp.exp(a) - jnp.exp(b) + lam_init


def _silu(x):
    return x * jax.nn.sigmoid(x)


def _norm_mm_body(x_ref, g_ref, w_ref, o_ref, h_ref):
    @pl.when(pl.program_id(1) == 0)
    def _():
        x = x_ref[...]
        ms = jnp.mean(x * x, axis=-1, keepdims=True)
        h_ref[...] = ((x * lax.rsqrt(ms + EPS)) * g_ref[...]).astype(BF16)

    o_ref[...] = jnp.dot(h_ref[...], w_ref[...], preferred_element_type=F32)


def norm_matmul(x, g, w, l):
    n, d = x.shape
    c = w.shape[2]
    tm = _pick(n, (1024, 512, 256, 128))
    tn = _pick(c, (1024, 512, 256, 128))
    return pl.pallas_call(
        _norm_mm_body,
        grid=(n // tm, c // tn),
        in_specs=[pl.BlockSpec((tm, d), lambda i, j: (i, 0)),
                  pl.BlockSpec((1, d), lambda i, j: (0, 0)),
                  pl.BlockSpec((None, d, tn), lambda i, j: (l, 0, j))],
        out_specs=pl.BlockSpec((tm, tn), lambda i, j: (i, j)),
        out_shape=jax.ShapeDtypeStruct((n, c), F32),
        scratch_shapes=[pltpu.VMEM((tm, d), BF16)],
        compiler_params=_cparams(("parallel", "arbitrary")),
        name="norm_matmul",
    )(x, g.reshape(1, d), w)


def _ffn_up_body(x_ref, g_ref, wg_ref, wu_ref, cw_ref, cb_ref, st_ref, y_ref, so_ref, h_ref, gbuf_ref, carry_ref,
                 *, tiles_per_seq):
    i = pl.program_id(0)
    j = pl.program_id(1)
    tm = x_ref.shape[0]
    lo = SUBLANES - (K_FFN - 1)

    @pl.when(j == 0)
    def _():
        x = x_ref[...]
        ms = jnp.mean(x * x, axis=-1, keepdims=True)
        h_ref[...] = ((x * lax.rsqrt(ms + EPS)) * g_ref[...]).astype(BF16)

    h = h_ref[...]
    first = (i % tiles_per_seq) == 0

    @pl.when(first)
    def _():
        gbuf_ref[0:lo, :] = jnp.zeros((lo, gbuf_ref.shape[1]), F32)
        gbuf_ref[lo:SUBLANES, :] = st_ref[0]

    @pl.when(jnp.logical_not(first))
    def _():
        gbuf_ref[0:SUBLANES, :] = carry_ref[j]

    gbuf_ref[SUBLANES:SUBLANES + tm, :] = jnp.dot(h, wg_ref[...], preferred_element_type=F32)
    acc = jnp.zeros((tm, gbuf_ref.shape[1]), F32)
    for k in range(K_FFN):
        acc = acc + gbuf_ref[lo + k:lo + k + tm, :] * cw_ref[k:k + 1, :]
    up = jnp.dot(h, wu_ref[...], preferred_element_type=F32)
    y_ref[...] = (_silu(acc + cb_ref[...]) * up).astype(BF16)
    carry_ref[j] = gbuf_ref[tm:tm + SUBLANES, :]
    so_ref[0] = gbuf_ref[lo + tm:SUBLANES + tm, :]


def ffn_up_prompt(x, g, w, l, state, conv_w, conv_b, batch, seq):
    n, d = x.shape
    f = w.shape[2] // 2
    tm = _pick(seq, (1024, 512, 256, 128))
    tn = _pick(f, (512, 256, 128))
    nj = f // tn
    tps = seq // tm
    act, tails = pl.pallas_call(
        functools.partial(_ffn_up_body, tiles_per_seq=tps),
        grid=(n // tm, nj),
        in_specs=[pl.BlockSpec((tm, d), lambda i, j: (i, 0)),
                  pl.BlockSpec((1, d), lambda i, j: (0, 0)),
                  pl.BlockSpec((None, d, tn), lambda i, j: (l, 0, j)),
                  pl.BlockSpec((None, d, tn), lambda i, j: (l, 0, nj + j)),
                  pl.BlockSpec((K_FFN, tn), lambda i, j: (0, j)),
                  pl.BlockSpec((1, tn), lambda i, j: (0, j)),
                  pl.BlockSpec((1, K_FFN - 1, tn), lambda i, j: (i // tps, 0, j))],
        out_specs=[pl.BlockSpec((tm, tn), lambda i, j: (i, j)),
                   pl.BlockSpec((1, K_FFN - 1, tn), lambda i, j: (i, 0, j))],
        out_shape=[jax.ShapeDtypeStruct((n, f), BF16),
                   jax.ShapeDtypeStruct((n // tm, K_FFN - 1, f), F32)],
        scratch_shapes=[pltpu.VMEM((tm, d), BF16),
                        pltpu.VMEM((SUBLANES + tm, tn), F32),
                        pltpu.VMEM((nj, SUBLANES, tn), F32)],
        compiler_params=_cparams(("arbitrary", "arbitrary")),
        name="ffn_up_prompt",
    )(x, g.reshape(1, d), w, w, conv_w, conv_b.reshape(1, f), state)
    return act, tails[tps - 1::tps]


def _mm_res_body(a_ref, w_ref, r_ref, o_ref, acc_ref, *, nk):
    k = pl.program_id(2)
    part = jnp.dot(a_ref[...], w_ref[...], preferred_element_type=F32)
    if nk == 1:
        o_ref[...] = r_ref[...] + part
    else:
        @pl.when(k == 0)
        def _():
            acc_ref[...] = part

        @pl.when(jnp.logical_and(k > 0, k < nk - 1))
        def _():
            acc_ref[...] += part

        @pl.when(k == nk - 1)
        def _():
            o_ref[...] = r_ref[...] + (acc_ref[...] + part)


def matmul_residual(a, w, l, res):
    n, kd = a.shape
    c = w.shape[2]
    tm = _pick(n, (1024, 512, 256, 128))
    tn = _pick(c, (1024, 512, 256, 128))
    tk = kd if kd <= 2048 else _pick(kd, (2816, 2048, 1024, 512))
    nk = kd // tk
    return pl.pallas_call(
        functools.partial(_mm_res_body, nk=nk),
        grid=(n // tm, c // tn, nk),
        in_specs=[pl.BlockSpec((tm, tk), lambda i, j, k: (i, k)),
                  pl.BlockSpec((None, tk, tn), lambda i, j, k: (l, k, j)),
                  pl.BlockSpec((tm, tn), lambda i, j, k: (i, j))],
        out_specs=pl.BlockSpec((tm, tn), lambda i, j, k: (i, j)),
        out_shape=jax.ShapeDtypeStruct((n, c), F32),
        scratch_shapes=[pltpu.VMEM((tm, tn), F32)],
        compiler_params=_cparams(("parallel", "parallel", "arbitrary")),
        name="matmul_residual",
    )(a, w, res)


def _group_rms(x, gain, gm):
    s = x * x
    hi = s.astype(BF16)
    lo = (s - hi.astype(F32)).astype(BF16)
    parts = []
    for c in range(x.shape[1] // LANES):
        sl = slice(c * LANES, (c + 1) * LANES)
        parts.append(jnp.dot(hi[:, sl], gm, preferred_element_type=F32)
                     + jnp.dot(lo[:, sl], gm, preferred_element_type=F32))
    ms = jnp.concatenate(parts, axis=-1) * (1.0 / DK_A)
    return (x * lax.rsqrt(ms + EPS)) * gain


_LOG2E = math.log2(math.e)
_Q_SCALE = DK_A ** -0.5 * _LOG2E


def _attn_prep_sample_body(q_ref, k_ref, v_ref, qg_ref, kg_ref, gm_ref, qo_ref, kf_ref, vf_ref):
    gm = gm_ref[...]
    qo_ref[...] = _group_rms(q_ref[...], qg_ref[...], gm) * _Q_SCALE
    kf_ref[...] = _group_rms(k_ref[...], kg_ref[...], gm)
    vf_ref[...] = v_ref[...]


def _attn_prep_prompt_body(q_ref, k_ref, v_ref, qg_ref, kg_ref, gm_ref, qt_ref, kf_ref, kb_ref, vf_ref, vt_ref):
    gm = gm_ref[...]
    qn = _group_rms(q_ref[...], qg_ref[...], gm) * _Q_SCALE
    kn = _group_rms(k_ref[...], kg_ref[...], gm)
    kf_ref[...] = kn
    kb_ref[...] = kn.astype(BF16)
    v = v_ref[...]
    vf_ref[...] = v
    for h in range(H_A):
        hs = slice(h * LANES, (h + 1) * LANES)
        qt_ref[0, h, 0] = qn[:, hs].T.astype(BF16)
        vt_ref[0, h, 0] = v[:, hs].T.astype(BF16)


def _attn_prep_consts(q_gain, k_gain):
    w = H_A * 2 * DK_A
    gm = np.kron(np.eye(LANES // DK_A, dtype=np.float32), np.ones((DK_A, DK_A), np.float32))
    return (jnp.tile(q_gain, w // DK_A).reshape(1, w), jnp.tile(k_gain, w // DK_A).reshape(1, w),
            jnp.asarray(gm, BF16))


def attn_prep_sample(proj, q_gain, k_gain):
    n = proj.shape[0]
    w = H_A * 2 * DK_A
    tm = _pick(n, (512, 256, 128))
    col = lambda c: pl.BlockSpec((tm, w), lambda i, c=c: (i, c))
    cst = lambda shape: pl.BlockSpec(shape, lambda i: (0, 0))
    out = lambda: pl.BlockSpec((tm, w), lambda i: (i, 0))
    return pl.pallas_call(
        _attn_prep_sample_body,
        grid=(n // tm,),
        in_specs=[col(0), col(1), col(2), cst((1, w)), cst((1, w)), cst((LANES, LANES))],
        out_specs=[out(), out(), out()],
        out_shape=[jax.ShapeDtypeStruct((n, w), F32)] * 3,
        compiler_params=_cparams(("parallel",)),
        name="attn_prep_sample",
    )(proj, proj, proj, *_attn_prep_consts(q_gain, k_gain))


def attn_prep_prompt(proj, q_gain, k_gain, batch, seq, t):
    n = batch * seq
    w = H_A * 2 * DK_A
    nt = seq // t
    col = lambda c: pl.BlockSpec((t, w), lambda b, i, c=c: (b * nt + i, c))
    cst = lambda shape: pl.BlockSpec(shape, lambda b, i: (0, 0))
    out = lambda: pl.BlockSpec((t, w), lambda b, i: (b * nt + i, 0))
    tr = lambda: pl.BlockSpec((1, H_A, 1, LANES, t), lambda b, i: (b, 0, i, 0, 0))
    tr_shape = jax.ShapeDtypeStruct((batch, H_A, nt, LANES, t), BF16)
    return pl.pallas_call(
        _attn_prep_prompt_body,
        grid=(batch, nt),
        in_specs=[col(0), col(1), col(2), cst((1, w)), cst((1, w)), cst((LANES, LANES))],
        out_specs=[tr(), out(), out(), out(), tr()],
        out_shape=[tr_shape, jax.ShapeDtypeStruct((n, w), F32), jax.ShapeDtypeStruct((n, w), BF16),
                   jax.ShapeDtypeStruct((n, w), F32), tr_shape],
        compiler_params=_cparams(("parallel", "parallel")),
        name="attn_prep_prompt",
    )(proj, proj, proj, *_attn_prep_consts(q_gain, k_gain))


def _prompt_attn_body(rb_ref, ib_ref, lv_ref, sg_ref, qt_ref, k_ref, vt_ref, o_ref,
                      bias_ref, m_ref, l_ref, acc_ref, *, t, lam_init):
    hg = pl.program_id(1)
    qi = pl.program_id(2)
    heads = range(_ATTN_HEADS)

    @pl.when(qi == 0)
    def _():
        for g in heads:
            h = hg * _ATTN_HEADS + g
            far_bias = rb_ref[N_BUCKETS - 1, h]
            for o in range(2):
                idx = ib_ref[o]
                bias_ref[g, o] = jnp.where(idx >= 0, (_bias_from_buckets(idx, rb_ref, h) - far_bias) * _LOG2E, NEG)

    q2t = []
    for g in heads:
        qt = qt_ref[0, g, 0]
        row = lax.broadcasted_iota(jnp.int32, qt.shape, 0)
        zero = jnp.zeros_like(qt)
        q2t.append(jnp.concatenate([jnp.where(row < DK_A, qt, zero), jnp.where(row >= DK_A, qt, zero)], axis=1))

    m_ref[...] = jnp.full(m_ref.shape, NEG, F32)
    l_ref[...] = jnp.zeros(l_ref.shape, F32)
    acc_ref[...] = jnp.zeros(acc_ref.shape, F32)

    def tile(kj, o):
        off = pl.multiple_of(kj * t, t)
        s = [jnp.dot(k_ref[0, pl.ds(off, t), g * LANES:(g + 1) * LANES], q2t[g], preferred_element_type=F32)
             for g in heads]
        if o is not None:
            s = [s[g] + bias_ref[g, o] for g in heads]
        m_old = [m_ref[g] for g in heads]
        m_new = [jnp.maximum(m_old[g], jnp.max(s[g], axis=0, keepdims=True)) for g in heads]
        alpha = [jnp.exp2(m_old[g] - m_new[g]) for g in heads]
        p = [jnp.exp2(s[g] - m_new[g]) for g in heads]
        for g in heads:
            l_ref[g] = alpha[g] * l_ref[g] + jnp.sum(p[g], axis=0, keepdims=True)
            m_ref[g] = m_new[g]
        pv = [jnp.dot(vt_ref[0, g, kj], p[g].astype(BF16), preferred_element_type=F32) for g in heads]
        for g in heads:
            acc_ref[g] = alpha[g] * acc_ref[g] + pv[g]

    def far(kj, carry):
        tile(kj, None)
        return carry

    lax.fori_loop(0, jnp.maximum(qi - 1, 0), far, 0)

    @pl.when(qi >= 1)
    def _():
        tile(qi - 1, 1)

    tile(qi, 0)

    lam = _lam(lv_ref, lam_init)
    for g in heads:
        o2t = acc_ref[g] / l_ref[g]
        o = (o2t[:, 0:t] - lam * o2t[:, t:2 * t]).T
        ms = jnp.mean(o * o, axis=-1, keepdims=True)
        y = ((o * lax.rsqrt(ms + EPS)) * sg_ref[...]) * (1.0 - lam_init)
        o_ref[0, :, g * LANES:(g + 1) * LANES] = y.astype(BF16)


_ATTN_TILE = 256
_ATTN_HEADS = 8


def prompt_attn(qt, kb, vt, rel_bias, lam_vec, subln_g, lam_init, batch, seq, t):
    w = H_A * DV_A
    assert t >= MAX_DIST and seq % t == 0
    nq = seq // t
    hgs = _ATTN_HEADS
    key = np.arange(t)[:, None]
    qry = (np.arange(2 * t) % t)[None, :]
    d0 = qry - key
    ib = jnp.asarray(np.stack([np.where(d0 >= 0, _t5_bucket_np(d0), -1), _t5_bucket_np(d0 + t)]), jnp.int32)
    assert int(_t5_bucket_np(np.array([t + 1]))[0]) == N_BUCKETS - 1
    out = pl.pallas_call(
        functools.partial(_prompt_attn_body, t=t, lam_init=lam_init),
        grid=(batch, H_A // hgs, nq),
        in_specs=[pl.BlockSpec(memory_space=pltpu.SMEM),
                  pl.BlockSpec((2, t, 2 * t), lambda b, h, i: (0, 0, 0)),
                  pl.BlockSpec((4, DK_A), lambda b, h, i: (0, 0)),
                  pl.BlockSpec((1, DV_A), lambda b, h, i: (0, 0)),
                  pl.BlockSpec((1, hgs, 1, LANES, t), lambda b, h, i: (b, h, i, 0, 0)),
                  pl.BlockSpec((1, seq, hgs * LANES), lambda b, h, i: (b, 0, h)),
                  pl.BlockSpec((1, hgs, nq, LANES, t), lambda b, h, i: (b, h, 0, 0, 0))],
        out_specs=pl.BlockSpec((1, t, hgs * LANES), lambda b, h, i: (b, i, h)),
        out_shape=jax.ShapeDtypeStruct((batch, seq, w), BF16),
        scratch_shapes=[pltpu.VMEM((hgs, 2, t, 2 * t), F32),
                        pltpu.VMEM((hgs, 1, 2 * t), F32),
                        pltpu.VMEM((hgs, 1, 2 * t), F32),
                        pltpu.VMEM((hgs, DV_A, 2 * t), F32)],
        compiler_params=_cparams(("arbitrary", "arbitrary", "arbitrary")),
        name="prompt_attn",
    )(rel_bias, ib, lam_vec, subln_g.reshape(1, DV_A), qt, kb.reshape(batch, seq, w), vt)
    return out.reshape(batch * seq, w)


_RH = SUBLANES


_KPAD = LANES // H_A


def _sample_attn_body(pt_ref, rb_ref, ibp_ref, ibn_ref, lv_ref, sg_ref, q_ref, kn_ref, vn_ref, *rest,
                      pps, nsteps, lam_init):
    k_refs = rest[0:pps]
    v_refs = rest[pps:2 * pps]
    o_ref = rest[2 * pps]
    biasp_ref, biasn_ref, m_ref, l_ref, acc_ref = rest[2 * pps + 1:]
    b = pl.program_id(0)
    s_id = pl.program_id(1)

    t_new = o_ref.shape[1]

    @pl.when(jnp.logical_and(b == 0, s_id == 0))
    def _():
        for h in range(H_A):
            sl = slice(h * _RH, (h + 1) * _RH)
            far_bias = rb_ref[N_BUCKETS - 1, h]
            for o in range(2):
                idx = ibp_ref[o, sl, :]
                biasp_ref[o, sl, :] = jnp.where(idx >= 0, (_bias_from_buckets(idx, rb_ref, h) - far_bias) * _LOG2E,
                                                NEG)
            idx = ibn_ref[sl, :]
            biasn_ref[sl, :] = jnp.where(idx >= 0, (_bias_from_buckets(idx, rb_ref, h) - far_bias) * _LOG2E, NEG)

    @pl.when(s_id == 0)
    def _():
        m_ref[...] = jnp.full(m_ref.shape, NEG, F32)
        l_ref[...] = jnp.zeros(l_ref.shape, F32)
        acc_ref[...] = jnp.zeros(acc_ref.shape, F32)

    q = q_ref[0]
    row = lax.broadcasted_iota(jnp.int32, (_RH, LANES), 0)
    lane = lax.broadcasted_iota(jnp.int32, (_RH, LANES), 1)
    keep = (row < t_new) == (lane < DK_A)
    q2 = jnp.concatenate([jnp.where(keep, q[:, h * LANES:(h + 1) * LANES], 0.0) for h in range(H_A)],
                         axis=0).astype(BF16)

    def attend(kt, vt, bias):
        s = lax.dot_general(q2, kt, _NT, preferred_element_type=F32) + bias
        m_old = m_ref[...]
        m_new = jnp.maximum(m_old, jnp.max(s, axis=-1, keepdims=True))
        alpha = jnp.exp2(m_old - m_new)
        p = jnp.exp2(s - m_new)
        l_ref[...] = alpha * l_ref[...] + jnp.sum(p, axis=-1, keepdims=True)
        acc_ref[...] = alpha * acc_ref[...] + jnp.dot(p.astype(BF16), vt, preferred_element_type=F32)
        m_ref[...] = m_new

    kt = jnp.concatenate([r[0, 0].astype(BF16) for r in k_refs], axis=0)
    vt = jnp.concatenate([r[0, 0].astype(BF16) for r in v_refs], axis=0)
    last = s_id == nsteps - 1
    bias = jnp.concatenate([biasp_ref[0]] * (pps - 1) + [biasp_ref[jnp.where(last, 1, 0)]], axis=1)
    attend(kt, vt, bias)

    @pl.when(last)
    def _():
        attend(kn_ref[0].astype(BF16), vn_ref[0].astype(BF16), biasn_ref[...])
        o2 = acc_ref[...] / l_ref[...]
        lam = _lam(lv_ref, lam_init)
        for h in range(H_A):
            x = o2[h * _RH:(h + 1) * _RH]
            o = x - lam * pltpu.roll(x, t_new, 0)
            ms = jnp.mean(o * o, axis=-1, keepdims=True)
            y = ((o * lax.rsqrt(ms + EPS)) * sg_ref[...]) * (1.0 - lam_init)
            o_ref[0, :, h * LANES:(h + 1) * LANES] = y[0:t_new]


def sample_attn(qn, kf, vf, cache_k, cache_v, layer, page_table, rel_bias, lam_vec, subln_g, lam_init, db, t_new):
    w = H_A * DV_A
    depth, n_pool, page = cache_k.shape[:3]
    n_pages = page_table.shape[1]
    assert 2 * t_new == _RH and t_new <= _KPAD and page >= MAX_DIST
    pps = _pick(n_pages, (8, 4, 2, 1))
    nsteps = n_pages // pps
    rows = H_A * _RH
    kw = page * H_A
    rh = (np.arange(rows) // _RH)[:, None]
    rt = (np.arange(rows) % t_new)[:, None]
    ck_key, ck_head = (np.arange(kw) // H_A)[None, :], (np.arange(kw) % H_A)[None, :]
    ib_far = np.where(rh == ck_head, N_BUCKETS - 1, -1)
    ib_last = np.where(rh == ck_head, _t5_bucket_np(page + rt - ck_key), -1)
    nk_key, nk_head = (np.arange(_KPAD * H_A) // H_A)[None, :], (np.arange(_KPAD * H_A) % H_A)[None, :]
    d_new = rt - nk_key
    ib_new = np.where((rh == nk_head) & (d_new >= 0) & (nk_key < t_new), _t5_bucket_np(d_new), -1)
    ibp = jnp.asarray(np.stack([ib_far, ib_last]), jnp.int32)
    ibn = jnp.asarray(ib_new, jnp.int32)
    assert int(_t5_bucket_np(np.array([page + 1]))[0]) == N_BUCKETS - 1

    q3 = qn.reshape(db, t_new, w)
    q_pad = jnp.concatenate([q3, q3], axis=1)

    def new_rows(x):
        return jnp.pad(x.reshape(db, t_new * H_A, LANES), ((0, 0), (0, (_KPAD - t_new) * H_A), (0, 0)))

    ck = cache_k.reshape(depth, n_pool, kw, LANES)
    cv = cache_v.reshape(depth, n_pool, kw, LANES)
    new_spec = pl.BlockSpec((1, _KPAD * H_A, LANES), lambda b, s, pt: (b, 0, 0))

    def page_spec(i):
        return pl.BlockSpec((1, 1, kw, LANES), lambda b, s, pt, i=i: (layer, pt[b, s * pps + i], 0, 0))

    grid_spec = pltpu.PrefetchScalarGridSpec(
        num_scalar_prefetch=1,
        grid=(db, nsteps),
        in_specs=[pl.BlockSpec(memory_space=pltpu.SMEM),
                  pl.BlockSpec((2, rows, kw), lambda b, s, pt: (0, 0, 0)),
                  pl.BlockSpec((rows, _KPAD * H_A), lambda b, s, pt: (0, 0)),
                  pl.BlockSpec((4, DK_A), lambda b, s, pt: (0, 0)),
                  pl.BlockSpec((1, DV_A), lambda b, s, pt: (0, 0)),
                  pl.BlockSpec((1, _RH, w), lambda b, s, pt: (b, 0, 0)),
                  new_spec, new_spec]
                 + [page_spec(i) for i in range(pps)] + [page_spec(i) for i in range(pps)],
        out_specs=pl.BlockSpec((1, t_new, w), lambda b, s, pt: (b, 0, 0)),
        scratch_shapes=[pltpu.VMEM((2, rows, kw), F32),
                        pltpu.VMEM((rows, _KPAD * H_A), F32),
                        pltpu.VMEM((rows, 1), F32),
                        pltpu.VMEM((rows, 1), F32),
                        pltpu.VMEM((rows, DV_A), F32)],
    )
    out = pl.pallas_call(
        functools.partial(_sample_attn_body, pps=pps, nsteps=nsteps, lam_init=lam_init),
        grid_spec=grid_spec,
        out_shape=jax.ShapeDtypeStruct((db, t_new, w), F32),
        compiler_params=_cparams(("arbitrary", "arbitrary")),
        name="sample_attn",
    )(page_table, rel_bias, ibp, ibn, lam_vec, subln_g.reshape(1, DV_A), q_pad, new_rows(kf), new_rows(vf),
      *([ck] * pps), *([cv] * pps))
    return out.reshape(db * t_new, w).astype(BF16)


def _log_gamma(h):
    return float(np.log1p(-np.exp2(np.float32(-5.0 - h), dtype=np.float32), dtype=np.float32))


def _swap_halves(x):
    n = x.shape[1]
    half = DK_B // 2
    lane = lax.broadcasted_iota(jnp.int32, x.shape, 1)
    return jnp.where((lane % DK_B) < half, pltpu.roll(x, n - half, 1), pltpu.roll(x, half, 1))


def _rotary(x, cos, sin):
    return x * cos + _swap_halves(x) * sin


def _ret_tile(q, k, v, c):
    r = q.shape[0]
    shift = int(math.log2(c))
    assert 1 << shift == c
    ri = lax.broadcasted_iota(jnp.int32, (r, r), 0)
    ci = lax.broadcasted_iota(jnp.int32, (r, r), 1)
    same = (ri >> shift) == (ci >> shift)
    diff = ((ri & (c - 1)) - (ci & (c - 1))).astype(F32)
    valid = jnp.logical_and(same, diff >= 0.0)
    ti = (lax.broadcasted_iota(jnp.int32, (r, 1), 0) & (c - 1)).astype(F32)
    lane = lax.broadcasted_iota(jnp.int32, (r, LANES), 1)
    first = lane < DK_B
    o_inner, q_m, dec_q, k_dec = [], [], [], []
    for p in range(H_B // 2):
        ps = slice(p * LANES, (p + 1) * LANES)
        qp, kp = q[:, ps], k[:, ps]
        kpb = kp.astype(BF16)
        lg0, lg1 = _log_gamma(2 * p), _log_gamma(2 * p + 1)
        k_dec.append((kp * jnp.where(first, jnp.exp((c - 1.0 - ti) * lg0), jnp.exp((c - 1.0 - ti) * lg1))).astype(BF16))
        for hh in range(2):
            h = 2 * p + hh
            lg = lg1 if hh else lg0
            qm = jnp.where(first if hh == 0 else jnp.logical_not(first), qp, 0.0).astype(BF16)
            dmat = jnp.where(valid, jnp.exp(jnp.maximum(diff, 0.0) * lg), 0.0)
            inner = lax.dot_general(qm, kpb, _NT, preferred_element_type=F32) * dmat
            vh = v[:, h * DV_B:(h + 1) * DV_B].astype(BF16)
            o_inner.append(jnp.dot(inner.astype(BF16), vh, preferred_element_type=F32))
            q_m.append(qm)
            dec_q.append(jnp.exp((ti + 1.0) * lg))
    return o_inner, q_m, dec_q, k_dec


def _state_update(s_pair, kd, v, p, c):
    row = lax.broadcasted_iota(jnp.int32, (LANES, DV_B), 0)
    top = row < DK_B
    u0 = lax.dot_general(kd, v[:, (2 * p) * DV_B:(2 * p + 1) * DV_B].astype(BF16), _TN, preferred_element_type=F32)
    u1 = lax.dot_general(kd, v[:, (2 * p + 1) * DV_B:(2 * p + 2) * DV_B].astype(BF16), _TN, preferred_element_type=F32)
    gc = jnp.where(top, math.exp(c * _log_gamma(2 * p)), math.exp(c * _log_gamma(2 * p + 1)))
    return s_pair * gc + jnp.where(top, u0, u1)


def _ret_epilogue(o, g, ng):
    ms = jnp.mean(o * o, axis=-1, keepdims=True)
    return ((o * lax.rsqrt(ms + EPS)) * ng) * _silu(g)


def _ret_prompt_body(q_ref, k_ref, v_ref, g_ref, cos_ref, sin_ref, ng_ref, y_ref, so_ref, s_ref, *, c):
    ci = pl.program_id(1)

    @pl.when(ci == 0)
    def _():
        s_ref[...] = jnp.zeros(s_ref.shape, F32)

    cos, sin = cos_ref[...], sin_ref[...]
    q = _rotary(q_ref[...], cos, sin)
    k = _rotary(k_ref[...], cos, sin) * (DK_B ** -0.5)
    v = v_ref[...]
    o_inner, q_m, dec_q, k_dec = _ret_tile(q, k, v, c)
    ng = ng_ref[...]
    for p in range(H_B // 2):
        s_pair = s_ref[p]
        sb = s_pair.astype(BF16)
        for hh in range(2):
            h = 2 * p + hh
            o = o_inner[h] + jnp.dot(q_m[h], sb, preferred_element_type=F32) * dec_q[h]
            hs = slice(h * DV_B, (h + 1) * DV_B)
            y_ref[:, hs] = _ret_epilogue(o, g_ref[:, hs], ng).astype(BF16)
        s_ref[p] = _state_update(s_pair, k_dec[p], v, p, c)

    @pl.when(ci == pl.num_programs(1) - 1)
    def _():
        so_ref[0] = s_ref[...]


def retention_prompt(proj, cos, sin, ret_norm_g, batch, seq, col0):
    n = batch * seq
    c = 128 if seq % 128 == 0 else seq
    nc = seq // c
    wq = H_B * DK_B
    wv = H_B * DV_B
    assert col0 % wv == 0
    qc, vc = col0 // wq, col0 // wv
    row = lambda b, i: b * nc + i
    y, st = pl.pallas_call(
        functools.partial(_ret_prompt_body, c=c),
        grid=(batch, nc),
        in_specs=[pl.BlockSpec((c, wq), lambda b, i: (row(b, i), qc)),
                  pl.BlockSpec((c, wq), lambda b, i: (row(b, i), qc + 1)),
                  pl.BlockSpec((c, wv), lambda b, i: (row(b, i), vc + 1)),
                  pl.BlockSpec((c, wv), lambda b, i: (row(b, i), vc + 2)),
                  pl.BlockSpec((c, wq), lambda b, i: (i, 0)),
                  pl.BlockSpec((c, wq), lambda b, i: (i, 0)),
                  pl.BlockSpec((1, DV_B), lambda b, i: (0, 0))],
        out_specs=[pl.BlockSpec((c, wv), lambda b, i: (row(b, i), 0)),
                   pl.BlockSpec((1, H_B // 2, LANES, DV_B), lambda b, i: (b, 0, 0, 0))],
        out_shape=[jax.ShapeDtypeStruct((n, wv), BF16),
                   jax.ShapeDtypeStruct((batch, H_B // 2, LANES, DV_B), F32)],
        scratch_shapes=[pltpu.VMEM((H_B // 2, LANES, DV_B), F32)],
        compiler_params=_cparams(("arbitrary", "arbitrary")),
        name="retention_prompt",
    )(proj, proj, proj, proj, cos, sin, ret_norm_g.reshape(1, DV_B))
    return y, st.reshape(batch, H_B, DK_B, DV_B)


def _ret_sample_body(q_ref, k_ref, v_ref, g_ref, cos_ref, sin_ref, ng_ref, s0_ref, y_ref, so_ref,
                     o_ref, qm_ref, kd_ref, *, c):
    gi = pl.program_id(0)
    r = q_ref.shape[0]
    shift = int(math.log2(c))
    v = v_ref[...]

    @pl.when(gi == 0)
    def _():
        cos, sin = cos_ref[...], sin_ref[...]
        q = _rotary(q_ref[...], cos, sin)
        k = _rotary(k_ref[...], cos, sin) * (DK_B ** -0.5)
        o_inner, q_m, dec_q, k_dec = _ret_tile(q, k, v, c)
        for h in range(H_B):
            o_ref[:, h * DV_B:(h + 1) * DV_B] = o_inner[h]
            qm_ref[h] = q_m[h]
        for p in range(H_B // 2):
            kd_ref[p] = k_dec[p]

    rsel = (lax.broadcasted_iota(jnp.int32, (r, LANES), 0) >> shift) == gi
    ti = (lax.broadcasted_iota(jnp.int32, (r, 1), 0) & (c - 1)).astype(F32)
    zero = jnp.zeros((r, LANES), BF16)
    for p in range(H_B // 2):
        s_pair = s0_ref[0, p]
        sb = s_pair.astype(BF16)
        for hh in range(2):
            h = 2 * p + hh
            qm = jnp.where(rsel, qm_ref[h], zero)
            hs = slice(h * DV_B, (h + 1) * DV_B)
            o_ref[:, hs] += jnp.dot(qm, sb, preferred_element_type=F32) * jnp.exp((ti + 1.0) * _log_gamma(h))
        kd = jnp.where(rsel, kd_ref[p], zero)
        so_ref[0, p] = _state_update(s_pair, kd, v, p, c)

    @pl.when(gi == pl.num_programs(0) - 1)
    def _():
        ng = ng_ref[...]
        for h in range(H_B):
            hs = slice(h * DV_B, (h + 1) * DV_B)
            y_ref[:, hs] = _ret_epilogue(o_ref[:, hs], g_ref[:, hs], ng).astype(BF16)


def retention_sample(proj, cos, sin, ret_norm_g, state0, db, t_new, col0):
    n = db * t_new
    wq = H_B * DK_B
    wv = H_B * DV_B
    qc, vc = col0 // wq, col0 // wv
    s0 = state0.reshape(db, H_B // 2, LANES, DV_B)
    st_spec = pl.BlockSpec((1, H_B // 2, LANES, DV_B), lambda g: (g, 0, 0, 0))
    y, st = pl.pallas_call(
        functools.partial(_ret_sample_body, c=t_new),
        grid=(db,),
        in_specs=[pl.BlockSpec((n, wq), lambda g: (0, qc)),
                  pl.BlockSpec((n, wq), lambda g: (0, qc + 1)),
                  pl.BlockSpec((n, wv), lambda g: (0, vc + 1)),
                  pl.BlockSpec((n, wv), lambda g: (0, vc + 2)),
                  pl.BlockSpec((n, wq), lambda g: (0, 0)),
                  pl.BlockSpec((n, wq), lambda g: (0, 0)),
                  pl.BlockSpec((1, DV_B), lambda g: (0, 0)),
                  st_spec],
        out_specs=[pl.BlockSpec((n, wv), lambda g: (0, 0)), st_spec],
        out_shape=[jax.ShapeDtypeStruct((n, wv), BF16),
                   jax.ShapeDtypeStruct((db, H_B // 2, LANES, DV_B), F32)],
        scratch_shapes=[pltpu.VMEM((n, wv), F32),
                        pltpu.VMEM((H_B, n, LANES), BF16),
                        pltpu.VMEM((H_B // 2, n, LANES), BF16)],
        compiler_params=_cparams(("arbitrary",)),
        name="retention_sample",
    )(proj, proj, proj, proj, cos, sin, ret_norm_g.reshape(1, DV_B), s0)
    return y, st.reshape(db, H_B, DK_B, DV_B)


def _rotary_tables(pos):
    half = DK_B // 2
    inv = 1.0 / (10000.0 ** (jnp.arange(half, dtype=F32) / half))
    ang = pos.astype(F32)[:, None] * inv[None, :]
    cos, sin = jnp.cos(ang), jnp.sin(ang)
    cos_t = jnp.tile(jnp.concatenate([cos, cos], axis=-1), (1, H_B))
    sin_t = jnp.tile(jnp.concatenate([-sin, sin], axis=-1), (1, H_B))
    return cos_t, sin_t


_HALO = 32
_CONV_ROWS = 64


def _conv_body(a_ref, b_ref, st_ref, w_ref, cb_ref, lg_ref, lb_ref, y_ref, so_ref, full_ref, sh_ref, uc_ref, *, ts):
    ti = pl.program_id(1)
    lo = _HALO - (K_CONV - 1)

    @pl.when(ti == 0)
    def _():
        full_ref[0:lo, :] = jnp.zeros((lo, full_ref.shape[1]), F32)
        full_ref[lo:_HALO, :] = st_ref[0]

    full_ref[_HALO:_HALO + ts, :] = a_ref[0] * jax.nn.sigmoid(b_ref[0])
    nsh = sh_ref.shape[1]
    for s in range(1, SUBLANES):
        sh_ref[s - 1] = full_ref[s:s + nsh, :]
    rb = min(ts, _CONV_ROWS)
    for c in range(full_ref.shape[1] // LANES):
        cs = slice(c * LANES, (c + 1) * LANES)
        for r0 in range(0, ts, rb):
            acc = None
            for j in range(K_CONV):
                s = (lo + j) % SUBLANES
                a0 = lo + j - s + r0
                win = full_ref[a0:a0 + rb, cs] if s == 0 else sh_ref[s - 1, a0:a0 + rb, cs]
                term = win * w_ref[j:j + 1, cs]
                acc = term if acc is None else acc + term
            uc_ref[r0:r0 + rb, cs] = acc + cb_ref[:, cs]
    uc = uc_ref[...]
    mu = jnp.mean(uc, axis=-1, keepdims=True)
    dev = uc - mu
    var = jnp.mean(dev * dev, axis=-1, keepdims=True)
    y = (dev * lax.rsqrt(var + EPS)) * lg_ref[...] + lb_ref[...]
    y_ref[0] = _silu(y).astype(BF16)
    so_ref[0] = full_ref[lo + ts:_HALO + ts, :]
    if ts >= _HALO:
        full_ref[0:_HALO, :] = full_ref[ts:ts + _HALO, :]


def conv_module(proj3, state, conv_w, conv_b, ln_g, ln_b, col0):
    b, s, _ = proj3.shape
    c = conv_w.shape[1]
    ts = _pick(s, (256, 128)) if s >= 128 else s
    nt = s // ts
    assert nt == 1 or ts >= _HALO
    cc = col0 // c
    vec = lambda: pl.BlockSpec((1, c), lambda i, t: (0, 0))
    return pl.pallas_call(
        functools.partial(_conv_body, ts=ts),
        grid=(b, nt),
        in_specs=[pl.BlockSpec((1, ts, c), lambda i, t: (i, t, cc)),
                  pl.BlockSpec((1, ts, c), lambda i, t: (i, t, cc + 1)),
                  pl.BlockSpec((1, K_CONV - 1, c), lambda i, t: (i, 0, 0)),
                  pl.BlockSpec((K_CONV, c), lambda i, t: (0, 0)),
                  vec(), vec(), vec()],
        out_specs=[pl.BlockSpec((1, ts, c), lambda i, t: (i, t, 0)),
                   pl.BlockSpec((1, K_CONV - 1, c), lambda i, t: (i, 0, 0))],
        out_shape=[jax.ShapeDtypeStruct((b, s, c), BF16),
                   jax.ShapeDtypeStruct((b, K_CONV - 1, c), F32)],
        scratch_shapes=[pltpu.VMEM((_HALO + ts, c), F32),
                        pltpu.VMEM((SUBLANES - 1, _HALO + ts - SUBLANES, c), F32),
                        pltpu.VMEM((ts, c), F32)],
        compiler_params=_cparams(("arbitrary", "arbitrary")),
        name="conv_module",
    )(proj3, proj3, state, conv_w, conv_b.reshape(1, c), ln_g.reshape(1, c), ln_b.reshape(1, c))


def _merge_body(a_ref, b_ref, c_ref, wa_ref, wb_ref, wc_ref, ga_ref, gb_ref, gc_ref,
                ba_ref, bb_ref, bc_ref, o_ref):
    ya = jnp.dot(a_ref[...], wa_ref[...], preferred_element_type=F32)
    m = jax.nn.sigmoid(ga_ref[...] + ba_ref[...]) * ya
    yb = jnp.dot(b_ref[...], wb_ref[...], preferred_element_type=F32)
    m = m + jax.nn.sigmoid(gb_ref[...] + bb_ref[...]) * yb
    yc = jnp.dot(c_ref[...], wc_ref[...], preferred_element_type=F32)
    m = m + jax.nn.sigmoid(gc_ref[...] + bc_ref[...]) * yc
    o_ref[...] = m.astype(BF16)


def branch_merge(a, b, c, wa, wb, wc, l, proj, b_gate, col0):
    n, kd = a.shape
    d = wa.shape[2]
    tm = _pick(n, (512, 256, 128))
    tn = _pick(d, (1024, 512, 256, 128))
    nj = d // tn
    g0 = col0 // tn
    bg = b_gate.reshape(1, 3 * d)
    x_spec = lambda: pl.BlockSpec((tm, kd), lambda i, j: (i, 0))
    w_spec = lambda: pl.BlockSpec((None, kd, tn), lambda i, j: (l, 0, j))
    g_spec = lambda k: pl.BlockSpec((tm, tn), lambda i, j, k=k: (i, g0 + k * nj + j))
    bias_spec = lambda k: pl.BlockSpec((1, tn), lambda i, j, k=k: (0, k * nj + j))
    return pl.pallas_call(
        _merge_body,
        grid=(n // tm, nj),
        in_specs=[x_spec(), x_spec(), x_spec(), w_spec(), w_spec(), w_spec(),
                  g_spec(0), g_spec(1), g_spec(2), bias_spec(0), bias_spec(1), bias_spec(2)],
        out_specs=pl.BlockSpec((tm, tn), lambda i, j: (i, j)),
        out_shape=jax.ShapeDtypeStruct((n, d), BF16),
        compiler_params=_cparams(("parallel", "arbitrary")),
        name="branch_merge",
    )(a, b, c, wa, wb, wc, proj, proj, proj, bg, bg, bg)


def _ffn_act_body(g_ref, u_ref, st_ref, w_ref, cb_ref, y_ref, so_ref, full_ref, *, ts):
    ti = pl.program_id(2)
    lo = SUBLANES - (K_FFN - 1)

    @pl.when(ti == 0)
    def _():
        full_ref[0:lo, :] = jnp.zeros((lo, full_ref.shape[1]), F32)
        full_ref[lo:SUBLANES, :] = st_ref[0]

    full_ref[SUBLANES:SUBLANES + ts, :] = g_ref[0]
    acc = jnp.zeros((ts, full_ref.shape[1]), F32)
    for j in range(K_FFN):
        acc = acc + full_ref[lo + j:lo + j + ts, :] * w_ref[j:j + 1, :]
    y_ref[0] = (_silu(acc + cb_ref[...]) * u_ref[0]).astype(BF16)
    so_ref[0] = full_ref[lo + ts:SUBLANES + ts, :]
    if ts >= SUBLANES:
        full_ref[0:SUBLANES, :] = full_ref[ts:ts + SUBLANES, :]


def ffn_act(h3, state, conv_w, conv_b):
    b, s, f2 = h3.shape
    f = f2 // 2
    ts = _pick(s, (512, 256, 128)) if s >= 128 else s
    tc = _pick(f, (512, 256, 128)) if ts >= 128 else f
    nc = f // tc
    nt = s // ts
    assert nt == 1 or ts >= SUBLANES
    return pl.pallas_call(
        functools.partial(_ffn_act_body, ts=ts),
        grid=(b, nc, nt),
        in_specs=[pl.BlockSpec((1, ts, tc), lambda i, c, t: (i, t, c)),
                  pl.BlockSpec((1, ts, tc), lambda i, c, t: (i, t, nc + c)),
                  pl.BlockSpec((1, K_FFN - 1, tc), lambda i, c, t: (i, 0, c)),
                  pl.BlockSpec((K_FFN, tc), lambda i, c, t: (0, c)),
                  pl.BlockSpec((1, tc), lambda i, c, t: (0, c))],
        out_specs=[pl.BlockSpec((1, ts, tc), lambda i, c, t: (i, t, c)),
                   pl.BlockSpec((1, K_FFN - 1, tc), lambda i, c, t: (i, 0, c))],
        out_shape=[jax.ShapeDtypeStruct((b, s, f), BF16),
                   jax.ShapeDtypeStruct((b, K_FFN - 1, f), F32)],
        scratch_shapes=[pltpu.VMEM((SUBLANES + ts, tc), F32)],
        compiler_params=_cparams(("arbitrary", "arbitrary", "arbitrary")),
        name="ffn_act",
    )(h3, h3, state, conv_w, conv_b.reshape(1, f))


def _layer(x, batch, seq, pos, attn_fn, ret_fn, conv0, ffn0, lw, l, fuse_ffn):
    n, d = x.shape
    a_cols = H_A * (4 * DK_A + DV_A)
    b_cols = H_B * (2 * DK_B + 2 * DV_B)
    c_conv = lw['conv_w'].shape[1]
    proj = norm_matmul(x, lw['norm1_g'], lw['w_in'], l)
    a_in, kf, vf = attn_fn(proj)
    cos, sin = _rotary_tables(pos)
    b_in, ret_new = ret_fn(proj, cos, sin, a_cols)
    c_in, conv_new = conv_module(proj.reshape(batch, seq, -1), conv0, lw['conv_w'], lw['conv_b'],
                                 lw['conv_ln_g'], lw['conv_ln_b'], a_cols + b_cols)
    c_in = c_in.reshape(n, c_conv)
    m = branch_merge(a_in, b_in, c_in, lw['w_o_a'], lw['w_o_b'], lw['w_o_c'], l, proj, lw['b_gate'],
                     a_cols + b_cols + 2 * c_conv)
    x = matmul_residual(m, lw['w_out'], l, x)
    if fuse_ffn:
        act, ffn_new = ffn_up_prompt(x, lw['norm2_g'], lw['w_ffn_in'], l, ffn0, lw['ffn_conv_w'], lw['ffn_conv_b'],
                                     batch, seq)
    else:
        h2 = norm_matmul(x, lw['norm2_g'], lw['w_ffn_in'], l)
        act, ffn_new = ffn_act(h2.reshape(batch, seq, -1), ffn0, lw['ffn_conv_w'], lw['ffn_conv_b'])
        act = act.reshape(n, -1)
    x = matmul_residual(act, lw['w_ffn_down'], l, x)
    states = (kf.reshape(batch, seq, H_A, 2 * DK_A), vf.reshape(batch, seq, H_A, DV_A), ret_new, conv_new, ffn_new)
    return x, states


def kernel(x_prompt, x_sample, cache_k, cache_v, state_ret, state_conv, state_ffn, page_table, rel_bias, norm1_g, w_in, b_gate, q_norm_g, k_norm_g, lam_vec, subln_g, w_o_a, ret_norm_g, w_o_b, conv_w, conv_b, conv_ln_g, conv_ln_b, w_o_c, w_out, norm2_g, w_ffn_in, ffn_conv_w, ffn_conv_b, w_ffn_down):
    bp, sp, d = x_prompt.shape
    db, t_new, _ = x_sample.shape
    depth = w_in.shape[0]
    past = page_table.shape[1] * cache_k.shape[2]
    pos_p = jnp.arange(sp)
    pos_s = jnp.tile(past + jnp.arange(t_new), db)
    yp = x_prompt.reshape(bp * sp, d)
    ys = x_sample.reshape(db * t_new, d)
    c_conv = conv_w.shape[2]
    d_ff = ffn_conv_w.shape[2]
    zero_conv = jnp.zeros((bp, K_CONV - 1, c_conv), F32)
    zero_ffn = jnp.zeros((bp, K_FFN - 1, d_ff), F32)
    sts_p, sts_s = [], []
    wb = {'w_in': w_in.astype(BF16), 'w_o_a': w_o_a.astype(BF16), 'w_o_b': w_o_b.astype(BF16),
          'w_o_c': w_o_c.astype(BF16), 'w_out': w_out.astype(BF16), 'w_ffn_in': w_ffn_in.astype(BF16),
          'w_ffn_down': w_ffn_down.astype(BF16)}
    for l in range(depth):
        lw = dict(wb)
        lw.update({'norm1_g': norm1_g[l], 'b_gate': b_gate[l], 'q_norm_g': q_norm_g[l], 'k_norm_g': k_norm_g[l],
                   'conv_w': conv_w[l], 'conv_b': conv_b[l], 'conv_ln_g': conv_ln_g[l], 'conv_ln_b': conv_ln_b[l],
                   'norm2_g': norm2_g[l], 'ffn_conv_w': ffn_conv_w[l], 'ffn_conv_b': ffn_conv_b[l]})
        lam_init = 0.8 - 0.6 * math.exp(-0.3 * l)

        def p_attn(proj, l=l, lam_init=lam_init, lw=lw):
            t = _pick(sp, (_ATTN_TILE, LANES))
            qt, kf, kb, vf, vt = attn_prep_prompt(proj, lw['q_norm_g'], lw['k_norm_g'], bp, sp, t)
            return prompt_attn(qt, kb, vt, rel_bias, lam_vec[l], subln_g[l], lam_init, bp, sp, t), kf, vf

        def p_ret(proj, cos, sin, col0, l=l):
            return retention_prompt(proj, cos, sin, ret_norm_g[l], bp, sp, col0)

        yp, st = _layer(yp, bp, sp, pos_p, p_attn, p_ret, zero_conv, zero_ffn, lw, l, True)
        sts_p.append(st)

        def s_attn(proj, l=l, lam_init=lam_init, lw=lw):
            qn, kf, vf = attn_prep_sample(proj, lw['q_norm_g'], lw['k_norm_g'])
            return sample_attn(qn, kf, vf, cache_k, cache_v, l, page_table, rel_bias, lam_vec[l], subln_g[l],
                               lam_init, db, t_new), kf, vf

        def s_ret(proj, cos, sin, col0, l=l):
            return retention_sample(proj, cos, sin, ret_norm_g[l], state_ret[l], db, t_new, col0)

        ys, st = _layer(ys, db, t_new, pos_s, s_attn, s_ret, state_conv[l], state_ffn[l], lw, l, False)
        sts_s.append(st)

    def stk(sts, i):
        return jnp.stack([st[i] for st in sts], axis=0)

    return (yp.reshape(bp, sp, d), ys.reshape(db, t_new, d),
            stk(sts_p, 0), stk(sts_p, 1), stk(sts_p, 2), stk(sts_p, 3), stk(sts_p, 4),
            stk(sts_s, 0), stk(sts_s, 1), stk(sts_s, 2), stk(sts_s, 3), stk(sts_s, 4))
```

```python
import functools
import math

import numpy as np
import jax
import jax.numpy as jnp
from jax import lax
from jax.experimental import pallas as pl
from jax.experimental.pallas import tpu as pltpu

F32 = jnp.float32
BF16 = jnp.bfloat16

H_A = 8
DK_A = 64
DV_A = 128
H_B = 8
DK_B = 64
DV_B = 128
K_CONV = 31
K_FFN = 3
N_BUCKETS = 32
MAX_DIST = 128
EPS = 1e-6
NEG = -1e30

LANES = 128
SUBLANES = 8
VMEM_LIMIT = 56 * 1024 * 1024

_NT = (((1,), (1,)), ((), ()))
_TN = (((0,), (0,)), ((), ()))


def _cparams(sem):
    return pltpu.CompilerParams(dimension_semantics=sem, vmem_limit_bytes=VMEM_LIMIT)


def _pick(n, cands):
    for c in cands:
        if n % c == 0:
            return c
    return n


def _t5_bucket_np(d):
    max_exact = N_BUCKETS // 2
    d = np.maximum(d, 0)
    df = np.maximum(d, 1).astype(np.float32)
    large = max_exact + (np.log(df / np.float32(max_exact)) / np.float32(math.log(MAX_DIST / max_exact))
                         * np.float32(N_BUCKETS - max_exact)).astype(np.int32)
    return np.where(d < max_exact, d, np.minimum(large, N_BUCKETS - 1)).astype(np.int32)


def _bias_from_buckets(idx, rb_ref, h):
    acc = jnp.zeros(idx.shape, F32)
    for b in range(N_BUCKETS):
        acc = jnp.where(idx == b, rb_ref[b, h], acc)
    return acc


def _lam(lv_ref, lam_init):
    lv = lv_ref[...]
    a = jnp.sum(lv[0:1] * lv[1:2], axis=-1, keepdims=True)
    b = jnp.sum(lv[2:3] * lv[3:4], axis=-1, keepdims=True)
    return jnp.exp(a) - jnp.exp(b) + lam_init


def _silu(x):
    return x * jax.nn.sigmoid(x)


def _norm_mm_body(x_ref, g_ref, w_ref, o_ref, h_ref):
    @pl.when(pl.program_id(1) == 0)
    def _():
        x = x_ref[...]
        ms = jnp.mean(x * x, axis=-1, keepdims=True)
        h_ref[...] = ((x * lax.rsqrt(ms + EPS)) * g_ref[...]).astype(BF16)

    o_ref[...] = jnp.dot(h_ref[...], w_ref[...], preferred_element_type=F32)


def norm_matmul(x, g, w, l):
    n, d = x.shape
    c = w.shape[2]
    tm = _pick(n, (1024, 512, 256, 128))
    tn = _pick(c, (1024, 512, 256, 128))
    return pl.pallas_call(
        _norm_mm_body,
        grid=(n // tm, c // tn),
        in_specs=[pl.BlockSpec((tm, d), lambda i, j: (i, 0)),
                  pl.BlockSpec((1, d), lambda i, j: (0, 0)),
                  pl.BlockSpec((None, d, tn), lambda i, j: (l, 0, j))],
        out_specs=pl.BlockSpec((tm, tn), lambda i, j: (i, j)),
        out_shape=jax.ShapeDtypeStruct((n, c), F32),
        scratch_shapes=[pltpu.VMEM((tm, d), BF16)],
        compiler_params=_cparams(("parallel", "arbitrary")),
        name="norm_matmul",
    )(x, g.reshape(1, d), w)


def _ffn_up_body(x_ref, g_ref, wg_ref, wu_ref, cw_ref, cb_ref, st_ref, y_ref, so_ref, h_ref, gbuf_ref, carry_ref,
                 *, tiles_per_seq):
    i = pl.program_id(0)
    j = pl.program_id(1)
    tm = x_ref.shape[0]
    lo = SUBLANES - (K_FFN - 1)

    @pl.when(j == 0)
    def _():
        x = x_ref[...]
        ms = jnp.mean(x * x, axis=-1, keepdims=True)
        h_ref[...] = ((x * lax.rsqrt(ms + EPS)) * g_ref[...]).astype(BF16)

    h = h_ref[...]
    first = (i % tiles_per_seq) == 0

    @pl.when(first)
    def _():
        gbuf_ref[0:lo, :] = jnp.zeros((lo, gbuf_ref.shape[1]), F32)
        gbuf_ref[lo:SUBLANES, :] = st_ref[0]

    @pl.when(jnp.logical_not(first))
    def _():
        gbuf_ref[0:SUBLANES, :] = carry_ref[j]

    gbuf_ref[SUBLANES:SUBLANES + tm, :] = jnp.dot(h, wg_ref[...], preferred_element_type=F32)
    acc = jnp.zeros((tm, gbuf_ref.shape[1]), F32)
    for k in range(K_FFN):
        acc = acc + gbuf_ref[lo + k:lo + k + tm, :] * cw_ref[k:k + 1, :]
    up = jnp.dot(h, wu_ref[...], preferred_element_type=F32)
    y_ref[...] = (_silu(acc + cb_ref[...]) * up).astype(BF16)
    carry_ref[j] = gbuf_ref[tm:tm + SUBLANES, :]
    so_ref[0] = gbuf_ref[lo + tm:SUBLANES + tm, :]


def ffn_up_prompt(x, g, w, l, state, conv_w, conv_b, batch, seq):
    n, d = x.shape
    f = w.shape[2] // 2
    tm = _pick(seq, (1024, 512, 256, 128))
    tn = _pick(f, (512, 256, 128))
    nj = f // tn
    tps = seq // tm
    act, tails = pl.pallas_call(
        functools.partial(_ffn_up_body, tiles_per_seq=tps),
        grid=(n // tm, nj),
        in_specs=[pl.BlockSpec((tm, d), lambda i, j: (i, 0)),
                  pl.BlockSpec((1, d), lambda i, j: (0, 0)),
                  pl.BlockSpec((None, d, tn), lambda i, j: (l, 0, j)),
                  pl.BlockSpec((None, d, tn), lambda i, j: (l, 0, nj + j)),
                  pl.BlockSpec((K_FFN, tn), lambda i, j: (0, j)),
                  pl.BlockSpec((1, tn), lambda i, j: (0, j)),
                  pl.BlockSpec((1, K_FFN - 1, tn), lambda i, j: (i // tps, 0, j))],
        out_specs=[pl.BlockSpec((tm, tn), lambda i, j: (i, j)),
                   pl.BlockSpec((1, K_FFN - 1, tn), lambda i, j: (i, 0, j))],
        out_shape=[jax.ShapeDtypeStruct((n, f), BF16),
                   jax.ShapeDtypeStruct((n // tm, K_FFN - 1, f), F32)],
        scratch_shapes=[pltpu.VMEM((tm, d), BF16),
                        pltpu.VMEM((SUBLANES + tm, tn), F32),
                        pltpu.VMEM((nj, SUBLANES, tn), F32)],
        compiler_params=_cparams(("arbitrary", "arbitrary")),
        name="ffn_up_prompt",
    )(x, g.reshape(1, d), w, w, conv_w, conv_b.reshape(1, f), state)
    return act, tails[tps - 1::tps]


def _mm_res_body(a_ref, w_ref, r_ref, o_ref, acc_ref, *, nk):
    k = pl.program_id(2)
    part = jnp.dot(a_ref[...], w_ref[...], preferred_element_type=F32)
    if nk == 1:
        o_ref[...] = r_ref[...] + part
    else:
        @pl.when(k == 0)
        def _():
            acc_ref[...] = part

        @pl.when(jnp.logical_and(k > 0, k < nk - 1))
        def _():
            acc_ref[...] += part

        @pl.when(k == nk - 1)
        def _():
            o_ref[...] = r_ref[...] + (acc_ref[...] + part)


def matmul_residual(a, w, l, res):
    n, kd = a.shape
    c = w.shape[2]
    tm = _pick(n, (1024, 512, 256, 128))
    tn = _pick(c, (1024, 512, 256, 128))
    tk = kd if kd <= 2048 else _pick(kd, (2816, 2048, 1024, 512))
    nk = kd // tk
    return pl.pallas_call(
        functools.partial(_mm_res_body, nk=nk),
        grid=(n // tm, c // tn, nk),
        in_specs=[pl.BlockSpec((tm, tk), lambda i, j, k: (i, k)),
                  pl.BlockSpec((None, tk, tn), lambda i, j, k: (l, k, j)),
                  pl.BlockSpec((tm, tn), lambda i, j, k: (i, j))],
        out_specs=pl.BlockSpec((tm, tn), lambda i, j, k: (i, j)),
        out_shape=jax.ShapeDtypeStruct((n, c), F32),
        scratch_shapes=[pltpu.VMEM((tm, tn), F32)],
        compiler_params=_cparams(("parallel", "parallel", "arbitrary")),
        name="matmul_residual",
    )(a, w, res)


def _group_rms(x, gain, gm):
    s = x * x
    hi = s.astype(BF16)
    lo = (s - hi.astype(F32)).astype(BF16)
    parts = []
    for c in range(x.shape[1] // LANES):
        sl = slice(c * LANES, (c + 1) * LANES)
        parts.append(jnp.dot(hi[:, sl], gm, preferred_element_type=F32)
                     + jnp.dot(lo[:, sl], gm, preferred_element_type=F32))
    ms = jnp.concatenate(parts, axis=-1) * (1.0 / DK_A)
    return (x * lax.rsqrt(ms + EPS)) * gain


_LOG2E = math.log2(math.e)
_Q_SCALE = DK_A ** -0.5 * _LOG2E


def _attn_prep_sample_body(q_ref, k_ref, v_ref, qg_ref, kg_ref, gm_ref, qo_ref, kf_ref, vf_ref):
    gm = gm_ref[...]
    qo_ref[...] = _group_rms(q_ref[...], qg_ref[...], gm) * _Q_SCALE
    kf_ref[...] = _group_rms(k_ref[...], kg_ref[...], gm)
    vf_ref[...] = v_ref[...]


def _attn_prep_prompt_body(q_ref, k_ref, v_ref, qg_ref, kg_ref, gm_ref, qt_ref, kf_ref, kb_ref, vf_ref, vt_ref):
    gm = gm_ref[...]
    qn = _group_rms(q_ref[...], qg_ref[...], gm) * _Q_SCALE
    kn = _group_rms(k_ref[...], kg_ref[...], gm)
    kf_ref[...] = kn
    kb_ref[...] = kn.astype(BF16)
    v = v_ref[...]
    vf_ref[...] = v
    for h in range(H_A):
        hs = slice(h * LANES, (h + 1) * LANES)
        qt_ref[0, h, 0] = qn[:, hs].T.astype(BF16)
        vt_ref[0, h, 0] = v[:, hs].T.astype(BF16)


def _attn_prep_consts(q_gain, k_gain):
    w = H_A * 2 * DK_A
    gm = np.kron(np.eye(LANES // DK_A, dtype=np.float32), np.ones((DK_A, DK_A), np.float32))
    return (jnp.tile(q_gain, w // DK_A).reshape(1, w), jnp.tile(k_gain, w // DK_A).reshape(1, w),
            jnp.asarray(gm, BF16))


def attn_prep_sample(proj, q_gain, k_gain):
    n = proj.shape[0]
    w = H_A * 2 * DK_A
    tm = _pick(n, (512, 256, 128))
    col = lambda c: pl.BlockSpec((tm, w), lambda i, c=c: (i, c))
    cst = lambda shape: pl.BlockSpec(shape, lambda i: (0, 0))
    out = lambda: pl.BlockSpec((tm, w), lambda i: (i, 0))
    return pl.pallas_call(
        _attn_prep_sample_body,
        grid=(n // tm,),
        in_specs=[col(0), col(1), col(2), cst((1, w)), cst((1, w)), cst((LANES, LANES))],
        out_specs=[out(), out(), out()],
        out_shape=[jax.ShapeDtypeStruct((n, w), F32)] * 3,
        compiler_params=_cparams(("parallel",)),
        name="attn_prep_sample",
    )(proj, proj, proj, *_attn_prep_consts(q_gain, k_gain))


def attn_prep_prompt(proj, q_gain, k_gain, batch, seq, t):
    n = batch * seq
    w = H_A * 2 * DK_A
    nt = seq // t
    col = lambda c: pl.BlockSpec((t, w), lambda b, i, c=c: (b * nt + i, c))
    cst = lambda shape: pl.BlockSpec(shape, lambda b, i: (0, 0))
    out = lambda: pl.BlockSpec((t, w), lambda b, i: (b * nt + i, 0))
    tr = lambda: pl.BlockSpec((1, H_A, 1, LANES, t), lambda b, i: (b, 0, i, 0, 0))
    tr_shape = jax.ShapeDtypeStruct((batch, H_A, nt, LANES, t), BF16)
    return pl.pallas_call(
        _attn_prep_prompt_body,
        grid=(batch, nt),
        in_specs=[col(0), col(1), col(2), cst((1, w)), cst((1, w)), cst((LANES, LANES))],
        out_specs=[tr(), out(), out(), out(), tr()],
        out_shape=[tr_shape, jax.ShapeDtypeStruct((n, w), F32), jax.ShapeDtypeStruct((n, w), BF16),
                   jax.ShapeDtypeStruct((n, w), F32), tr_shape],
        compiler_params=_cparams(("parallel", "parallel")),
        name="attn_prep_prompt",
    )(proj, proj, proj, *_attn_prep_consts(q_gain, k_gain))


def _prompt_attn_body(rb_ref, ib_ref, lv_ref, sg_ref, qt_ref, k_ref, vt_ref, o_ref,
                      bias_ref, m_ref, l_ref, acc_ref, *, t, lam_init, n_hg):
    hg = pl.program_id(1)
    qi = pl.program_id(2)
    heads = range(_ATTN_HEADS)
    build = qi == 0
    if n_hg == 1:
        build = jnp.logical_and(build, pl.program_id(0) == 0)

    @pl.when(build)
    def _():
        for g in heads:
            h = hg * _ATTN_HEADS + g
            far_bias = rb_ref[N_BUCKETS - 1, h]
            for o in range(2):
                idx = ib_ref[o]
                bias_ref[g, o] = jnp.where(idx >= 0, (_bias_from_buckets(idx, rb_ref, h) - far_bias) * _LOG2E, NEG)

    q2t = []
    for g in heads:
        qt = qt_ref[0, g, 0]
        row = lax.broadcasted_iota(jnp.int32, qt.shape, 0)
        zero = jnp.zeros_like(qt)
        q2t.append(jnp.concatenate([jnp.where(row < DK_A, qt, zero), jnp.where(row >= DK_A, qt, zero)], axis=1))

    m_ref[...] = jnp.full(m_ref.shape, NEG, F32)
    l_ref[...] = jnp.zeros(l_ref.shape, F32)
    acc_ref[...] = jnp.zeros(acc_ref.shape, F32)

    def tile(kj, o):
        off = pl.multiple_of(kj * t, t)
        s = [jnp.dot(k_ref[0, pl.ds(off, t), g * LANES:(g + 1) * LANES], q2t[g], preferred_element_type=F32)
             for g in heads]
        if o is not None:
            s = [s[g] + bias_ref[g, o] for g in heads]
        m_old = [m_ref[g] for g in heads]
        m_new = [jnp.maximum(m_old[g], jnp.max(s[g], axis=0, keepdims=True)) for g in heads]
        alpha = [jnp.exp2(m_old[g] - m_new[g]) for g in heads]
        p = [jnp.exp2(s[g] - m_new[g]) for g in heads]
        for g in heads:
            l_ref[g] = alpha[g] * l_ref[g] + jnp.sum(p[g], axis=0, keepdims=True)
            m_ref[g] = m_new[g]
        pv = [jnp.dot(vt_ref[0, g, kj], p[g].astype(BF16), preferred_element_type=F32) for g in heads]
        for g in heads:
            acc_ref[g] = alpha[g] * acc_ref[g] + pv[g]

    def far(kj, carry):
        tile(kj, None)
        return carry

    lax.fori_loop(0, jnp.maximum(qi - 1, 0), far, 0)

    @pl.when(qi >= 1)
    def _():
        tile(qi - 1, 1)

    tile(qi, 0)

    lam = _lam(lv_ref, lam_init)
    for g in heads:
        o2t = acc_ref[g] / l_ref[g]
        o = (o2t[:, 0:t] - lam * o2t[:, t:2 * t]).T
        ms = jnp.mean(o * o, axis=-1, keepdims=True)
        y = ((o * lax.rsqrt(ms + EPS)) * sg_ref[...]) * (1.0 - lam_init)
        o_ref[0, :, g * LANES:(g + 1) * LANES] = y.astype(BF16)


_ATTN_TILE = 256
_ATTN_HEADS = 8


def prompt_attn(qt, kb, vt, rel_bias, lam_vec, subln_g, lam_init, batch, seq, t):
    w = H_A * DV_A
    assert t >= MAX_DIST and seq % t == 0
    nq = seq // t
    hgs = _ATTN_HEADS
    key = np.arange(t)[:, None]
    qry = (np.arange(2 * t) % t)[None, :]
    d0 = qry - key
    ib = jnp.asarray(np.stack([np.where(d0 >= 0, _t5_bucket_np(d0), -1), _t5_bucket_np(d0 + t)]), jnp.int32)
    assert int(_t5_bucket_np(np.array([t + 1]))[0]) == N_BUCKETS - 1
    out = pl.pallas_call(
        functools.partial(_prompt_attn_body, t=t, lam_init=lam_init, n_hg=H_A // hgs),
        grid=(batch, H_A // hgs, nq),
        in_specs=[pl.BlockSpec(memory_space=pltpu.SMEM),
                  pl.BlockSpec((2, t, 2 * t), lambda b, h, i: (0, 0, 0)),
                  pl.BlockSpec((4, DK_A), lambda b, h, i: (0, 0)),
                  pl.BlockSpec((1, DV_A), lambda b, h, i: (0, 0)),
                  pl.BlockSpec((1, hgs, 1, LANES, t), lambda b, h, i: (b, h, i, 0, 0)),
                  pl.BlockSpec((1, seq, hgs * LANES), lambda b, h, i: (b, 0, h)),
                  pl.BlockSpec((1, hgs, nq, LANES, t), lambda b, h, i: (b, h, 0, 0, 0))],
        out_specs=pl.BlockSpec((1, t, hgs * LANES), lambda b, h, i: (b, i, h)),
        out_shape=jax.ShapeDtypeStruct((batch, seq, w), BF16),
        scratch_shapes=[pltpu.VMEM((hgs, 2, t, 2 * t), F32),
                        pltpu.VMEM((hgs, 1, 2 * t), F32),
                        pltpu.VMEM((hgs, 1, 2 * t), F32),
                        pltpu.VMEM((hgs, DV_A, 2 * t), F32)],
        compiler_params=_cparams(("arbitrary", "arbitrary", "arbitrary")),
        name="prompt_attn",
    )(rel_bias, ib, lam_vec, subln_g.reshape(1, DV_A), qt, kb.reshape(batch, seq, w), vt)
    return out.reshape(batch * seq, w)


_RH = SUBLANES


_KPAD = LANES // H_A


def _sample_attn_body(pt_ref, rb_ref, ibp_ref, ibn_ref, lv_ref, sg_ref, q_ref, kn_ref, vn_ref, *rest,
                      pps, nsteps, lam_init):
    k_refs = rest[0:pps]
    v_refs = rest[pps:2 * pps]
    o_ref = rest[2 * pps]
    biasp_ref, biasn_ref, m_ref, l_ref, acc_ref = rest[2 * pps + 1:]
    b = pl.program_id(0)
    s_id = pl.program_id(1)

    t_new = o_ref.shape[1]

    @pl.when(jnp.logical_and(b == 0, s_id == 0))
    def _():
        for h in range(H_A):
            sl = slice(h * _RH, (h + 1) * _RH)
            far_bias = rb_ref[N_BUCKETS - 1, h]
            for o in range(2):
                idx = ibp_ref[o, sl, :]
                biasp_ref[o, sl, :] = jnp.where(idx >= 0, (_bias_from_buckets(idx, rb_ref, h) - far_bias) * _LOG2E,
                                                NEG)
            idx = ibn_ref[sl, :]
            biasn_ref[sl, :] = jnp.where(idx >= 0, (_bias_from_buckets(idx, rb_ref, h) - far_bias) * _LOG2E, NEG)

    @pl.when(s_id == 0)
    def _():
        m_ref[...] = jnp.full(m_ref.shape, NEG, F32)
        l_ref[...] = jnp.zeros(l_ref.shape, F32)
        acc_ref[...] = jnp.zeros(acc_ref.shape, F32)

    q = q_ref[0]
    row = lax.broadcasted_iota(jnp.int32, (_RH, LANES), 0)
    lane = lax.broadcasted_iota(jnp.int32, (_RH, LANES), 1)
    keep = (row < t_new) == (lane < DK_A)
    q2 = jnp.concatenate([jnp.where(keep, q[:, h * LANES:(h + 1) * LANES], 0.0) for h in range(H_A)],
                         axis=0).astype(BF16)

    def attend(kt, vt, bias):
        s = lax.dot_general(q2, kt, _NT, preferred_element_type=F32) + bias
        m_old = m_ref[...]
        m_new = jnp.maximum(m_old, jnp.max(s, axis=-1, keepdims=True))
        alpha = jnp.exp2(m_old - m_new)
        p = jnp.exp2(s - m_new)
        l_ref[...] = alpha * l_ref[...] + jnp.sum(p, axis=-1, keepdims=True)
        acc_ref[...] = alpha * acc_ref[...] + jnp.dot(p.astype(BF16), vt, preferred_element_type=F32)
        m_ref[...] = m_new

    kt = jnp.concatenate([r[0, 0].astype(BF16) for r in k_refs], axis=0)
    vt = jnp.concatenate([r[0, 0].astype(BF16) for r in v_refs], axis=0)
    last = s_id == nsteps - 1
    bias = jnp.concatenate([biasp_ref[0]] * (pps - 1) + [biasp_ref[jnp.where(last, 1, 0)]], axis=1)
    attend(kt, vt, bias)

    @pl.when(last)
    def _():
        attend(kn_ref[0].astype(BF16), vn_ref[0].astype(BF16), biasn_ref[...])
        o2 = acc_ref[...] / l_ref[...]
        lam = _lam(lv_ref, lam_init)
        for h in range(H_A):
            x = o2[h * _RH:(h + 1) * _RH]
            o = x - lam * pltpu.roll(x, t_new, 0)
            ms = jnp.mean(o * o, axis=-1, keepdims=True)
            y = ((o * lax.rsqrt(ms + EPS)) * sg_ref[...]) * (1.0 - lam_init)
            o_ref[0, :, h * LANES:(h + 1) * LANES] = y[0:t_new]


def sample_attn(qn, kf, vf, cache_k, cache_v, layer, page_table, rel_bias, lam_vec, subln_g, lam_init, db, t_new):
    w = H_A * DV_A
    depth, n_pool, page = cache_k.shape[:3]
    n_pages = page_table.shape[1]
    assert 2 * t_new == _RH and t_new <= _KPAD and page >= MAX_DIST
    pps = _pick(n_pages, (16, 8, 4, 2, 1))
    nsteps = n_pages // pps
    rows = H_A * _RH
    kw = page * H_A
    rh = (np.arange(rows) // _RH)[:, None]
    rt = (np.arange(rows) % t_new)[:, None]
    ck_key, ck_head = (np.arange(kw) // H_A)[None, :], (np.arange(kw) % H_A)[None, :]
    ib_far = np.where(rh == ck_head, N_BUCKETS - 1, -1)
    ib_last = np.where(rh == ck_head, _t5_bucket_np(page + rt - ck_key), -1)
    nk_key, nk_head = (np.arange(_KPAD * H_A) // H_A)[None, :], (np.arange(_KPAD * H_A) % H_A)[None, :]
    d_new = rt - nk_key
    ib_new = np.where((rh == nk_head) & (d_new >= 0) & (nk_key < t_new), _t5_bucket_np(d_new), -1)
    ibp = jnp.asarray(np.stack([ib_far, ib_last]), jnp.int32)
    ibn = jnp.asarray(ib_new, jnp.int32)
    assert int(_t5_bucket_np(np.array([page + 1]))[0]) == N_BUCKETS - 1

    q3 = qn.reshape(db, t_new, w)
    q_pad = jnp.concatenate([q3, q3], axis=1)

    def new_rows(x):
        return jnp.pad(x.reshape(db, t_new * H_A, LANES), ((0, 0), (0, (_KPAD - t_new) * H_A), (0, 0)))

    ck = cache_k.reshape(depth, n_pool, kw, LANES)
    cv = cache_v.reshape(depth, n_pool, kw, LANES)
    new_spec = pl.BlockSpec((1, _KPAD * H_A, LANES), lambda b, s, pt: (b, 0, 0))

    def page_spec(i):
        return pl.BlockSpec((1, 1, kw, LANES), lambda b, s, pt, i=i: (layer, pt[b, s * pps + i], 0, 0))

    grid_spec = pltpu.PrefetchScalarGridSpec(
        num_scalar_prefetch=1,
        grid=(db, nsteps),
        in_specs=[pl.BlockSpec(memory_space=pltpu.SMEM),
                  pl.BlockSpec((2, rows, kw), lambda b, s, pt: (0, 0, 0)),
                  pl.BlockSpec((rows, _KPAD * H_A), lambda b, s, pt: (0, 0)),
                  pl.BlockSpec((4, DK_A), lambda b, s, pt: (0, 0)),
                  pl.BlockSpec((1, DV_A), lambda b, s, pt: (0, 0)),
                  pl.BlockSpec((1, _RH, w), lambda b, s, pt: (b, 0, 0)),
                  new_spec, new_spec]
                 + [page_spec(i) for i in range(pps)] + [page_spec(i) for i in range(pps)],
        out_specs=pl.BlockSpec((1, t_new, w), lambda b, s, pt: (b, 0, 0)),
        scratch_shapes=[pltpu.VMEM((2, rows, kw), F32),
                        pltpu.VMEM((rows, _KPAD * H_A), F32),
                        pltpu.VMEM((rows, 1), F32),
                        pltpu.VMEM((rows, 1), F32),
                        pltpu.VMEM((rows, DV_A), F32)],
    )
    out = pl.pallas_call(
        functools.partial(_sample_attn_body, pps=pps, nsteps=nsteps, lam_init=lam_init),
        grid_spec=grid_spec,
        out_shape=jax.ShapeDtypeStruct((db, t_new, w), F32),
        compiler_params=_cparams(("arbitrary", "arbitrary")),
        name="sample_attn",
    )(page_table, rel_bias, ibp, ibn, lam_vec, subln_g.reshape(1, DV_A), q_pad, new_rows(kf), new_rows(vf),
      *([ck] * pps), *([cv] * pps))
    return out.reshape(db * t_new, w).astype(BF16)


def _log_gamma(h):
    return float(np.log1p(-np.exp2(np.float32(-5.0 - h), dtype=np.float32), dtype=np.float32))


def _swap_halves(x):
    n = x.shape[1]
    half = DK_B // 2
    lane = lax.broadcasted_iota(jnp.int32, x.shape, 1)
    return jnp.where((lane % DK_B) < half, pltpu.roll(x, n - half, 1), pltpu.roll(x, half, 1))


def _rotary(x, cos, sin):
    return x * cos + _swap_halves(x) * sin


def _ret_tile(q, k, v, c):
    r = q.shape[0]
    shift = int(math.log2(c))
    assert 1 << shift == c
    ri = lax.broadcasted_iota(jnp.int32, (r, r), 0)
    ci = lax.broadcasted_iota(jnp.int32, (r, r), 1)
    same = (ri >> shift) == (ci >> shift)
    diff = ((ri & (c - 1)) - (ci & (c - 1))).astype(F32)
    valid = jnp.logical_and(same, diff >= 0.0)
    ti = (lax.broadcasted_iota(jnp.int32, (r, 1), 0) & (c - 1)).astype(F32)
    lane = lax.broadcasted_iota(jnp.int32, (r, LANES), 1)
    first = lane < DK_B
    o_inner, q_m, dec_q, k_dec = [], [], [], []
    for p in range(H_B // 2):
        ps = slice(p * LANES, (p + 1) * LANES)
        qp, kp = q[:, ps], k[:, ps]
        kpb = kp.astype(BF16)
        lg0, lg1 = _log_gamma(2 * p), _log_gamma(2 * p + 1)
        k_dec.append((kp * jnp.where(first, jnp.exp((c - 1.0 - ti) * lg0), jnp.exp((c - 1.0 - ti) * lg1))).astype(BF16))
        for hh in range(2):
            h = 2 * p + hh
            lg = lg1 if hh else lg0
            qm = jnp.where(first if hh == 0 else jnp.logical_not(first), qp, 0.0).astype(BF16)
            dmat = jnp.where(valid, jnp.exp(jnp.maximum(diff, 0.0) * lg), 0.0)
            inner = lax.dot_general(qm, kpb, _NT, preferred_element_type=F32) * dmat
            vh = v[:, h * DV_B:(h + 1) * DV_B].astype(BF16)
            o_inner.append(jnp.dot(inner.astype(BF16), vh, preferred_element_type=F32))
            q_m.append(qm)
            dec_q.append(jnp.exp((ti + 1.0) * lg))
    return o_inner, q_m, dec_q, k_dec


def _state_update(s_pair, kd, v, p, c):
    row = lax.broadcasted_iota(jnp.int32, (LANES, DV_B), 0)
    top = row < DK_B
    u0 = lax.dot_general(kd, v[:, (2 * p) * DV_B:(2 * p + 1) * DV_B].astype(BF16), _TN, preferred_element_type=F32)
    u1 = lax.dot_general(kd, v[:, (2 * p + 1) * DV_B:(2 * p + 2) * DV_B].astype(BF16), _TN, preferred_element_type=F32)
    gc = jnp.where(top, math.exp(c * _log_gamma(2 * p)), math.exp(c * _log_gamma(2 * p + 1)))
    return s_pair * gc + jnp.where(top, u0, u1)


def _ret_epilogue(o, g, ng):
    ms = jnp.mean(o * o, axis=-1, keepdims=True)
    return ((o * lax.rsqrt(ms + EPS)) * ng) * _silu(g)


def _ret_prompt_body(q_ref, k_ref, v_ref, g_ref, cos_ref, sin_ref, ng_ref, y_ref, so_ref, s_ref, *, c):
    ci = pl.program_id(1)

    @pl.when(ci == 0)
    def _():
        s_ref[...] = jnp.zeros(s_ref.shape, F32)

    cos, sin = cos_ref[...], sin_ref[...]
    q = _rotary(q_ref[...], cos, sin)
    k = _rotary(k_ref[...], cos, sin) * (DK_B ** -0.5)
    v = v_ref[...]
    o_inner, q_m, dec_q, k_dec = _ret_tile(q, k, v, c)
    ng = ng_ref[...]
    for p in range(H_B // 2):
        s_pair = s_ref[p]
        sb = s_pair.astype(BF16)
        for hh in range(2):
            h = 2 * p + hh
            o = o_inner[h] + jnp.dot(q_m[h], sb, preferred_element_type=F32) * dec_q[h]
            hs = slice(h * DV_B, (h + 1) * DV_B)
            y_ref[:, hs] = _ret_epilogue(o, g_ref[:, hs], ng).astype(BF16)
        s_ref[p] = _state_update(s_pair, k_dec[p], v, p, c)

    @pl.when(ci == pl.num_programs(1) - 1)
    def _():
        so_ref[0] = s_ref[...]


def retention_prompt(proj, cos, sin, ret_norm_g, batch, seq, col0):
    n = batch * seq
    c = 128 if seq % 128 == 0 else seq
    nc = seq // c
    wq = H_B * DK_B
    wv = H_B * DV_B
    assert col0 % wv == 0
    qc, vc = col0 // wq, col0 // wv
    row = lambda b, i: b * nc + i
    y, st = pl.pallas_call(
        functools.partial(_ret_prompt_body, c=c),
        grid=(batch, nc),
        in_specs=[pl.BlockSpec((c, wq), lambda b, i: (row(b, i), qc)),
                  pl.BlockSpec((c, wq), lambda b, i: (row(b, i), qc + 1)),
                  pl.BlockSpec((c, wv), lambda b, i: (row(b, i), vc + 1)),
                  pl.BlockSpec((c, wv), lambda b, i: (row(b, i), vc + 2)),
                  pl.BlockSpec((c, wq), lambda b, i: (i, 0)),
                  pl.BlockSpec((c, wq), lambda b, i: (i, 0)),
                  pl.BlockSpec((1, DV_B), lambda b, i: (0, 0))],
        out_specs=[pl.BlockSpec((c, wv), lambda b, i: (row(b, i), 0)),
                   pl.BlockSpec((1, H_B // 2, LANES, DV_B), lambda b, i: (b, 0, 0, 0))],
        out_shape=[jax.ShapeDtypeStruct((n, wv), BF16),
                   jax.ShapeDtypeStruct((batch, H_B // 2, LANES, DV_B), F32)],
        scratch_shapes=[pltpu.VMEM((H_B // 2, LANES, DV_B), F32)],
        compiler_params=_cparams(("arbitrary", "arbitrary")),
        name="retention_prompt",
    )(proj, proj, proj, proj, cos, sin, ret_norm_g.reshape(1, DV_B))
    return y, st.reshape(batch, H_B, DK_B, DV_B)


def _ret_sample_body(q_ref, k_ref, v_ref, g_ref, cos_ref, sin_ref, ng_ref, s0_ref, y_ref, so_ref,
                     o_ref, qm_ref, kd_ref, *, c):
    gi = pl.program_id(0)
    r = q_ref.shape[0]
    shift = int(math.log2(c))
    v = v_ref[...]

    @pl.when(gi == 0)
    def _():
        cos, sin = cos_ref[...], sin_ref[...]
        q = _rotary(q_ref[...], cos, sin)
        k = _rotary(k_ref[...], cos, sin) * (DK_B ** -0.5)
        o_inner, q_m, dec_q, k_dec = _ret_tile(q, k, v, c)
        for h in range(H_B):
            o_ref[:, h * DV_B:(h + 1) * DV_B] = o_inner[h]
            qm_ref[h] = q_m[h]
        for p in range(H_B // 2):
            kd_ref[p] = k_dec[p]

    rsel = (lax.broadcasted_iota(jnp.int32, (r, LANES), 0) >> shift) == gi
    ti = (lax.broadcasted_iota(jnp.int32, (r, 1), 0) & (c - 1)).astype(F32)
    zero = jnp.zeros((r, LANES), BF16)
    for p in range(H_B // 2):
        s_pair = s0_ref[0, p]
        sb = s_pair.astype(BF16)
        for hh in range(2):
            h = 2 * p + hh
            qm = jnp.where(rsel, qm_ref[h], zero)
            hs = slice(h * DV_B, (h + 1) * DV_B)
            o_ref[:, hs] += jnp.dot(qm, sb, preferred_element_type=F32) * jnp.exp((ti + 1.0) * _log_gamma(h))
        kd = jnp.where(rsel, kd_ref[p], zero)
        so_ref[0, p] = _state_update(s_pair, kd, v, p, c)

    @pl.when(gi == pl.num_programs(0) - 1)
    def _():
        ng = ng_ref[...]
        for h in range(H_B):
            hs = slice(h * DV_B, (h + 1) * DV_B)
            y_ref[:, hs] = _ret_epilogue(o_ref[:, hs], g_ref[:, hs], ng).astype(BF16)


def retention_sample(proj, cos, sin, ret_norm_g, state0, db, t_new, col0):
    n = db * t_new
    wq = H_B * DK_B
    wv = H_B * DV_B
    qc, vc = col0 // wq, col0 // wv
    s0 = state0.reshape(db, H_B // 2, LANES, DV_B)
    st_spec = pl.BlockSpec((1, H_B // 2, LANES, DV_B), lambda g: (g, 0, 0, 0))
    y, st = pl.pallas_call(
        functools.partial(_ret_sample_body, c=t_new),
        grid=(db,),
        in_specs=[pl.BlockSpec((n, wq), lambda g: (0, qc)),
                  pl.BlockSpec((n, wq), lambda g: (0, qc + 1)),
                  pl.BlockSpec((n, wv), lambda g: (0, vc + 1)),
                  pl.BlockSpec((n, wv), lambda g: (0, vc + 2)),
                  pl.BlockSpec((n, wq), lambda g: (0, 0)),
                  pl.BlockSpec((n, wq), lambda g: (0, 0)),
                  pl.BlockSpec((1, DV_B), lambda g: (0, 0)),
                  st_spec],
        out_specs=[pl.BlockSpec((n, wv), lambda g: (0, 0)), st_spec],
        out_shape=[jax.ShapeDtypeStruct((n, wv), BF16),
                   jax.ShapeDtypeStruct((db, H_B // 2, LANES, DV_B), F32)],
        scratch_shapes=[pltpu.VMEM((n, wv), F32),
                        pltpu.VMEM((H_B, n, LANES), BF16),
                        pltpu.VMEM((H_B // 2, n, LANES), BF16)],
        compiler_params=_cparams(("arbitrary",)),
        name="retention_sample",
    )(proj, proj, proj, proj, cos, sin, ret_norm_g.reshape(1, DV_B), s0)
    return y, st.reshape(db, H_B, DK_B, DV_B)


def _rotary_tables(pos):
    half = DK_B // 2
    inv = 1.0 / (10000.0 ** (jnp.arange(half, dtype=F32) / half))
    ang = pos.astype(F32)[:, None] * inv[None, :]
    cos, sin = jnp.cos(ang), jnp.sin(ang)
    cos_t = jnp.tile(jnp.concatenate([cos, cos], axis=-1), (1, H_B))
    sin_t = jnp.tile(jnp.concatenate([-sin, sin], axis=-1), (1, H_B))
    return cos_t, sin_t


_HALO = 32
_CONV_ROWS = 64


def _conv_body(a_ref, b_ref, st_ref, w_ref, cb_ref, lg_ref, lb_ref, y_ref, so_ref, full_ref, sh_ref, uc_ref, *, ts):
    ti = pl.program_id(1)
    lo = _HALO - (K_CONV - 1)

    @pl.when(ti == 0)
    def _():
        full_ref[0:lo, :] = jnp.zeros((lo, full_ref.shape[1]), F32)
        full_ref[lo:_HALO, :] = st_ref[0]

    full_ref[_HALO:_HALO + ts, :] = a_ref[0] * jax.nn.sigmoid(b_ref[0])
    nsh = sh_ref.shape[1]
    for s in range(1, SUBLANES):
        sh_ref[s - 1] = full_ref[s:s + nsh, :]
    rb = min(ts, _CONV_ROWS)
    for c in range(full_ref.shape[1] // LANES):
        cs = slice(c * LANES, (c + 1) * LANES)
        for r0 in range(0, ts, rb):
            acc = None
            for j in range(K_CONV):
                s = (lo + j) % SUBLANES
                a0 = lo + j - s + r0
                win = full_ref[a0:a0 + rb, cs] if s == 0 else sh_ref[s - 1, a0:a0 + rb, cs]
                term = win * w_ref[j:j + 1, cs]
                acc = term if acc is None else acc + term
            uc_ref[r0:r0 + rb, cs] = acc + cb_ref[:, cs]
    uc = uc_ref[...]
    mu = jnp.mean(uc, axis=-1, keepdims=True)
    dev = uc - mu
    var = jnp.mean(dev * dev, axis=-1, keepdims=True)
    y = (dev * lax.rsqrt(var + EPS)) * lg_ref[...] + lb_ref[...]
    y_ref[0] = _silu(y).astype(BF16)
    so_ref[0] = full_ref[lo + ts:_HALO + ts, :]
    if ts >= _HALO:
        full_ref[0:_HALO, :] = full_ref[ts:ts + _HALO, :]


def conv_module(proj3, state, conv_w, conv_b, ln_g, ln_b, col0):
    b, s, _ = proj3.shape
    c = conv_w.shape[1]
    ts = _pick(s, (256, 128)) if s >= 128 else s
    nt = s // ts
    assert nt == 1 or ts >= _HALO
    cc = col0 // c
    vec = lambda: pl.BlockSpec((1, c), lambda i, t: (0, 0))
    return pl.pallas_call(
        functools.partial(_conv_body, ts=ts),
        grid=(b, nt),
        in_specs=[pl.BlockSpec((1, ts, c), lambda i, t: (i, t, cc)),
                  pl.BlockSpec((1, ts, c), lambda i, t: (i, t, cc + 1)),
                  pl.BlockSpec((1, K_CONV - 1, c), lambda i, t: (i, 0, 0)),
                  pl.BlockSpec((K_CONV, c), lambda i, t: (0, 0)),
                  vec(), vec(), vec()],
        out_specs=[pl.BlockSpec((1, ts, c), lambda i, t: (i, t, 0)),
                   pl.BlockSpec((1, K_CONV - 1, c), lambda i, t: (i, 0, 0))],
        out_shape=[jax.ShapeDtypeStruct((b, s, c), BF16),
                   jax.ShapeDtypeStruct((b, K_CONV - 1, c), F32)],
        scratch_shapes=[pltpu.VMEM((_HALO + ts, c), F32),
                        pltpu.VMEM((SUBLANES - 1, _HALO + ts - SUBLANES, c), F32),
                        pltpu.VMEM((ts, c), F32)],
        compiler_params=_cparams(("arbitrary", "arbitrary")),
        name="conv_module",
    )(proj3, proj3, state, conv_w, conv_b.reshape(1, c), ln_g.reshape(1, c), ln_b.reshape(1, c))


def _merge_body(a_ref, b_ref, c_ref, wa_ref, wb_ref, wc_ref, ga_ref, gb_ref, gc_ref,
                ba_ref, bb_ref, bc_ref, o_ref):
    ya = jnp.dot(a_ref[...], wa_ref[...], preferred_element_type=F32)
    m = jax.nn.sigmoid(ga_ref[...] + ba_ref[...]) * ya
    yb = jnp.dot(b_ref[...], wb_ref[...], preferred_element_type=F32)
    m = m + jax.nn.sigmoid(gb_ref[...] + bb_ref[...]) * yb
    yc = jnp.dot(c_ref[...], wc_ref[...], preferred_element_type=F32)
    m = m + jax.nn.sigmoid(gc_ref[...] + bc_ref[...]) * yc
    o_ref[...] = m.astype(BF16)


def branch_merge(a, b, c, wa, wb, wc, l, proj, b_gate, col0):
    n, kd = a.shape
    d = wa.shape[2]
    tm = _pick(n, (512, 256, 128))
    tn = _pick(d, (1024, 512, 256, 128))
    nj = d // tn
    g0 = col0 // tn
    bg = b_gate.reshape(1, 3 * d)
    x_spec = lambda: pl.BlockSpec((tm, kd), lambda j, i: (i, 0))
    w_spec = lambda: pl.BlockSpec((None, kd, tn), lambda j, i: (l, 0, j))
    g_spec = lambda k: pl.BlockSpec((tm, tn), lambda j, i, k=k: (i, g0 + k * nj + j))
    bias_spec = lambda k: pl.BlockSpec((1, tn), lambda j, i, k=k: (0, k * nj + j))
    return pl.pallas_call(
        _merge_body,
        grid=(nj, n // tm),
        in_specs=[x_spec(), x_spec(), x_spec(), w_spec(), w_spec(), w_spec(),
                  g_spec(0), g_spec(1), g_spec(2), bias_spec(0), bias_spec(1), bias_spec(2)],
        out_specs=pl.BlockSpec((tm, tn), lambda j, i: (i, j)),
        out_shape=jax.ShapeDtypeStruct((n, d), BF16),
        compiler_params=_cparams(("parallel", "arbitrary")),
        name="branch_merge",
    )(a, b, c, wa, wb, wc, proj, proj, proj, bg, bg, bg)


def _ffn_act_body(g_ref, u_ref, st_ref, w_ref, cb_ref, y_ref, so_ref, full_ref, *, ts):
    ti = pl.program_id(2)
    lo = SUBLANES - (K_FFN - 1)

    @pl.when(ti == 0)
    def _():
        full_ref[0:lo, :] = jnp.zeros((lo, full_ref.shape[1]), F32)
        full_ref[lo:SUBLANES, :] = st_ref[0]

    full_ref[SUBLANES:SUBLANES + ts, :] = g_ref[0]
    acc = jnp.zeros((ts, full_ref.shape[1]), F32)
    for j in range(K_FFN):
        acc = acc + full_ref[lo + j:lo + j + ts, :] * w_ref[j:j + 1, :]
    y_ref[0] = (_silu(acc + cb_ref[...]) * u_ref[0]).astype(BF16)
    so_ref[0] = full_ref[lo + ts:SUBLANES + ts, :]
    if ts >= SUBLANES:
        full_ref[0:SUBLANES, :] = full_ref[ts:ts + SUBLANES, :]


def ffn_act(h3, state, conv_w, conv_b):
    b, s, f2 = h3.shape
    f = f2 // 2
    ts = _pick(s, (512, 256, 128)) if s >= 128 else s
    tc = _pick(f, (512, 256, 128)) if ts >= 128 else f
    nc = f // tc
    nt = s // ts
    assert nt == 1 or ts >= SUBLANES
    return pl.pallas_call(
        functools.partial(_ffn_act_body, ts=ts),
        grid=(b, nc, nt),
        in_specs=[pl.BlockSpec((1, ts, tc), lambda i, c, t: (i, t, c)),
                  pl.BlockSpec((1, ts, tc), lambda i, c, t: (i, t, nc + c)),
                  pl.BlockSpec((1, K_FFN - 1, tc), lambda i, c, t: (i, 0, c)),
                  pl.BlockSpec((K_FFN, tc), lambda i, c, t: (0, c)),
                  pl.BlockSpec((1, tc), lambda i, c, t: (0, c))],
        out_specs=[pl.BlockSpec((1, ts, tc), lambda i, c, t: (i, t, c)),
                   pl.BlockSpec((1, K_FFN - 1, tc), lambda i, c, t: (i, 0, c))],
        out_shape=[jax.ShapeDtypeStruct((b, s, f), BF16),
                   jax.ShapeDtypeStruct((b, K_FFN - 1, f), F32)],
        scratch_shapes=[pltpu.VMEM((SUBLANES + ts, tc), F32)],
        compiler_params=_cparams(("arbitrary", "arbitrary", "arbitrary")),
        name="ffn_act",
    )(h3, h3, state, conv_w, conv_b.reshape(1, f))


def _layer(x, batch, seq, pos, attn_fn, ret_fn, conv0, ffn0, lw, l, fuse_ffn):
    n, d = x.shape
    a_cols = H_A * (4 * DK_A + DV_A)
    b_cols = H_B * (2 * DK_B + 2 * DV_B)
    c_conv = lw['conv_w'].shape[1]
    proj = norm_matmul(x, lw['norm1_g'], lw['w_in'], l)
    a_in, kf, vf = attn_fn(proj)
    cos, sin = _rotary_tables(pos)
    b_in, ret_new = ret_fn(proj, cos, sin, a_cols)
    c_in, conv_new = conv_module(proj.reshape(batch, seq, -1), conv0, lw['conv_w'], lw['conv_b'],
                                 lw['conv_ln_g'], lw['conv_ln_b'], a_cols + b_cols)
    c_in = c_in.reshape(n, c_conv)
    m = branch_merge(a_in, b_in, c_in, lw['w_o_a'], lw['w_o_b'], lw['w_o_c'], l, proj, lw['b_gate'],
                     a_cols + b_cols + 2 * c_conv)
    x = matmul_residual(m, lw['w_out'], l, x)
    if fuse_ffn:
        act, ffn_new = ffn_up_prompt(x, lw['norm2_g'], lw['w_ffn_in'], l, ffn0, lw['ffn_conv_w'], lw['ffn_conv_b'],
                                     batch, seq)
    else:
        h2 = norm_matmul(x, lw['norm2_g'], lw['w_ffn_in'], l)
        act, ffn_new = ffn_act(h2.reshape(batch, seq, -1), ffn0, lw['ffn_conv_w'], lw['ffn_conv_b'])
        act = act.reshape(n, -1)
    x = matmul_residual(act, lw['w_ffn_down'], l, x)
    states = (kf.reshape(batch, seq, H_A, 2 * DK_A), vf.reshape(batch, seq, H_A, DV_A), ret_new, conv_new, ffn_new)
    return x, states


def kernel(x_prompt, x_sample, cache_k, cache_v, state_ret, state_conv, state_ffn, page_table, rel_bias, norm1_g, w_in, b_gate, q_norm_g, k_norm_g, lam_vec, subln_g, w_o_a, ret_norm_g, w_o_b, conv_w, conv_b, conv_ln_g, conv_ln_b, w_o_c, w_out, norm2_g, w_ffn_in, ffn_conv_w, ffn_conv_b, w_ffn_down):
    bp, sp, d = x_prompt.shape
    db, t_new, _ = x_sample.shape
    depth = w_in.shape[0]
    past = page_table.shape[1] * cache_k.shape[2]
    pos_p = jnp.arange(sp)
    pos_s = jnp.tile(past + jnp.arange(t_new), db)
    yp = x_prompt.reshape(bp * sp, d)
    ys = x_sample.reshape(db * t_new, d)
    c_conv = conv_w.shape[2]
    d_ff = ffn_conv_w.shape[2]
    zero_conv = jnp.zeros((bp, K_CONV - 1, c_conv), F32)
    zero_ffn = jnp.zeros((bp, K_FFN - 1, d_ff), F32)
    sts_p, sts_s = [], []
    wb = {'w_in': w_in.astype(BF16), 'w_o_a': w_o_a.astype(BF16), 'w_o_b': w_o_b.astype(BF16),
          'w_o_c': w_o_c.astype(BF16), 'w_out': w_out.astype(BF16), 'w_ffn_in': w_ffn_in.astype(BF16),
          'w_ffn_down': w_ffn_down.astype(BF16)}
    for l in range(depth):
        lw = dict(wb)
        lw.update({'norm1_g': norm1_g[l], 'b_gate': b_gate[l], 'q_norm_g': q_norm_g[l], 'k_norm_g': k_norm_g[l],
                   'conv_w': conv_w[l], 'conv_b': conv_b[l], 'conv_ln_g': conv_ln_g[l], 'conv_ln_b': conv_ln_b[l],
                   'norm2_g': norm2_g[l], 'ffn_conv_w': ffn_conv_w[l], 'ffn_conv_b': ffn_conv_b[l]})
        lam_init = 0.8 - 0.6 * math.exp(-0.3 * l)

        def p_attn(proj, l=l, lam_init=lam_init, lw=lw):
            t = _pick(sp, (_ATTN_TILE, LANES))
            qt, kf, kb, vf, vt = attn_prep_prompt(proj, lw['q_norm_g'], lw['k_norm_g'], bp, sp, t)
            return prompt_attn(qt, kb, vt, rel_bias, lam_vec[l], subln_g[l], lam_init, bp, sp, t), kf, vf

        def p_ret(proj, cos, sin, col0, l=l):
            return retention_prompt(proj, cos, sin, ret_norm_g[l], bp, sp, col0)

        yp, st = _layer(yp, bp, sp, pos_p, p_attn, p_ret, zero_conv, zero_ffn, lw, l, True)
        sts_p.append(st)

        def s_attn(proj, l=l, lam_init=lam_init, lw=lw):
            qn, kf, vf = attn_prep_sample(proj, lw['q_norm_g'], lw['k_norm_g'])
            return sample_attn(qn, kf, vf, cache_k, cache_v, l, page_table, rel_bias, lam_vec[l], subln_g[l],
                               lam_init, db, t_new), kf, vf

        def s_ret(proj, cos, sin, col0, l=l):
            return retention_sample(proj, cos, sin, ret_norm_g[l], state_ret[l], db, t_new, col0)

        ys, st = _layer(ys, db, t_new, pos_s, s_attn, s_ret, state_conv[l], state_ffn[l], lw, l, False)
        sts_s.append(st)

    def stk(sts, i):
        return jnp.stack([st[i] for st in sts], axis=0)

    return (yp.reshape(bp, sp, d), ys.reshape(db, t_new, d),
            stk(sts_p, 0), stk(sts_p, 1), stk(sts_p, 2), stk(sts_p, 3), stk(sts_p, 4),
            stk(sts_s, 0), stk(sts_s, 1), stk(sts_s, 2), stk(sts_s, 3), stk(sts_s, 4))
```

```python
import functools
import math

import numpy as np
import jax
import jax.numpy as jnp
from jax import lax
from jax.experimental import pallas as pl
from jax.experimental.pallas import tpu as pltpu

F32 = jnp.float32
BF16 = jnp.bfloat16

H_A = 8
DK_A = 64
DV_A = 128
H_B = 8
DK_B = 64
DV_B = 128
K_CONV = 31
K_FFN = 3
N_BUCKETS = 32
MAX_DIST = 128
EPS = 1e-6
NEG = -1e30

LANES = 128
SUBLANES = 8
VMEM_LIMIT = 56 * 1024 * 1024

_NT = (((1,), (1,)), ((), ()))
_TN = (((0,), (0,)), ((), ()))


def _cparams(sem):
    return pltpu.CompilerParams(dimension_semantics=sem, vmem_limit_bytes=VMEM_LIMIT)


def _pick(n, cands):
    for c in cands:
        if n % c == 0:
            return c
    return n


def _t5_bucket_np(d):
    max_exact = N_BUCKETS // 2
    d = np.maximum(d, 0)
    df = np.maximum(d, 1).astype(np.float32)
    large = max_exact + (np.log(df / np.float32(max_exact)) / np.float32(math.log(MAX_DIST / max_exact))
                         * np.float32(N_BUCKETS - max_exact)).astype(np.int32)
    return np.where(d < max_exact, d, np.minimum(large, N_BUCKETS - 1)).astype(np.int32)


def _bias_from_buckets(idx, rb_ref, h):
    acc = jnp.zeros(idx.shape, F32)
    for b in range(N_BUCKETS):
        acc = jnp.where(idx == b, rb_ref[b, h], acc)
    return acc


def _lam(lv_ref, lam_init):
    lv = lv_ref[...]
    a = jnp.sum(lv[0:1] * lv[1:2], axis=-1, keepdims=True)
    b = jnp.sum(lv[2:3] * lv[3:4], axis=-1, keepdims=True)
    return jnp.exp(a) - jnp.exp(b) + lam_init


def _silu(x):
    return x * jax.nn.sigmoid(x)


def _norm_mm_body(x_ref, g_ref, w_ref, o_ref, h_ref):
    @pl.when(pl.program_id(1) == 0)
    def _():
        x = x_ref[...]
        ms = jnp.mean(x * x, axis=-1, keepdims=True)
        h_ref[...] = ((x * lax.rsqrt(ms + EPS)) * g_ref[...]).astype(BF16)

    o_ref[...] = jnp.dot(h_ref[...], w_ref[...], preferred_element_type=F32)


def norm_matmul(x, g, w, l):
    n, d = x.shape
    c = w.shape[2]
    tm = _pick(n, (1024, 512, 256, 128))
    tn = _pick(c, (2048, 1024, 512, 256, 128))
    return pl.pallas_call(
        _norm_mm_body,
        grid=(n // tm, c // tn),
        in_specs=[pl.BlockSpec((tm, d), lambda i, j: (i, 0)),
                  pl.BlockSpec((1, d), lambda i, j: (0, 0)),
                  pl.BlockSpec((None, d, tn), lambda i, j: (l, 0, j))],
        out_specs=pl.BlockSpec((tm, tn), lambda i, j: (i, j)),
        out_shape=jax.ShapeDtypeStruct((n, c), F32),
        scratch_shapes=[pltpu.VMEM((tm, d), BF16)],
        compiler_params=_cparams(("parallel", "arbitrary")),
        name="norm_matmul",
    )(x, g.reshape(1, d), w)


def _ffn_up_body(x_ref, g_ref, wg_ref, wu_ref, cw_ref, cb_ref, st_ref, y_ref, so_ref, h_ref, gbuf_ref, carry_ref,
                 *, tiles_per_seq):
    i = pl.program_id(0)
    j = pl.program_id(1)
    tm = x_ref.shape[0]
    lo = SUBLANES - (K_FFN - 1)

    @pl.when(j == 0)
    def _():
        x = x_ref[...]
        ms = jnp.mean(x * x, axis=-1, keepdims=True)
        h_ref[...] = ((x * lax.rsqrt(ms + EPS)) * g_ref[...]).astype(BF16)

    h = h_ref[...]
    first = (i % tiles_per_seq) == 0

    @pl.when(first)
    def _():
        gbuf_ref[0:lo, :] = jnp.zeros((lo, gbuf_ref.shape[1]), F32)
        gbuf_ref[lo:SUBLANES, :] = st_ref[0]

    @pl.when(jnp.logical_not(first))
    def _():
        gbuf_ref[0:SUBLANES, :] = carry_ref[j]

    gbuf_ref[SUBLANES:SUBLANES + tm, :] = jnp.dot(h, wg_ref[...], preferred_element_type=F32)
    acc = jnp.zeros((tm, gbuf_ref.shape[1]), F32)
    for k in range(K_FFN):
        acc = acc + gbuf_ref[lo + k:lo + k + tm, :] * cw_ref[k:k + 1, :]
    up = jnp.dot(h, wu_ref[...], preferred_element_type=F32)
    y_ref[...] = (_silu(acc + cb_ref[...]) * up).astype(BF16)
    carry_ref[j] = gbuf_ref[tm:tm + SUBLANES, :]
    so_ref[0] = gbuf_ref[lo + tm:SUBLANES + tm, :]


def ffn_up_prompt(x, g, w, l, state, conv_w, conv_b, batch, seq):
    n, d = x.shape
    f = w.shape[2] // 2
    tm = _pick(seq, (1024, 512, 256, 128))
    tn = _pick(f, (512, 256, 128))
    nj = f // tn
    tps = seq // tm
    act, tails = pl.pallas_call(
        functools.partial(_ffn_up_body, tiles_per_seq=tps),
        grid=(n // tm, nj),
        in_specs=[pl.BlockSpec((tm, d), lambda i, j: (i, 0)),
                  pl.BlockSpec((1, d), lambda i, j: (0, 0)),
                  pl.BlockSpec((None, d, tn), lambda i, j: (l, 0, j)),
                  pl.BlockSpec((None, d, tn), lambda i, j: (l, 0, nj + j)),
                  pl.BlockSpec((K_FFN, tn), lambda i, j: (0, j)),
                  pl.BlockSpec((1, tn), lambda i, j: (0, j)),
                  pl.BlockSpec((1, K_FFN - 1, tn), lambda i, j: (i // tps, 0, j))],
        out_specs=[pl.BlockSpec((tm, tn), lambda i, j: (i, j)),
                   pl.BlockSpec((1, K_FFN - 1, tn), lambda i, j: (i, 0, j))],
        out_shape=[jax.ShapeDtypeStruct((n, f), BF16),
                   jax.ShapeDtypeStruct((n // tm, K_FFN - 1, f), F32)],
        scratch_shapes=[pltpu.VMEM((tm, d), BF16),
                        pltpu.VMEM((SUBLANES + tm, tn), F32),
                        pltpu.VMEM((nj, SUBLANES, tn), F32)],
        compiler_params=_cparams(("arbitrary", "arbitrary")),
        name="ffn_up_prompt",
    )(x, g.reshape(1, d), w, w, conv_w, conv_b.reshape(1, f), state)
    return act, tails[tps - 1::tps]


def _mm_res_body(a_ref, w_ref, r_ref, o_ref, acc_ref, *, nk):
    k = pl.program_id(2)
    part = jnp.dot(a_ref[...], w_ref[...], preferred_element_type=F32)
    if nk == 1:
        o_ref[...] = r_ref[...] + part
    else:
        @pl.when(k == 0)
        def _():
            acc_ref[...] = part

        @pl.when(jnp.logical_and(k > 0, k < nk - 1))
        def _():
            acc_ref[...] += part

        @pl.when(k == nk - 1)
        def _():
            o_ref[...] = r_ref[...] + (acc_ref[...] + part)


def matmul_residual(a, w, l, res):
    n, kd = a.shape
    c = w.shape[2]
    tm = _pick(n, (1024, 512, 256, 128))
    tn = _pick(c, (1024, 512, 256, 128))
    tk = kd if kd <= 2048 else _pick(kd, (2816, 2048, 1024, 512))
    nk = kd // tk
    return pl.pallas_call(
        functools.partial(_mm_res_body, nk=nk),
        grid=(n // tm, c // tn, nk),
        in_specs=[pl.BlockSpec((tm, tk), lambda i, j, k: (i, k)),
                  pl.BlockSpec((None, tk, tn), lambda i, j, k: (l, k, j)),
                  pl.BlockSpec((tm, tn), lambda i, j, k: (i, j))],
        out_specs=pl.BlockSpec((tm, tn), lambda i, j, k: (i, j)),
        out_shape=jax.ShapeDtypeStruct((n, c), F32),
        scratch_shapes=[pltpu.VMEM((tm, tn), F32)],
        compiler_params=_cparams(("parallel", "parallel", "arbitrary")),
        name="matmul_residual",
    )(a, w, res)


def _group_rms(x, gain, gm):
    s = x * x
    hi = s.astype(BF16)
    lo = (s - hi.astype(F32)).astype(BF16)
    parts = []
    for c in range(x.shape[1] // LANES):
        sl = slice(c * LANES, (c + 1) * LANES)
        parts.append(jnp.dot(hi[:, sl], gm, preferred_element_type=F32)
                     + jnp.dot(lo[:, sl], gm, preferred_element_type=F32))
    ms = jnp.concatenate(parts, axis=-1) * (1.0 / DK_A)
    return (x * lax.rsqrt(ms + EPS)) * gain


_LOG2E = math.log2(math.e)
_Q_SCALE = DK_A ** -0.5 * _LOG2E
_DVX = DV_A + 16


def _attn_prep_sample_body(q_ref, k_ref, v_ref, qg_ref, kg_ref, gm_ref, qo_ref, kf_ref, vf_ref):
    gm = gm_ref[...]
    qo_ref[...] = _group_rms(q_ref[...], qg_ref[...], gm) * _Q_SCALE
    kf_ref[...] = _group_rms(k_ref[...], kg_ref[...], gm)
    vf_ref[...] = v_ref[...]


def _attn_prep_prompt_body(q_ref, k_ref, v_ref, qg_ref, kg_ref, gm_ref, qt_ref, kf_ref, kb_ref, vf_ref, vt_ref):
    gm = gm_ref[...]
    qn = _group_rms(q_ref[...], qg_ref[...], gm) * _Q_SCALE
    kn = _group_rms(k_ref[...], kg_ref[...], gm)
    kf_ref[...] = kn
    kb_ref[...] = kn.astype(BF16)
    v = v_ref[...]
    vf_ref[...] = v
    for h in range(H_A):
        hs = slice(h * LANES, (h + 1) * LANES)
        qt_ref[0, h, 0] = qn[:, hs].T.astype(BF16)
        vt_ref[0, h, 0, 0:DV_A, :] = v[:, hs].T.astype(BF16)
        vt_ref[0, h, 0, DV_A:_DVX, :] = jnp.ones((_DVX - DV_A, v.shape[0]), BF16)


def _attn_prep_consts(q_gain, k_gain):
    w = H_A * 2 * DK_A
    gm = np.kron(np.eye(LANES // DK_A, dtype=np.float32), np.ones((DK_A, DK_A), np.float32))
    return (jnp.tile(q_gain, w // DK_A).reshape(1, w), jnp.tile(k_gain, w // DK_A).reshape(1, w),
            jnp.asarray(gm, BF16))


def attn_prep_sample(proj, q_gain, k_gain):
    n = proj.shape[0]
    w = H_A * 2 * DK_A
    tm = _pick(n, (512, 256, 128))
    col = lambda c: pl.BlockSpec((tm, w), lambda i, c=c: (i, c))
    cst = lambda shape: pl.BlockSpec(shape, lambda i: (0, 0))
    out = lambda: pl.BlockSpec((tm, w), lambda i: (i, 0))
    return pl.pallas_call(
        _attn_prep_sample_body,
        grid=(n // tm,),
        in_specs=[col(0), col(1), col(2), cst((1, w)), cst((1, w)), cst((LANES, LANES))],
        out_specs=[out(), out(), out()],
        out_shape=[jax.ShapeDtypeStruct((n, w), F32)] * 3,
        compiler_params=_cparams(("parallel",)),
        name="attn_prep_sample",
    )(proj, proj, proj, *_attn_prep_consts(q_gain, k_gain))


def attn_prep_prompt(proj, q_gain, k_gain, batch, seq, t):
    n = batch * seq
    w = H_A * 2 * DK_A
    nt = seq // t
    col = lambda c: pl.BlockSpec((t, w), lambda b, i, c=c: (b * nt + i, c))
    cst = lambda shape: pl.BlockSpec(shape, lambda b, i: (0, 0))
    out = lambda: pl.BlockSpec((t, w), lambda b, i: (b * nt + i, 0))
    tr = lambda rows=LANES: pl.BlockSpec((1, H_A, 1, rows, t), lambda b, i: (b, 0, i, 0, 0))
    tr_shape = lambda rows=LANES: jax.ShapeDtypeStruct((batch, H_A, nt, rows, t), BF16)
    return pl.pallas_call(
        _attn_prep_prompt_body,
        grid=(batch, nt),
        in_specs=[col(0), col(1), col(2), cst((1, w)), cst((1, w)), cst((LANES, LANES))],
        out_specs=[tr(), out(), out(), out(), tr(_DVX)],
        out_shape=[tr_shape(), jax.ShapeDtypeStruct((n, w), F32), jax.ShapeDtypeStruct((n, w), BF16),
                   jax.ShapeDtypeStruct((n, w), F32), tr_shape(_DVX)],
        compiler_params=_cparams(("parallel", "parallel")),
        name="attn_prep_prompt",
    )(proj, proj, proj, *_attn_prep_consts(q_gain, k_gain))


def _prompt_attn_body(rb_ref, ib_ref, lv_ref, sg_ref, qt_ref, k_ref, vt_ref, o_ref,
                      bias_ref, m_ref, acc_ref, *, t, lam_init, n_hg):
    hg = pl.program_id(1)
    qi = pl.program_id(2)
    heads = range(_ATTN_HEADS)
    build = qi == 0
    if n_hg == 1:
        build = jnp.logical_and(build, pl.program_id(0) == 0)

    @pl.when(build)
    def _():
        for g in heads:
            h = hg * _ATTN_HEADS + g
            far_bias = rb_ref[N_BUCKETS - 1, h]
            for o in range(2):
                idx = ib_ref[o]
                bias_ref[g, o] = jnp.where(idx >= 0, (_bias_from_buckets(idx, rb_ref, h) - far_bias) * _LOG2E, NEG)

    q2t = []
    for g in heads:
        qt = qt_ref[0, g, 0]
        row = lax.broadcasted_iota(jnp.int32, qt.shape, 0)
        zero = jnp.zeros_like(qt)
        q2t.append(jnp.concatenate([jnp.where(row < DK_A, qt, zero), jnp.where(row >= DK_A, qt, zero)], axis=1))

    m_ref[...] = jnp.full(m_ref.shape, NEG, F32)
    acc_ref[...] = jnp.zeros(acc_ref.shape, F32)

    def tile(kj, o):
        off = pl.multiple_of(kj * t, t)
        s = [jnp.dot(k_ref[0, pl.ds(off, t), g * LANES:(g + 1) * LANES], q2t[g], preferred_element_type=F32)
             for g in heads]
        if o is not None:
            s = [s[g] + bias_ref[g, o] for g in heads]
        m_old = [m_ref[g] for g in heads]
        m_new = [jnp.maximum(m_old[g], jnp.max(s[g], axis=0, keepdims=True)) for g in heads]
        alpha = [jnp.exp2(m_old[g] - m_new[g]) for g in heads]
        p = [jnp.exp2(s[g] - m_new[g]).astype(BF16) for g in heads]
        for g in heads:
            m_ref[g] = m_new[g]
        pv = [jnp.dot(vt_ref[0, g, kj], p[g], preferred_element_type=F32) for g in heads]
        for g in heads:
            acc_ref[g] = alpha[g] * acc_ref[g] + pv[g]

    def far(kj, carry):
        tile(kj, None)
        return carry

    lax.fori_loop(0, jnp.maximum(qi - 1, 0), far, 0)

    @pl.when(qi >= 1)
    def _():
        tile(qi - 1, 1)

    tile(qi, 0)

    lam = _lam(lv_ref, lam_init)
    for g in heads:
        o2t = acc_ref[g, 0:DV_A, :] / acc_ref[g, DV_A:DV_A + 1, :]
        o = (o2t[:, 0:t] - lam * o2t[:, t:2 * t]).T
        ms = jnp.mean(o * o, axis=-1, keepdims=True)
        y = ((o * lax.rsqrt(ms + EPS)) * sg_ref[...]) * (1.0 - lam_init)
        o_ref[0, :, g * LANES:(g + 1) * LANES] = y.astype(BF16)


_ATTN_TILE = 256
_ATTN_HEADS = 8


def prompt_attn(qt, kb, vt, rel_bias, lam_vec, subln_g, lam_init, batch, seq, t):
    w = H_A * DV_A
    assert t >= MAX_DIST and seq % t == 0
    nq = seq // t
    hgs = _ATTN_HEADS
    key = np.arange(t)[:, None]
    qry = (np.arange(2 * t) % t)[None, :]
    d0 = qry - key
    ib = jnp.asarray(np.stack([np.where(d0 >= 0, _t5_bucket_np(d0), -1), _t5_bucket_np(d0 + t)]), jnp.int32)
    assert int(_t5_bucket_np(np.array([t + 1]))[0]) == N_BUCKETS - 1
    out = pl.pallas_call(
        functools.partial(_prompt_attn_body, t=t, lam_init=lam_init, n_hg=H_A // hgs),
        grid=(batch, H_A // hgs, nq),
        in_specs=[pl.BlockSpec(memory_space=pltpu.SMEM),
                  pl.BlockSpec((2, t, 2 * t), lambda b, h, i: (0, 0, 0)),
                  pl.BlockSpec((4, DK_A), lambda b, h, i: (0, 0)),
                  pl.BlockSpec((1, DV_A), lambda b, h, i: (0, 0)),
                  pl.BlockSpec((1, hgs, 1, LANES, t), lambda b, h, i: (b, h, i, 0, 0)),
                  pl.BlockSpec((1, seq, hgs * LANES), lambda b, h, i: (b, 0, h)),
                  pl.BlockSpec((1, hgs, nq, _DVX, t), lambda b, h, i: (b, h, 0, 0, 0))],
        out_specs=pl.BlockSpec((1, t, hgs * LANES), lambda b, h, i: (b, i, h)),
        out_shape=jax.ShapeDtypeStruct((batch, seq, w), BF16),
        scratch_shapes=[pltpu.VMEM((hgs, 2, t, 2 * t), F32),
                        pltpu.VMEM((hgs, 1, 2 * t), F32),
                        pltpu.VMEM((hgs, _DVX, 2 * t), F32)],
        compiler_params=_cparams(("arbitrary", "arbitrary", "arbitrary")),
        name="prompt_attn",
    )(rel_bias, ib, lam_vec, subln_g.reshape(1, DV_A), qt, kb.reshape(batch, seq, w), vt)
    return out.reshape(batch * seq, w)


_RH = SUBLANES


_KPAD = LANES // H_A


def _sample_attn_body(pt_ref, rb_ref, ibp_ref, ibn_ref, lv_ref, sg_ref, q_ref, kn_ref, vn_ref, *rest,
                      pps, nsteps, lam_init):
    k_refs = rest[0:pps]
    v_refs = rest[pps:2 * pps]
    o_ref = rest[2 * pps]
    biasp_ref, biasn_ref, m_ref, l_ref, acc_ref = rest[2 * pps + 1:]
    b = pl.program_id(0)
    s_id = pl.program_id(1)

    t_new = o_ref.shape[1]

    @pl.when(jnp.logical_and(b == 0, s_id == 0))
    def _():
        for h in range(H_A):
            sl = slice(h * _RH, (h + 1) * _RH)
            far_bias = rb_ref[N_BUCKETS - 1, h]
            for o in range(2):
                idx = ibp_ref[o, sl, :]
                biasp_ref[o, sl, :] = jnp.where(idx >= 0, (_bias_from_buckets(idx, rb_ref, h) - far_bias) * _LOG2E,
                                                NEG)
            idx = ibn_ref[sl, :]
            biasn_ref[sl, :] = jnp.where(idx >= 0, (_bias_from_buckets(idx, rb_ref, h) - far_bias) * _LOG2E, NEG)

    @pl.when(s_id == 0)
    def _():
        m_ref[...] = jnp.full(m_ref.shape, NEG, F32)
        l_ref[...] = jnp.zeros(l_ref.shape, F32)
        acc_ref[...] = jnp.zeros(acc_ref.shape, F32)

    q = q_ref[0]
    row = lax.broadcasted_iota(jnp.int32, (_RH, LANES), 0)
    lane = lax.broadcasted_iota(jnp.int32, (_RH, LANES), 1)
    keep = (row < t_new) == (lane < DK_A)
    q2 = jnp.concatenate([jnp.where(keep, q[:, h * LANES:(h + 1) * LANES], 0.0) for h in range(H_A)],
                         axis=0).astype(BF16)

    def attend(kt, vt, bias):
        s = lax.dot_general(q2, kt, _NT, preferred_element_type=F32) + bias
        m_old = m_ref[...]
        m_new = jnp.maximum(m_old, jnp.max(s, axis=-1, keepdims=True))
        alpha = jnp.exp2(m_old - m_new)
        p = jnp.exp2(s - m_new)
        l_ref[...] = alpha * l_ref[...] + jnp.sum(p, axis=-1, keepdims=True)
        acc_ref[...] = alpha * acc_ref[...] + jnp.dot(p.astype(BF16), vt, preferred_element_type=F32)
        m_ref[...] = m_new

    kt = jnp.concatenate([r[0, 0].astype(BF16) for r in k_refs], axis=0)
    vt = jnp.concatenate([r[0, 0].astype(BF16) for r in v_refs], axis=0)
    last = s_id == nsteps - 1
    bias = jnp.concatenate([biasp_ref[0]] * (pps - 1) + [biasp_ref[jnp.where(last, 1, 0)]], axis=1)
    attend(kt, vt, bias)

    @pl.when(last)
    def _():
        attend(kn_ref[0].astype(BF16), vn_ref[0].astype(BF16), biasn_ref[...])
        o2 = acc_ref[...] / l_ref[...]
        lam = _lam(lv_ref, lam_init)
        for h in range(H_A):
            x = o2[h * _RH:(h + 1) * _RH]
            o = x - lam * pltpu.roll(x, t_new, 0)
            ms = jnp.mean(o * o, axis=-1, keepdims=True)
            y = ((o * lax.rsqrt(ms + EPS)) * sg_ref[...]) * (1.0 - lam_init)
            o_ref[0, :, h * LANES:(h + 1) * LANES] = y[0:t_new]


def sample_attn(qn, kf, vf, cache_k, cache_v, layer, page_table, rel_bias, lam_vec, subln_g, lam_init, db, t_new):
    w = H_A * DV_A
    depth, n_pool, page = cache_k.shape[:3]
    n_pages = page_table.shape[1]
    assert 2 * t_new == _RH and t_new <= _KPAD and page >= MAX_DIST
    pps = _pick(n_pages, (16, 8, 4, 2, 1))
    nsteps = n_pages // pps
    rows = H_A * _RH
    kw = page * H_A
    rh = (np.arange(rows) // _RH)[:, None]
    rt = (np.arange(rows) % t_new)[:, None]
    ck_key, ck_head = (np.arange(kw) // H_A)[None, :], (np.arange(kw) % H_A)[None, :]
    ib_far = np.where(rh == ck_head, N_BUCKETS - 1, -1)
    ib_last = np.where(rh == ck_head, _t5_bucket_np(page + rt - ck_key), -1)
    nk_key, nk_head = (np.arange(_KPAD * H_A) // H_A)[None, :], (np.arange(_KPAD * H_A) % H_A)[None, :]
    d_new = rt - nk_key
    ib_new = np.where((rh == nk_head) & (d_new >= 0) & (nk_key < t_new), _t5_bucket_np(d_new), -1)
    ibp = jnp.asarray(np.stack([ib_far, ib_last]), jnp.int32)
    ibn = jnp.asarray(ib_new, jnp.int32)
    assert int(_t5_bucket_np(np.array([page + 1]))[0]) == N_BUCKETS - 1

    q3 = qn.reshape(db, t_new, w)
    q_pad = jnp.concatenate([q3, q3], axis=1)

    def new_rows(x):
        return jnp.pad(x.reshape(db, t_new * H_A, LANES), ((0, 0), (0, (_KPAD - t_new) * H_A), (0, 0)))

    ck = cache_k.reshape(depth, n_pool, kw, LANES)
    cv = cache_v.reshape(depth, n_pool, kw, LANES)
    new_spec = pl.BlockSpec((1, _KPAD * H_A, LANES), lambda b, s, pt: (b, 0, 0))

    def page_spec(i):
        return pl.BlockSpec((1, 1, kw, LANES), lambda b, s, pt, i=i: (layer, pt[b, s * pps + i], 0, 0))

    grid_spec = pltpu.PrefetchScalarGridSpec(
        num_scalar_prefetch=1,
        grid=(db, nsteps),
        in_specs=[pl.BlockSpec(memory_space=pltpu.SMEM),
                  pl.BlockSpec((2, rows, kw), lambda b, s, pt: (0, 0, 0)),
                  pl.BlockSpec((rows, _KPAD * H_A), lambda b, s, pt: (0, 0)),
                  pl.BlockSpec((4, DK_A), lambda b, s, pt: (0, 0)),
                  pl.BlockSpec((1, DV_A), lambda b, s, pt: (0, 0)),
                  pl.BlockSpec((1, _RH, w), lambda b, s, pt: (b, 0, 0)),
                  new_spec, new_spec]
                 + [page_spec(i) for i in range(pps)] + [page_spec(i) for i in range(pps)],
        out_specs=pl.BlockSpec((1, t_new, w), lambda b, s, pt: (b, 0, 0)),
        scratch_shapes=[pltpu.VMEM((2, rows, kw), F32),
                        pltpu.VMEM((rows, _KPAD * H_A), F32),
                        pltpu.VMEM((rows, 1), F32),
                        pltpu.VMEM((rows, 1), F32),
                        pltpu.VMEM((rows, DV_A), F32)],
    )
    out = pl.pallas_call(
        functools.partial(_sample_attn_body, pps=pps, nsteps=nsteps, lam_init=lam_init),
        grid_spec=grid_spec,
        out_shape=jax.ShapeDtypeStruct((db, t_new, w), F32),
        compiler_params=_cparams(("arbitrary", "arbitrary")),
        name="sample_attn",
    )(page_table, rel_bias, ibp, ibn, lam_vec, subln_g.reshape(1, DV_A), q_pad, new_rows(kf), new_rows(vf),
      *([ck] * pps), *([cv] * pps))
    return out.reshape(db * t_new, w).astype(BF16)


def _log_gamma(h):
    return float(np.log1p(-np.exp2(np.float32(-5.0 - h), dtype=np.float32), dtype=np.float32))


def _swap_halves(x):
    n = x.shape[1]
    half = DK_B // 2
    lane = lax.broadcasted_iota(jnp.int32, x.shape, 1)
    return jnp.where((lane % DK_B) < half, pltpu.roll(x, n - half, 1), pltpu.roll(x, half, 1))


def _rotary(x, cos, sin):
    return x * cos + _swap_halves(x) * sin


def _ret_tile(q, k, v, c):
    r = q.shape[0]
    shift = int(math.log2(c))
    assert 1 << shift == c
    ri = lax.broadcasted_iota(jnp.int32, (r, r), 0)
    ci = lax.broadcasted_iota(jnp.int32, (r, r), 1)
    same = (ri >> shift) == (ci >> shift)
    diff = ((ri & (c - 1)) - (ci & (c - 1))).astype(F32)
    valid = jnp.logical_and(same, diff >= 0.0)
    ti = (lax.broadcasted_iota(jnp.int32, (r, 1), 0) & (c - 1)).astype(F32)
    lane = lax.broadcasted_iota(jnp.int32, (r, LANES), 1)
    first = lane < DK_B
    o_inner, q_m, dec_q, k_dec = [], [], [], []
    for p in range(H_B // 2):
        ps = slice(p * LANES, (p + 1) * LANES)
        qp, kp = q[:, ps], k[:, ps]
        kpb = kp.astype(BF16)
        lg0, lg1 = _log_gamma(2 * p), _log_gamma(2 * p + 1)
        k_dec.append((kp * jnp.where(first, jnp.exp((c - 1.0 - ti) * lg0), jnp.exp((c - 1.0 - ti) * lg1))).astype(BF16))
        for hh in range(2):
            h = 2 * p + hh
            lg = lg1 if hh else lg0
            qm = jnp.where(first if hh == 0 else jnp.logical_not(first), qp, 0.0).astype(BF16)
            dmat = jnp.where(valid, jnp.exp(jnp.maximum(diff, 0.0) * lg), 0.0)
            inner = lax.dot_general(qm, kpb, _NT, preferred_element_type=F32) * dmat
            vh = v[:, h * DV_B:(h + 1) * DV_B].astype(BF16)
            o_inner.append(jnp.dot(inner.astype(BF16), vh, preferred_element_type=F32))
            q_m.append(qm)
            dec_q.append(jnp.exp((ti + 1.0) * lg))
    return o_inner, q_m, dec_q, k_dec


def _state_update(s_pair, kd, v, p, c):
    row = lax.broadcasted_iota(jnp.int32, (LANES, DV_B), 0)
    top = row < DK_B
    u0 = lax.dot_general(kd, v[:, (2 * p) * DV_B:(2 * p + 1) * DV_B].astype(BF16), _TN, preferred_element_type=F32)
    u1 = lax.dot_general(kd, v[:, (2 * p + 1) * DV_B:(2 * p + 2) * DV_B].astype(BF16), _TN, preferred_element_type=F32)
    gc = jnp.where(top, math.exp(c * _log_gamma(2 * p)), math.exp(c * _log_gamma(2 * p + 1)))
    return s_pair * gc + jnp.where(top, u0, u1)


def _ret_epilogue(o, g, ng):
    ms = jnp.mean(o * o, axis=-1, keepdims=True)
    return ((o * lax.rsqrt(ms + EPS)) * ng) * _silu(g)


def _ret_prompt_body(q_ref, k_ref, v_ref, g_ref, cos_ref, sin_ref, ng_ref, y_ref, so_ref, s_ref, *, c):
    ci = pl.program_id(1)

    @pl.when(ci == 0)
    def _():
        s_ref[...] = jnp.zeros(s_ref.shape, F32)

    cos, sin = cos_ref[...], sin_ref[...]
    q = _rotary(q_ref[...], cos, sin)
    k = _rotary(k_ref[...], cos, sin) * (DK_B ** -0.5)
    v = v_ref[...]
    o_inner, q_m, dec_q, k_dec = _ret_tile(q, k, v, c)
    ng = ng_ref[...]
    for p in range(H_B // 2):
        s_pair = s_ref[p]
        sb = s_pair.astype(BF16)
        for hh in range(2):
            h = 2 * p + hh
            o = o_inner[h] + jnp.dot(q_m[h], sb, preferred_element_type=F32) * dec_q[h]
            hs = slice(h * DV_B, (h + 1) * DV_B)
            y_ref[:, hs] = _ret_epilogue(o, g_ref[:, hs], ng).astype(BF16)
        s_ref[p] = _state_update(s_pair, k_dec[p], v, p, c)

    @pl.when(ci == pl.num_programs(1) - 1)
    def _():
        so_ref[0] = s_ref[...]


def retention_prompt(proj, cos, sin, ret_norm_g, batch, seq, col0):
    n = batch * seq
    c = 128 if seq % 128 == 0 else seq
    nc = seq // c
    wq = H_B * DK_B
    wv = H_B * DV_B
    assert col0 % wv == 0
    qc, vc = col0 // wq, col0 // wv
    row = lambda b, i: b * nc + i
    y, st = pl.pallas_call(
        functools.partial(_ret_prompt_body, c=c),
        grid=(batch, nc),
        in_specs=[pl.BlockSpec((c, wq), lambda b, i: (row(b, i), qc)),
                  pl.BlockSpec((c, wq), lambda b, i: (row(b, i), qc + 1)),
                  pl.BlockSpec((c, wv), lambda b, i: (row(b, i), vc + 1)),
                  pl.BlockSpec((c, wv), lambda b, i: (row(b, i), vc + 2)),
                  pl.BlockSpec((c, wq), lambda b, i: (i, 0)),
                  pl.BlockSpec((c, wq), lambda b, i: (i, 0)),
                  pl.BlockSpec((1, DV_B), lambda b, i: (0, 0))],
        out_specs=[pl.BlockSpec((c, wv), lambda b, i: (row(b, i), 0)),
                   pl.BlockSpec((1, H_B // 2, LANES, DV_B), lambda b, i: (b, 0, 0, 0))],
        out_shape=[jax.ShapeDtypeStruct((n, wv), BF16),
                   jax.ShapeDtypeStruct((batch, H_B // 2, LANES, DV_B), F32)],
        scratch_shapes=[pltpu.VMEM((H_B // 2, LANES, DV_B), F32)],
        compiler_params=_cparams(("arbitrary", "arbitrary")),
        name="retention_prompt",
    )(proj, proj, proj, proj, cos, sin, ret_norm_g.reshape(1, DV_B))
    return y, st.reshape(batch, H_B, DK_B, DV_B)


def _ret_sample_body(q_ref, k_ref, v_ref, g_ref, cos_ref, sin_ref, ng_ref, s0_ref, y_ref, so_ref,
                     o_ref, qm_ref, kd_ref, *, c):
    gi = pl.program_id(0)
    r = q_ref.shape[0]
    shift = int(math.log2(c))
    v = v_ref[...]

    @pl.when(gi == 0)
    def _():
        cos, sin = cos_ref[...], sin_ref[...]
        q = _rotary(q_ref[...], cos, sin)
        k = _rotary(k_ref[...], cos, sin) * (DK_B ** -0.5)
        o_inner, q_m, dec_q, k_dec = _ret_tile(q, k, v, c)
        for h in range(H_B):
            o_ref[:, h * DV_B:(h + 1) * DV_B] = o_inner[h]
            qm_ref[h] = q_m[h]
        for p in range(H_B // 2):
            kd_ref[p] = k_dec[p]

    rsel = (lax.broadcasted_iota(jnp.int32, (r, LANES), 0) >> shift) == gi
    ti = (lax.broadcasted_iota(jnp.int32, (r, 1), 0) & (c - 1)).astype(F32)
    zero = jnp.zeros((r, LANES), BF16)
    for p in range(H_B // 2):
        s_pair = s0_ref[0, p]
        sb = s_pair.astype(BF16)
        for hh in range(2):
            h = 2 * p + hh
            qm = jnp.where(rsel, qm_ref[h], zero)
            hs = slice(h * DV_B, (h + 1) * DV_B)
            o_ref[:, hs] += jnp.dot(qm, sb, preferred_element_type=F32) * jnp.exp((ti + 1.0) * _log_gamma(h))
        kd = jnp.where(rsel, kd_ref[p], zero)
        so_ref[0, p] = _state_update(s_pair, kd, v, p, c)

    @pl.when(gi == pl.num_programs(0) - 1)
    def _():
        ng = ng_ref[...]
        for h in range(H_B):
            hs = slice(h * DV_B, (h + 1) * DV_B)
            y_ref[:, hs] = _ret_epilogue(o_ref[:, hs], g_ref[:, hs], ng).astype(BF16)


def retention_sample(proj, cos, sin, ret_norm_g, state0, db, t_new, col0):
    n = db * t_new
    wq = H_B * DK_B
    wv = H_B * DV_B
    qc, vc = col0 // wq, col0 // wv
    s0 = state0.reshape(db, H_B // 2, LANES, DV_B)
    st_spec = pl.BlockSpec((1, H_B // 2, LANES, DV_B), lambda g: (g, 0, 0, 0))
    y, st = pl.pallas_call(
        functools.partial(_ret_sample_body, c=t_new),
        grid=(db,),
        in_specs=[pl.BlockSpec((n, wq), lambda g: (0, qc)),
                  pl.BlockSpec((n, wq), lambda g: (0, qc + 1)),
                  pl.BlockSpec((n, wv), lambda g: (0, vc + 1)),
                  pl.BlockSpec((n, wv), lambda g: (0, vc + 2)),
                  pl.BlockSpec((n, wq), lambda g: (0, 0)),
                  pl.BlockSpec((n, wq), lambda g: (0, 0)),
                  pl.BlockSpec((1, DV_B), lambda g: (0, 0)),
                  st_spec],
        out_specs=[pl.BlockSpec((n, wv), lambda g: (0, 0)), st_spec],
        out_shape=[jax.ShapeDtypeStruct((n, wv), BF16),
                   jax.ShapeDtypeStruct((db, H_B // 2, LANES, DV_B), F32)],
        scratch_shapes=[pltpu.VMEM((n, wv), F32),
                        pltpu.VMEM((H_B, n, LANES), BF16),
                        pltpu.VMEM((H_B // 2, n, LANES), BF16)],
        compiler_params=_cparams(("arbitrary",)),
        name="retention_sample",
    )(proj, proj, proj, proj, cos, sin, ret_norm_g.reshape(1, DV_B), s0)
    return y, st.reshape(db, H_B, DK_B, DV_B)


def _rotary_tables(pos):
    half = DK_B // 2
    inv = 1.0 / (10000.0 ** (jnp.arange(half, dtype=F32) / half))
    ang = pos.astype(F32)[:, None] * inv[None, :]
    cos, sin = jnp.cos(ang), jnp.sin(ang)
    cos_t = jnp.tile(jnp.concatenate([cos, cos], axis=-1), (1, H_B))
    sin_t = jnp.tile(jnp.concatenate([-sin, sin], axis=-1), (1, H_B))
    return cos_t, sin_t


_HALO = 32
_CONV_ROWS = 64


def _conv_body(a_ref, b_ref, st_ref, w_ref, cb_ref, lg_ref, lb_ref, y_ref, so_ref, full_ref, sh_ref, uc_ref, *, ts):
    ti = pl.program_id(1)
    lo = _HALO - (K_CONV - 1)

    @pl.when(ti == 0)
    def _():
        full_ref[0:lo, :] = jnp.zeros((lo, full_ref.shape[1]), F32)
        full_ref[lo:_HALO, :] = st_ref[0]

    full_ref[_HALO:_HALO + ts, :] = a_ref[0] * jax.nn.sigmoid(b_ref[0])
    nsh = sh_ref.shape[1]
    for s in range(1, SUBLANES):
        sh_ref[s - 1] = full_ref[s:s + nsh, :]
    rb = min(ts, _CONV_ROWS)
    for c in range(full_ref.shape[1] // LANES):
        cs = slice(c * LANES, (c + 1) * LANES)
        for r0 in range(0, ts, rb):
            acc = None
            for j in range(K_CONV):
                s = (lo + j) % SUBLANES
                a0 = lo + j - s + r0
                win = full_ref[a0:a0 + rb, cs] if s == 0 else sh_ref[s - 1, a0:a0 + rb, cs]
                term = win * w_ref[j:j + 1, cs]
                acc = term if acc is None else acc + term
            uc_ref[r0:r0 + rb, cs] = acc + cb_ref[:, cs]
    uc = uc_ref[...]
    mu = jnp.mean(uc, axis=-1, keepdims=True)
    dev = uc - mu
    var = jnp.mean(dev * dev, axis=-1, keepdims=True)
    y = (dev * lax.rsqrt(var + EPS)) * lg_ref[...] + lb_ref[...]
    y_ref[0] = _silu(y).astype(BF16)
    so_ref[0] = full_ref[lo + ts:_HALO + ts, :]
    if ts >= _HALO:
        full_ref[0:_HALO, :] = full_ref[ts:ts + _HALO, :]


def conv_module(proj3, state, conv_w, conv_b, ln_g, ln_b, col0):
    b, s, _ = proj3.shape
    c = conv_w.shape[1]
    ts = _pick(s, (256, 128)) if s >= 128 else s
    nt = s // ts
    assert nt == 1 or ts >= _HALO
    cc = col0 // c
    vec = lambda: pl.BlockSpec((1, c), lambda i, t: (0, 0))
    return pl.pallas_call(
        functools.partial(_conv_body, ts=ts),
        grid=(b, nt),
        in_specs=[pl.BlockSpec((1, ts, c), lambda i, t: (i, t, cc)),
                  pl.BlockSpec((1, ts, c), lambda i, t: (i, t, cc + 1)),
                  pl.BlockSpec((1, K_CONV - 1, c), lambda i, t: (i, 0, 0)),
                  pl.BlockSpec((K_CONV, c), lambda i, t: (0, 0)),
                  vec(), vec(), vec()],
        out_specs=[pl.BlockSpec((1, ts, c), lambda i, t: (i, t, 0)),
                   pl.BlockSpec((1, K_CONV - 1, c), lambda i, t: (i, 0, 0))],
        out_shape=[jax.ShapeDtypeStruct((b, s, c), BF16),
                   jax.ShapeDtypeStruct((b, K_CONV - 1, c), F32)],
        scratch_shapes=[pltpu.VMEM((_HALO + ts, c), F32),
                        pltpu.VMEM((SUBLANES - 1, _HALO + ts - SUBLANES, c), F32),
                        pltpu.VMEM((ts, c), F32)],
        compiler_params=_cparams(("arbitrary", "arbitrary")),
        name="conv_module",
    )(proj3, proj3, state, conv_w, conv_b.reshape(1, c), ln_g.reshape(1, c), ln_b.reshape(1, c))


def _merge_body(a_ref, b_ref, c_ref, wa_ref, wb_ref, wc_ref, ga_ref, gb_ref, gc_ref,
                ba_ref, bb_ref, bc_ref, o_ref):
    ya = jnp.dot(a_ref[...], wa_ref[...], preferred_element_type=F32)
    m = jax.nn.sigmoid(ga_ref[...] + ba_ref[...]) * ya
    yb = jnp.dot(b_ref[...], wb_ref[...], preferred_element_type=F32)
    m = m + jax.nn.sigmoid(gb_ref[...] + bb_ref[...]) * yb
    yc = jnp.dot(c_ref[...], wc_ref[...], preferred_element_type=F32)
    m = m + jax.nn.sigmoid(gc_ref[...] + bc_ref[...]) * yc
    o_ref[...] = m.astype(BF16)


def branch_merge(a, b, c, wa, wb, wc, l, proj, b_gate, col0):
    n, kd = a.shape
    d = wa.shape[2]
    tm = _pick(n, (512, 256, 128))
    tn = _pick(d, (1024, 512, 256, 128))
    nj = d // tn
    g0 = col0 // tn
    bg = b_gate.reshape(1, 3 * d)
    x_spec = lambda: pl.BlockSpec((tm, kd), lambda j, i: (i, 0))
    w_spec = lambda: pl.BlockSpec((None, kd, tn), lambda j, i: (l, 0, j))
    g_spec = lambda k: pl.BlockSpec((tm, tn), lambda j, i, k=k: (i, g0 + k * nj + j))
    bias_spec = lambda k: pl.BlockSpec((1, tn), lambda j, i, k=k: (0, k * nj + j))
    return pl.pallas_call(
        _merge_body,
        grid=(nj, n // tm),
        in_specs=[x_spec(), x_spec(), x_spec(), w_spec(), w_spec(), w_spec(),
                  g_spec(0), g_spec(1), g_spec(2), bias_spec(0), bias_spec(1), bias_spec(2)],
        out_specs=pl.BlockSpec((tm, tn), lambda j, i: (i, j)),
        out_shape=jax.ShapeDtypeStruct((n, d), BF16),
        compiler_params=_cparams(("parallel", "arbitrary")),
        name="branch_merge",
    )(a, b, c, wa, wb, wc, proj, proj, proj, bg, bg, bg)


def _ffn_act_body(g_ref, u_ref, st_ref, w_ref, cb_ref, y_ref, so_ref, full_ref, *, ts):
    ti = pl.program_id(2)
    lo = SUBLANES - (K_FFN - 1)

    @pl.when(ti == 0)
    def _():
        full_ref[0:lo, :] = jnp.zeros((lo, full_ref.shape[1]), F32)
        full_ref[lo:SUBLANES, :] = st_ref[0]

    full_ref[SUBLANES:SUBLANES + ts, :] = g_ref[0]
    acc = jnp.zeros((ts, full_ref.shape[1]), F32)
    for j in range(K_FFN):
        acc = acc + full_ref[lo + j:lo + j + ts, :] * w_ref[j:j + 1, :]
    y_ref[0] = (_silu(acc + cb_ref[...]) * u_ref[0]).astype(BF16)
    so_ref[0] = full_ref[lo + ts:SUBLANES + ts, :]
    if ts >= SUBLANES:
        full_ref[0:SUBLANES, :] = full_ref[ts:ts + SUBLANES, :]


def ffn_act(h3, state, conv_w, conv_b):
    b, s, f2 = h3.shape
    f = f2 // 2
    ts = _pick(s, (512, 256, 128)) if s >= 128 else s
    tc = _pick(f, (512, 256, 128)) if ts >= 128 else f
    nc = f // tc
    nt = s // ts
    assert nt == 1 or ts >= SUBLANES
    return pl.pallas_call(
        functools.partial(_ffn_act_body, ts=ts),
        grid=(b, nc, nt),
        in_specs=[pl.BlockSpec((1, ts, tc), lambda i, c, t: (i, t, c)),
                  pl.BlockSpec((1, ts, tc), lambda i, c, t: (i, t, nc + c)),
                  pl.BlockSpec((1, K_FFN - 1, tc), lambda i, c, t: (i, 0, c)),
                  pl.BlockSpec((K_FFN, tc), lambda i, c, t: (0, c)),
                  pl.BlockSpec((1, tc), lambda i, c, t: (0, c))],
        out_specs=[pl.BlockSpec((1, ts, tc), lambda i, c, t: (i, t, c)),
                   pl.BlockSpec((1, K_FFN - 1, tc), lambda i, c, t: (i, 0, c))],
        out_shape=[jax.ShapeDtypeStruct((b, s, f), BF16),
                   jax.ShapeDtypeStruct((b, K_FFN - 1, f), F32)],
        scratch_shapes=[pltpu.VMEM((SUBLANES + ts, tc), F32)],
        compiler_params=_cparams(("arbitrary", "arbitrary", "arbitrary")),
        name="ffn_act",
    )(h3, h3, state, conv_w, conv_b.reshape(1, f))


def _layer(x, batch, seq, pos, attn_fn, ret_fn, conv0, ffn0, lw, l, fuse_ffn):
    n, d = x.shape
    a_cols = H_A * (4 * DK_A + DV_A)
    b_cols = H_B * (2 * DK_B + 2 * DV_B)
    c_conv = lw['conv_w'].shape[1]
    proj = norm_matmul(x, lw['norm1_g'], lw['w_in'], l)
    a_in, kf, vf = attn_fn(proj)
    cos, sin = _rotary_tables(pos)
    b_in, ret_new = ret_fn(proj, cos, sin, a_cols)
    c_in, conv_new = conv_module(proj.reshape(batch, seq, -1), conv0, lw['conv_w'], lw['conv_b'],
                                 lw['conv_ln_g'], lw['conv_ln_b'], a_cols + b_cols)
    c_in = c_in.reshape(n, c_conv)
    m = branch_merge(a_in, b_in, c_in, lw['w_o_a'], lw['w_o_b'], lw['w_o_c'], l, proj, lw['b_gate'],
                     a_cols + b_cols + 2 * c_conv)
    x = matmul_residual(m, lw['w_out'], l, x)
    if fuse_ffn:
        act, ffn_new = ffn_up_prompt(x, lw['norm2_g'], lw['w_ffn_in'], l, ffn0, lw['ffn_conv_w'], lw['ffn_conv_b'],
                                     batch, seq)
    else:
        h2 = norm_matmul(x, lw['norm2_g'], lw['w_ffn_in'], l)
        act, ffn_new = ffn_act(h2.reshape(batch, seq, -1), ffn0, lw['ffn_conv_w'], lw['ffn_conv_b'])
        act = act.reshape(n, -1)
    x = matmul_residual(act, lw['w_ffn_down'], l, x)
    states = (kf.reshape(batch, seq, H_A, 2 * DK_A), vf.reshape(batch, seq, H_A, DV_A), ret_new, conv_new, ffn_new)
    return x, states


def kernel(x_prompt, x_sample, cache_k, cache_v, state_ret, state_conv, state_ffn, page_table, rel_bias, norm1_g, w_in, b_gate, q_norm_g, k_norm_g, lam_vec, subln_g, w_o_a, ret_norm_g, w_o_b, conv_w, conv_b, conv_ln_g, conv_ln_b, w_o_c, w_out, norm2_g, w_ffn_in, ffn_conv_w, ffn_conv_b, w_ffn_down):
    bp, sp, d = x_prompt.shape
    db, t_new, _ = x_sample.shape
    depth = w_in.shape[0]
    past = page_table.shape[1] * cache_k.shape[2]
    pos_p = jnp.arange(sp)
    pos_s = jnp.tile(past + jnp.arange(t_new), db)
    yp = x_prompt.reshape(bp * sp, d)
    ys = x_sample.reshape(db * t_new, d)
    c_conv = conv_w.shape[2]
    d_ff = ffn_conv_w.shape[2]
    zero_conv = jnp.zeros((bp, K_CONV - 1, c_conv), F32)
    zero_ffn = jnp.zeros((bp, K_FFN - 1, d_ff), F32)
    sts_p, sts_s = [], []
    wb = {'w_in': w_in.astype(BF16), 'w_o_a': w_o_a.astype(BF16), 'w_o_b': w_o_b.astype(BF16),
          'w_o_c': w_o_c.astype(BF16), 'w_out': w_out.astype(BF16), 'w_ffn_in': w_ffn_in.astype(BF16),
          'w_ffn_down': w_ffn_down.astype(BF16)}
    for l in range(depth):
        lw = dict(wb)
        lw.update({'norm1_g': norm1_g[l], 'b_gate': b_gate[l], 'q_norm_g': q_norm_g[l], 'k_norm_g': k_norm_g[l],
                   'conv_w': conv_w[l], 'conv_b': conv_b[l], 'conv_ln_g': conv_ln_g[l], 'conv_ln_b': conv_ln_b[l],
                   'norm2_g': norm2_g[l], 'ffn_conv_w': ffn_conv_w[l], 'ffn_conv_b': ffn_conv_b[l]})
        lam_init = 0.8 - 0.6 * math.exp(-0.3 * l)

        def p_attn(proj, l=l, lam_init=lam_init, lw=lw):
            t = _pick(sp, (_ATTN_TILE, LANES))
            qt, kf, kb, vf, vt = attn_prep_prompt(proj, lw['q_norm_g'], lw['k_norm_g'], bp, sp, t)
            return prompt_attn(qt, kb, vt, rel_bias, lam_vec[l], subln_g[l], lam_init, bp, sp, t), kf, vf

        def p_ret(proj, cos, sin, col0, l=l):
            return retention_prompt(proj, cos, sin, ret_norm_g[l], bp, sp, col0)

        yp, st = _layer(yp, bp, sp, pos_p, p_attn, p_ret, zero_conv, zero_ffn, lw, l, True)
        sts_p.append(st)

        def s_attn(proj, l=l, lam_init=lam_init, lw=lw):
            qn, kf, vf = attn_prep_sample(proj, lw['q_norm_g'], lw['k_norm_g'])
            return sample_attn(qn, kf, vf, cache_k, cache_v, l, page_table, rel_bias, lam_vec[l], subln_g[l],
                               lam_init, db, t_new), kf, vf

        def s_ret(proj, cos, sin, col0, l=l):
            return retention_sample(proj, cos, sin, ret_norm_g[l], state_ret[l], db, t_new, col0)

        ys, st = _layer(ys, db, t_new, pos_s, s_attn, s_ret, state_conv[l], state_ffn[l], lw, l, False)
        sts_s.append(st)

    def stk(sts, i):
        return jnp.stack([st[i] for st in sts], axis=0)

    return (yp.reshape(bp, sp, d), ys.reshape(db, t_new, d),
            stk(sts_p, 0), stk(sts_p, 1), stk(sts_p, 2), stk(sts_p, 3), stk(sts_p, 4),
            stk(sts_s, 0), stk(sts_s, 1), stk(sts_s, 2), stk(sts_s, 3), stk(sts_s, 4))
```

```python
import functools
import math

import numpy as np
import jax
import jax.numpy as jnp
from jax import lax
from jax.experimental import pallas as pl
from jax.experimental.pallas import tpu as pltpu

F32 = jnp.float32
BF16 = jnp.bfloat16

H_A = 8
DK_A = 64
DV_A = 128
H_B = 8
DK_B = 64
DV_B = 128
K_CONV = 31
K_FFN = 3
N_BUCKETS = 32
MAX_DIST = 128
EPS = 1e-6
NEG = -1e30

LANES = 128
SUBLANES = 8
VMEM_LIMIT = 56 * 1024 * 1024

_NT = (((1,), (1,)), ((), ()))
_TN = (((0,), (0,)), ((), ()))


def _cparams(sem):
    return pltpu.CompilerParams(dimension_semantics=sem, vmem_limit_bytes=VMEM_LIMIT)


def _pick(n, cands):
    for c in cands:
        if n % c == 0:
            return c
    return n


def _t5_bucket_np(d):
    max_exact = N_BUCKETS // 2
    d = np.maximum(d, 0)
    df = np.maximum(d, 1).astype(np.float32)
    large = max_exact + (np.log(df / np.float32(max_exact)) / np.float32(math.log(MAX_DIST / max_exact))
                         * np.float32(N_BUCKETS - max_exact)).astype(np.int32)
    return np.where(d < max_exact, d, np.minimum(large, N_BUCKETS - 1)).astype(np.int32)


def _bias_from_buckets(idx, rb_ref, h):
    acc = jnp.zeros(idx.shape, F32)
    for b in range(N_BUCKETS):
        acc = jnp.where(idx == b, rb_ref[b, h], acc)
    return acc


def _lam(lv_ref, lam_init):
    lv = lv_ref[...]
    a = jnp.sum(lv[0:1] * lv[1:2], axis=-1, keepdims=True)
    b = jnp.sum(lv[2:3] * lv[3:4], axis=-1, keepdims=True)
    return jnp.exp(a) - jnp.exp(b) + lam_init


def _silu(x):
    return x * jax.nn.sigmoid(x)


def _weight_form(w):
    if isinstance(w, tuple):
        arr, l = w
        return arr, True, l, arr.shape[1], arr.shape[2]
    return w, False, None, w.shape[0], w.shape[1]


def _norm_mm_body(x_ref, g_ref, w_ref, o_ref, *rest, emit):
    h_ref = rest[-1]

    @pl.when(pl.program_id(1) == 0)
    def _():
        x = x_ref[...]
        ms = jnp.mean(x * x, axis=-1, keepdims=True)
        h_ref[...] = ((x * lax.rsqrt(ms + EPS)) * g_ref[...]).astype(BF16)

    w = w_ref[...]
    if emit:
        w = w.astype(BF16)
        rest[0][...] = w
    o_ref[...] = jnp.dot(h_ref[...], w, preferred_element_type=F32)


def norm_matmul(x, g, w):
    n, d = x.shape
    w_arr, emit, l, _, c = _weight_form(w)
    tm = _pick(n, (1024, 512, 256, 128))
    tn = _pick(c, (1024, 512, 256, 128) if emit else (2048, 1024, 512, 256, 128))
    assert not emit or n == tm
    w_spec = (pl.BlockSpec((None, d, tn), lambda i, j: (l, 0, j)) if emit
              else pl.BlockSpec((d, tn), lambda i, j: (0, j)))
    out_specs = [pl.BlockSpec((tm, tn), lambda i, j: (i, j))]
    out_shape = [jax.ShapeDtypeStruct((n, c), F32)]
    if emit:
        out_specs.append(pl.BlockSpec((d, tn), lambda i, j: (0, j)))
        out_shape.append(jax.ShapeDtypeStruct((d, c), BF16))
    outs = pl.pallas_call(
        functools.partial(_norm_mm_body, emit=emit),
        grid=(n // tm, c // tn),
        in_specs=[pl.BlockSpec((tm, d), lambda i, j: (i, 0)),
                  pl.BlockSpec((1, d), lambda i, j: (0, 0)),
                  w_spec],
        out_specs=out_specs,
        out_shape=out_shape,
        scratch_shapes=[pltpu.VMEM((tm, d), BF16)],
        compiler_params=_cparams(("parallel", "arbitrary")),
        name="norm_matmul",
    )(x, g.reshape(1, d), w_arr)
    return tuple(outs) if emit else outs[0]


def _ffn_up_body(x_ref, g_ref, wg_ref, wu_ref, cw_ref, cb_ref, st_ref, y_ref, so_ref, h_ref, gbuf_ref, carry_ref,
                 *, tiles_per_seq):
    i = pl.program_id(0)
    j = pl.program_id(1)
    tm = x_ref.shape[0]
    lo = SUBLANES - (K_FFN - 1)

    @pl.when(j == 0)
    def _():
        x = x_ref[...]
        ms = jnp.mean(x * x, axis=-1, keepdims=True)
        h_ref[...] = ((x * lax.rsqrt(ms + EPS)) * g_ref[...]).astype(BF16)

    h = h_ref[...]
    first = (i % tiles_per_seq) == 0

    @pl.when(first)
    def _():
        gbuf_ref[0:lo, :] = jnp.zeros((lo, gbuf_ref.shape[1]), F32)
        gbuf_ref[lo:SUBLANES, :] = st_ref[0]

    @pl.when(jnp.logical_not(first))
    def _():
        gbuf_ref[0:SUBLANES, :] = carry_ref[j]

    gbuf_ref[SUBLANES:SUBLANES + tm, :] = jnp.dot(h, wg_ref[...], preferred_element_type=F32)
    acc = jnp.zeros((tm, gbuf_ref.shape[1]), F32)
    for k in range(K_FFN):
        acc = acc + gbuf_ref[lo + k:lo + k + tm, :] * cw_ref[k:k + 1, :]
    up = jnp.dot(h, wu_ref[...], preferred_element_type=F32)
    y_ref[...] = (_silu(acc + cb_ref[...]) * up).astype(BF16)
    carry_ref[j] = gbuf_ref[tm:tm + SUBLANES, :]
    so_ref[0] = gbuf_ref[lo + tm:SUBLANES + tm, :]


def ffn_up_prompt(x, g, w, state, conv_w, conv_b, batch, seq):
    n, d = x.shape
    f = w.shape[1] // 2
    tm = _pick(seq, (1024, 512, 256, 128))
    tn = _pick(f, (512, 256, 128))
    nj = f // tn
    tps = seq // tm
    act, tails = pl.pallas_call(
        functools.partial(_ffn_up_body, tiles_per_seq=tps),
        grid=(n // tm, nj),
        in_specs=[pl.BlockSpec((tm, d), lambda i, j: (i, 0)),
                  pl.BlockSpec((1, d), lambda i, j: (0, 0)),
                  pl.BlockSpec((d, tn), lambda i, j: (0, j)),
                  pl.BlockSpec((d, tn), lambda i, j: (0, nj + j)),
                  pl.BlockSpec((K_FFN, tn), lambda i, j: (0, j)),
                  pl.BlockSpec((1, tn), lambda i, j: (0, j)),
                  pl.BlockSpec((1, K_FFN - 1, tn), lambda i, j: (i // tps, 0, j))],
        out_specs=[pl.BlockSpec((tm, tn), lambda i, j: (i, j)),
                   pl.BlockSpec((1, K_FFN - 1, tn), lambda i, j: (i, 0, j))],
        out_shape=[jax.ShapeDtypeStruct((n, f), BF16),
                   jax.ShapeDtypeStruct((n // tm, K_FFN - 1, f), F32)],
        scratch_shapes=[pltpu.VMEM((tm, d), BF16),
                        pltpu.VMEM((SUBLANES + tm, tn), F32),
                        pltpu.VMEM((nj, SUBLANES, tn), F32)],
        compiler_params=_cparams(("arbitrary", "arbitrary")),
        name="ffn_up_prompt",
    )(x, g.reshape(1, d), w, w, conv_w, conv_b.reshape(1, f), state)
    return act, tails[tps - 1::tps]


def _mm_res_body(a_ref, w_ref, r_ref, o_ref, *rest, nk, emit):
    acc_ref = rest[-1]
    k = pl.program_id(2)
    w = w_ref[...]
    if emit:
        w = w.astype(BF16)
        rest[0][...] = w
    part = jnp.dot(a_ref[...], w, preferred_element_type=F32)
    if nk == 1:
        o_ref[...] = r_ref[...] + part
    else:
        @pl.when(k == 0)
        def _():
            acc_ref[...] = part

        @pl.when(jnp.logical_and(k > 0, k < nk - 1))
        def _():
            acc_ref[...] += part

        @pl.when(k == nk - 1)
        def _():
            o_ref[...] = r_ref[...] + (acc_ref[...] + part)


def matmul_residual(a, w, res):
    n = a.shape[0]
    w_arr, emit, l, kd, c = _weight_form(w)
    tm = _pick(n, (1024, 512, 256, 128))
    tn = _pick(c, (1024, 512, 256, 128))
    tk = kd if kd <= 2048 else _pick(kd, (2816, 2048, 1024, 512))
    nk = kd // tk
    assert not emit or n == tm
    w_spec = (pl.BlockSpec((None, tk, tn), lambda i, j, k: (l, k, j)) if emit
              else pl.BlockSpec((tk, tn), lambda i, j, k: (k, j)))
    out_specs = [pl.BlockSpec((tm, tn), lambda i, j, k: (i, j))]
    out_shape = [jax.ShapeDtypeStruct((n, c), F32)]
    if emit:
        out_specs.append(pl.BlockSpec((tk, tn), lambda i, j, k: (k, j)))
        out_shape.append(jax.ShapeDtypeStruct((kd, c), BF16))
    outs = pl.pallas_call(
        functools.partial(_mm_res_body, nk=nk, emit=emit),
        grid=(n // tm, c // tn, nk),
        in_specs=[pl.BlockSpec((tm, tk), lambda i, j, k: (i, k)),
                  w_spec,
                  pl.BlockSpec((tm, tn), lambda i, j, k: (i, j))],
        out_specs=out_specs,
        out_shape=out_shape,
        scratch_shapes=[pltpu.VMEM((tm, tn), F32)],
        compiler_params=_cparams(("parallel", "parallel", "arbitrary")),
        name="matmul_residual",
    )(a, w_arr, res)
    return tuple(outs) if emit else outs[0]


def _group_rms(x, gain, gm):
    s = x * x
    hi = s.astype(BF16)
    lo = (s - hi.astype(F32)).astype(BF16)
    parts = []
    for c in range(x.shape[1] // LANES):
        sl = slice(c * LANES, (c + 1) * LANES)
        parts.append(jnp.dot(hi[:, sl], gm, preferred_element_type=F32)
                     + jnp.dot(lo[:, sl], gm, preferred_element_type=F32))
    ms = jnp.concatenate(parts, axis=-1) * (1.0 / DK_A)
    return (x * lax.rsqrt(ms + EPS)) * gain


_LOG2E = math.log2(math.e)
_Q_SCALE = DK_A ** -0.5 * _LOG2E
_DVX = DV_A + 16


def _attn_prep_sample_body(q_ref, k_ref, v_ref, qg_ref, kg_ref, gm_ref, qo_ref, kf_ref, vf_ref):
    gm = gm_ref[...]
    qo_ref[...] = _group_rms(q_ref[...], qg_ref[...], gm) * _Q_SCALE
    kf_ref[...] = _group_rms(k_ref[...], kg_ref[...], gm)
    vf_ref[...] = v_ref[...]


def _attn_prep_prompt_body(q_ref, k_ref, v_ref, qg_ref, kg_ref, gm_ref, qt_ref, kf_ref, kb_ref, vf_ref, vt_ref):
    gm = gm_ref[...]
    qn = _group_rms(q_ref[...], qg_ref[...], gm) * _Q_SCALE
    kn = _group_rms(k_ref[...], kg_ref[...], gm)
    kf_ref[...] = kn
    kb_ref[...] = kn.astype(BF16)
    v = v_ref[...]
    vf_ref[...] = v
    for h in range(H_A):
        hs = slice(h * LANES, (h + 1) * LANES)
        qt_ref[0, h, 0] = qn[:, hs].T.astype(BF16)
        vt_ref[0, h, 0, 0:DV_A, :] = v[:, hs].T.astype(BF16)
        vt_ref[0, h, 0, DV_A:_DVX, :] = jnp.ones((_DVX - DV_A, v.shape[0]), BF16)


def _attn_prep_consts(q_gain, k_gain):
    w = H_A * 2 * DK_A
    gm = np.kron(np.eye(LANES // DK_A, dtype=np.float32), np.ones((DK_A, DK_A), np.float32))
    return (jnp.tile(q_gain, w // DK_A).reshape(1, w), jnp.tile(k_gain, w // DK_A).reshape(1, w),
            jnp.asarray(gm, BF16))


def attn_prep_sample(proj, q_gain, k_gain):
    n = proj.shape[0]
    w = H_A * 2 * DK_A
    tm = _pick(n, (512, 256, 128))
    col = lambda c: pl.BlockSpec((tm, w), lambda i, c=c: (i, c))
    cst = lambda shape: pl.BlockSpec(shape, lambda i: (0, 0))
    out = lambda: pl.BlockSpec((tm, w), lambda i: (i, 0))
    return pl.pallas_call(
        _attn_prep_sample_body,
        grid=(n // tm,),
        in_specs=[col(0), col(1), col(2), cst((1, w)), cst((1, w)), cst((LANES, LANES))],
        out_specs=[out(), out(), out()],
        out_shape=[jax.ShapeDtypeStruct((n, w), F32)] * 3,
        compiler_params=_cparams(("parallel",)),
        name="attn_prep_sample",
    )(proj, proj, proj, *_attn_prep_consts(q_gain, k_gain))


def attn_prep_prompt(proj, q_gain, k_gain, batch, seq, t):
    n = batch * seq
    w = H_A * 2 * DK_A
    nt = seq // t
    col = lambda c: pl.BlockSpec((t, w), lambda b, i, c=c: (b * nt + i, c))
    cst = lambda shape: pl.BlockSpec(shape, lambda b, i: (0, 0))
    out = lambda: pl.BlockSpec((t, w), lambda b, i: (b * nt + i, 0))
    tr = lambda rows=LANES: pl.BlockSpec((1, H_A, 1, rows, t), lambda b, i: (b, 0, i, 0, 0))
    tr_shape = lambda rows=LANES: jax.ShapeDtypeStruct((batch, H_A, nt, rows, t), BF16)
    return pl.pallas_call(
        _attn_prep_prompt_body,
        grid=(batch, nt),
        in_specs=[col(0), col(1), col(2), cst((1, w)), cst((1, w)), cst((LANES, LANES))],
        out_specs=[tr(), out(), out(), out(), tr(_DVX)],
        out_shape=[tr_shape(), jax.ShapeDtypeStruct((n, w), F32), jax.ShapeDtypeStruct((n, w), BF16),
                   jax.ShapeDtypeStruct((n, w), F32), tr_shape(_DVX)],
        compiler_params=_cparams(("parallel", "parallel")),
        name="attn_prep_prompt",
    )(proj, proj, proj, *_attn_prep_consts(q_gain, k_gain))


def _prompt_attn_body(rb_ref, ib_ref, lv_ref, sg_ref, qt_ref, k_ref, vt_ref, o_ref,
                      bias_ref, m_ref, acc_ref, *, t, lam_init, n_hg):
    hg = pl.program_id(1)
    qi = pl.program_id(2)
    heads = range(_ATTN_HEADS)
    build = qi == 0
    if n_hg == 1:
        build = jnp.logical_and(build, pl.program_id(0) == 0)

    @pl.when(build)
    def _():
        for g in heads:
            h = hg * _ATTN_HEADS + g
            far_bias = rb_ref[N_BUCKETS - 1, h]
            for o in range(2):
                idx = ib_ref[o]
                bias_ref[g, o] = jnp.where(idx >= 0, (_bias_from_buckets(idx, rb_ref, h) - far_bias) * _LOG2E, NEG)

    q2t = []
    for g in heads:
        qt = qt_ref[0, g, 0]
        row = lax.broadcasted_iota(jnp.int32, qt.shape, 0)
        zero = jnp.zeros_like(qt)
        q2t.append(jnp.concatenate([jnp.where(row < DK_A, qt, zero), jnp.where(row >= DK_A, qt, zero)], axis=1))

    m_ref[...] = jnp.full(m_ref.shape, NEG, F32)
    acc_ref[...] = jnp.zeros(acc_ref.shape, F32)

    def tile(kj, o):
        off = pl.multiple_of(kj * t, t)
        s = [jnp.dot(k_ref[0, pl.ds(off, t), g * LANES:(g + 1) * LANES], q2t[g], preferred_element_type=F32)
             for g in heads]
        if o is not None:
            s = [s[g] + bias_ref[g, o] for g in heads]
        m_old = [m_ref[g] for g in heads]
        m_new = [jnp.maximum(m_old[g], jnp.max(s[g], axis=0, keepdims=True)) for g in heads]
        alpha = [jnp.exp2(m_old[g] - m_new[g]) for g in heads]
        p = [jnp.exp2(s[g] - m_new[g]).astype(BF16) for g in heads]
        for g in heads:
            m_ref[g] = m_new[g]
        pv = [jnp.dot(vt_ref[0, g, kj], p[g], preferred_element_type=F32) for g in heads]
        for g in heads:
            acc_ref[g] = alpha[g] * acc_ref[g] + pv[g]

    def far(kj, carry):
        tile(kj, None)
        return carry

    lax.fori_loop(0, jnp.maximum(qi - 1, 0), far, 0)

    @pl.when(qi >= 1)
    def _():
        tile(qi - 1, 1)

    tile(qi, 0)

    lam = _lam(lv_ref, lam_init)
    for g in heads:
        o2t = acc_ref[g, 0:DV_A, :] / acc_ref[g, DV_A:DV_A + 1, :]
        o = (o2t[:, 0:t] - lam * o2t[:, t:2 * t]).T
        ms = jnp.mean(o * o, axis=-1, keepdims=True)
        y = ((o * lax.rsqrt(ms + EPS)) * sg_ref[...]) * (1.0 - lam_init)
        o_ref[0, :, g * LANES:(g + 1) * LANES] = y.astype(BF16)


_ATTN_TILE = 256
_ATTN_HEADS = 8


def prompt_attn(qt, kb, vt, rel_bias, lam_vec, subln_g, lam_init, batch, seq, t):
    w = H_A * DV_A
    assert t >= MAX_DIST and seq % t == 0
    nq = seq // t
    hgs = _ATTN_HEADS
    key = np.arange(t)[:, None]
    qry = (np.arange(2 * t) % t)[None, :]
    d0 = qry - key
    ib = jnp.asarray(np.stack([np.where(d0 >= 0, _t5_bucket_np(d0), -1), _t5_bucket_np(d0 + t)]), jnp.int32)
    assert int(_t5_bucket_np(np.array([t + 1]))[0]) == N_BUCKETS - 1
    out = pl.pallas_call(
        functools.partial(_prompt_attn_body, t=t, lam_init=lam_init, n_hg=H_A // hgs),
        grid=(batch, H_A // hgs, nq),
        in_specs=[pl.BlockSpec(memory_space=pltpu.SMEM),
                  pl.BlockSpec((2, t, 2 * t), lambda b, h, i: (0, 0, 0)),
                  pl.BlockSpec((4, DK_A), lambda b, h, i: (0, 0)),
                  pl.BlockSpec((1, DV_A), lambda b, h, i: (0, 0)),
                  pl.BlockSpec((1, hgs, 1, LANES, t), lambda b, h, i: (b, h, i, 0, 0)),
                  pl.BlockSpec((1, seq, hgs * LANES), lambda b, h, i: (b, 0, h)),
                  pl.BlockSpec((1, hgs, nq, _DVX, t), lambda b, h, i: (b, h, 0, 0, 0))],
        out_specs=pl.BlockSpec((1, t, hgs * LANES), lambda b, h, i: (b, i, h)),
        out_shape=jax.ShapeDtypeStruct((batch, seq, w), BF16),
        scratch_shapes=[pltpu.VMEM((hgs, 2, t, 2 * t), F32),
                        pltpu.VMEM((hgs, 1, 2 * t), F32),
                        pltpu.VMEM((hgs, _DVX, 2 * t), F32)],
        compiler_params=_cparams(("arbitrary", "arbitrary", "arbitrary")),
        name="prompt_attn",
    )(rel_bias, ib, lam_vec, subln_g.reshape(1, DV_A), qt, kb.reshape(batch, seq, w), vt)
    return out.reshape(batch * seq, w)


_RH = SUBLANES


_KPAD = LANES // H_A


def _sample_attn_body(pt_ref, rb_ref, ibp_ref, ibn_ref, lv_ref, sg_ref, q_ref, kn_ref, vn_ref, *rest,
                      pps, nsteps, lam_init):
    k_refs = rest[0:pps]
    v_refs = rest[pps:2 * pps]
    o_ref = rest[2 * pps]
    biasp_ref, biasn_ref, m_ref, l_ref, acc_ref = rest[2 * pps + 1:]
    b = pl.program_id(0)
    s_id = pl.program_id(1)

    t_new = o_ref.shape[1]

    @pl.when(jnp.logical_and(b == 0, s_id == 0))
    def _():
        for h in range(H_A):
            sl = slice(h * _RH, (h + 1) * _RH)
            far_bias = rb_ref[N_BUCKETS - 1, h]
            for o in range(2):
                idx = ibp_ref[o, sl, :]
                biasp_ref[o, sl, :] = jnp.where(idx >= 0, (_bias_from_buckets(idx, rb_ref, h) - far_bias) * _LOG2E,
                                                NEG)
            idx = ibn_ref[sl, :]
            biasn_ref[sl, :] = jnp.where(idx >= 0, (_bias_from_buckets(idx, rb_ref, h) - far_bias) * _LOG2E, NEG)

    @pl.when(s_id == 0)
    def _():
        m_ref[...] = jnp.full(m_ref.shape, NEG, F32)
        l_ref[...] = jnp.zeros(l_ref.shape, F32)
        acc_ref[...] = jnp.zeros(acc_ref.shape, F32)

    q = q_ref[0]
    row = lax.broadcasted_iota(jnp.int32, (_RH, LANES), 0)
    lane = lax.broadcasted_iota(jnp.int32, (_RH, LANES), 1)
    keep = (row < t_new) == (lane < DK_A)
    q2 = jnp.concatenate([jnp.where(keep, q[:, h * LANES:(h + 1) * LANES], 0.0) for h in range(H_A)],
                         axis=0).astype(BF16)

    def attend(kt, vt, bias):
        s = lax.dot_general(q2, kt, _NT, preferred_element_type=F32) + bias
        m_old = m_ref[...]
        m_new = jnp.maximum(m_old, jnp.max(s, axis=-1, keepdims=True))
        alpha = jnp.exp2(m_old - m_new)
        p = jnp.exp2(s - m_new)
        l_ref[...] = alpha * l_ref[...] + jnp.sum(p, axis=-1, keepdims=True)
        acc_ref[...] = alpha * acc_ref[...] + jnp.dot(p.astype(BF16), vt, preferred_element_type=F32)
        m_ref[...] = m_new

    kt = jnp.concatenate([r[0, 0].astype(BF16) for r in k_refs], axis=0)
    vt = jnp.concatenate([r[0, 0].astype(BF16) for r in v_refs], axis=0)
    last = s_id == nsteps - 1
    bias = jnp.concatenate([biasp_ref[0]] * (pps - 1) + [biasp_ref[jnp.where(last, 1, 0)]], axis=1)
    attend(kt, vt, bias)

    @pl.when(last)
    def _():
        attend(kn_ref[0].astype(BF16), vn_ref[0].astype(BF16), biasn_ref[...])
        o2 = acc_ref[...] / l_ref[...]
        lam = _lam(lv_ref, lam_init)
        for h in range(H_A):
            x = o2[h * _RH:(h + 1) * _RH]
            o = x - lam * pltpu.roll(x, t_new, 0)
            ms = jnp.mean(o * o, axis=-1, keepdims=True)
            y = ((o * lax.rsqrt(ms + EPS)) * sg_ref[...]) * (1.0 - lam_init)
            o_ref[0, :, h * LANES:(h + 1) * LANES] = y[0:t_new]


def sample_attn(qn, kf, vf, cache_k, cache_v, layer, page_table, rel_bias, lam_vec, subln_g, lam_init, db, t_new):
    w = H_A * DV_A
    depth, n_pool, page = cache_k.shape[:3]
    n_pages = page_table.shape[1]
    assert 2 * t_new == _RH and t_new <= _KPAD and page >= MAX_DIST
    pps = _pick(n_pages, (16, 8, 4, 2, 1))
    nsteps = n_pages // pps
    rows = H_A * _RH
    kw = page * H_A
    rh = (np.arange(rows) // _RH)[:, None]
    rt = (np.arange(rows) % t_new)[:, None]
    ck_key, ck_head = (np.arange(kw) // H_A)[None, :], (np.arange(kw) % H_A)[None, :]
    ib_far = np.where(rh == ck_head, N_BUCKETS - 1, -1)
    ib_last = np.where(rh == ck_head, _t5_bucket_np(page + rt - ck_key), -1)
    nk_key, nk_head = (np.arange(_KPAD * H_A) // H_A)[None, :], (np.arange(_KPAD * H_A) % H_A)[None, :]
    d_new = rt - nk_key
    ib_new = np.where((rh == nk_head) & (d_new >= 0) & (nk_key < t_new), _t5_bucket_np(d_new), -1)
    ibp = jnp.asarray(np.stack([ib_far, ib_last]), jnp.int32)
    ibn = jnp.asarray(ib_new, jnp.int32)
    assert int(_t5_bucket_np(np.array([page + 1]))[0]) == N_BUCKETS - 1

    q3 = qn.reshape(db, t_new, w)
    q_pad = jnp.concatenate([q3, q3], axis=1)

    def new_rows(x):
        return jnp.pad(x.reshape(db, t_new * H_A, LANES), ((0, 0), (0, (_KPAD - t_new) * H_A), (0, 0)))

    ck = cache_k.reshape(depth, n_pool, kw, LANES)
    cv = cache_v.reshape(depth, n_pool, kw, LANES)
    new_spec = pl.BlockSpec((1, _KPAD * H_A, LANES), lambda b, s, pt: (b, 0, 0))

    def page_spec(i):
        return pl.BlockSpec((1, 1, kw, LANES), lambda b, s, pt, i=i: (layer, pt[b, s * pps + i], 0, 0))

    grid_spec = pltpu.PrefetchScalarGridSpec(
        num_scalar_prefetch=1,
        grid=(db, nsteps),
        in_specs=[pl.BlockSpec(memory_space=pltpu.SMEM),
                  pl.BlockSpec((2, rows, kw), lambda b, s, pt: (0, 0, 0)),
                  pl.BlockSpec((rows, _KPAD * H_A), lambda b, s, pt: (0, 0)),
                  pl.BlockSpec((4, DK_A), lambda b, s, pt: (0, 0)),
                  pl.BlockSpec((1, DV_A), lambda b, s, pt: (0, 0)),
                  pl.BlockSpec((1, _RH, w), lambda b, s, pt: (b, 0, 0)),
                  new_spec, new_spec]
                 + [page_spec(i) for i in range(pps)] + [page_spec(i) for i in range(pps)],
        out_specs=pl.BlockSpec((1, t_new, w), lambda b, s, pt: (b, 0, 0)),
        scratch_shapes=[pltpu.VMEM((2, rows, kw), F32),
                        pltpu.VMEM((rows, _KPAD * H_A), F32),
                        pltpu.VMEM((rows, 1), F32),
                        pltpu.VMEM((rows, 1), F32),
                        pltpu.VMEM((rows, DV_A), F32)],
    )
    out = pl.pallas_call(
        functools.partial(_sample_attn_body, pps=pps, nsteps=nsteps, lam_init=lam_init),
        grid_spec=grid_spec,
        out_shape=jax.ShapeDtypeStruct((db, t_new, w), F32),
        compiler_params=_cparams(("arbitrary", "arbitrary")),
        name="sample_attn",
    )(page_table, rel_bias, ibp, ibn, lam_vec, subln_g.reshape(1, DV_A), q_pad, new_rows(kf), new_rows(vf),
      *([ck] * pps), *([cv] * pps))
    return out.reshape(db * t_new, w).astype(BF16)


def _log_gamma(h):
    return float(np.log1p(-np.exp2(np.float32(-5.0 - h), dtype=np.float32), dtype=np.float32))


def _swap_halves(x):
    n = x.shape[1]
    half = DK_B // 2
    lane = lax.broadcasted_iota(jnp.int32, x.shape, 1)
    return jnp.where((lane % DK_B) < half, pltpu.roll(x, n - half, 1), pltpu.roll(x, half, 1))


def _rotary(x, cos, sin):
    return x * cos + _swap_halves(x) * sin


def _ret_tile(q, k, v, c):
    r = q.shape[0]
    shift = int(math.log2(c))
    assert 1 << shift == c
    ri = lax.broadcasted_iota(jnp.int32, (r, r), 0)
    ci = lax.broadcasted_iota(jnp.int32, (r, r), 1)
    same = (ri >> shift) == (ci >> shift)
    diff = ((ri & (c - 1)) - (ci & (c - 1))).astype(F32)
    valid = jnp.logical_and(same, diff >= 0.0)
    ti = (lax.broadcasted_iota(jnp.int32, (r, 1), 0) & (c - 1)).astype(F32)
    lane = lax.broadcasted_iota(jnp.int32, (r, LANES), 1)
    first = lane < DK_B
    o_inner, q_m, dec_q, k_dec = [], [], [], []
    for p in range(H_B // 2):
        ps = slice(p * LANES, (p + 1) * LANES)
        qp, kp = q[:, ps], k[:, ps]
        kpb = kp.astype(BF16)
        lg0, lg1 = _log_gamma(2 * p), _log_gamma(2 * p + 1)
        k_dec.append((kp * jnp.where(first, jnp.exp((c - 1.0 - ti) * lg0), jnp.exp((c - 1.0 - ti) * lg1))).astype(BF16))
        for hh in range(2):
            h = 2 * p + hh
            lg = lg1 if hh else lg0
            qm = jnp.where(first if hh == 0 else jnp.logical_not(first), qp, 0.0).astype(BF16)
            dmat = jnp.where(valid, jnp.exp(jnp.maximum(diff, 0.0) * lg), 0.0)
            inner = lax.dot_general(qm, kpb, _NT, preferred_element_type=F32) * dmat
            vh = v[:, h * DV_B:(h + 1) * DV_B].astype(BF16)
            o_inner.append(jnp.dot(inner.astype(BF16), vh, preferred_element_type=F32))
            q_m.append(qm)
            dec_q.append(jnp.exp((ti + 1.0) * lg))
    return o_inner, q_m, dec_q, k_dec


def _state_update(s_pair, kd, v, p, c):
    row = lax.broadcasted_iota(jnp.int32, (LANES, DV_B), 0)
    top = row < DK_B
    u0 = lax.dot_general(kd, v[:, (2 * p) * DV_B:(2 * p + 1) * DV_B].astype(BF16), _TN, preferred_element_type=F32)
    u1 = lax.dot_general(kd, v[:, (2 * p + 1) * DV_B:(2 * p + 2) * DV_B].astype(BF16), _TN, preferred_element_type=F32)
    gc = jnp.where(top, math.exp(c * _log_gamma(2 * p)), math.exp(c * _log_gamma(2 * p + 1)))
    return s_pair * gc + jnp.where(top, u0, u1)


def _ret_epilogue(o, g, ng):
    ms = jnp.mean(o * o, axis=-1, keepdims=True)
    return ((o * lax.rsqrt(ms + EPS)) * ng) * _silu(g)


def _ret_prompt_body(q_ref, k_ref, v_ref, g_ref, cos_ref, sin_ref, ng_ref, y_ref, so_ref, s_ref, *, c):
    ci = pl.program_id(1)

    @pl.when(ci == 0)
    def _():
        s_ref[...] = jnp.zeros(s_ref.shape, F32)

    cos, sin = cos_ref[...], sin_ref[...]
    q = _rotary(q_ref[...], cos, sin)
    k = _rotary(k_ref[...], cos, sin) * (DK_B ** -0.5)
    v = v_ref[...]
    o_inner, q_m, dec_q, k_dec = _ret_tile(q, k, v, c)
    ng = ng_ref[...]
    for p in range(H_B // 2):
        s_pair = s_ref[p]
        sb = s_pair.astype(BF16)
        for hh in range(2):
            h = 2 * p + hh
            o = o_inner[h] + jnp.dot(q_m[h], sb, preferred_element_type=F32) * dec_q[h]
            hs = slice(h * DV_B, (h + 1) * DV_B)
            y_ref[:, hs] = _ret_epilogue(o, g_ref[:, hs], ng).astype(BF16)
        s_ref[p] = _state_update(s_pair, k_dec[p], v, p, c)

    @pl.when(ci == pl.num_programs(1) - 1)
    def _():
        so_ref[0] = s_ref[...]


def retention_prompt(proj, cos, sin, ret_norm_g, batch, seq, col0):
    n = batch * seq
    c = 128 if seq % 128 == 0 else seq
    nc = seq // c
    wq = H_B * DK_B
    wv = H_B * DV_B
    assert col0 % wv == 0
    qc, vc = col0 // wq, col0 // wv
    row = lambda b, i: b * nc + i
    y, st = pl.pallas_call(
        functools.partial(_ret_prompt_body, c=c),
        grid=(batch, nc),
        in_specs=[pl.BlockSpec((c, wq), lambda b, i: (row(b, i), qc)),
                  pl.BlockSpec((c, wq), lambda b, i: (row(b, i), qc + 1)),
                  pl.BlockSpec((c, wv), lambda b, i: (row(b, i), vc + 1)),
                  pl.BlockSpec((c, wv), lambda b, i: (row(b, i), vc + 2)),
                  pl.BlockSpec((c, wq), lambda b, i: (i, 0)),
                  pl.BlockSpec((c, wq), lambda b, i: (i, 0)),
                  pl.BlockSpec((1, DV_B), lambda b, i: (0, 0))],
        out_specs=[pl.BlockSpec((c, wv), lambda b, i: (row(b, i), 0)),
                   pl.BlockSpec((1, H_B // 2, LANES, DV_B), lambda b, i: (b, 0, 0, 0))],
        out_shape=[jax.ShapeDtypeStruct((n, wv), BF16),
                   jax.ShapeDtypeStruct((batch, H_B // 2, LANES, DV_B), F32)],
        scratch_shapes=[pltpu.VMEM((H_B // 2, LANES, DV_B), F32)],
        compiler_params=_cparams(("arbitrary", "arbitrary")),
        name="retention_prompt",
    )(proj, proj, proj, proj, cos, sin, ret_norm_g.reshape(1, DV_B))
    return y, st.reshape(batch, H_B, DK_B, DV_B)


def _ret_sample_body(q_ref, k_ref, v_ref, g_ref, cos_ref, sin_ref, ng_ref, s0_ref, y_ref, so_ref,
                     o_ref, qm_ref, kd_ref, *, c):
    gi = pl.program_id(0)
    r = q_ref.shape[0]
    shift = int(math.log2(c))
    v = v_ref[...]

    @pl.when(gi == 0)
    def _():
        cos, sin = cos_ref[...], sin_ref[...]
        q = _rotary(q_ref[...], cos, sin)
        k = _rotary(k_ref[...], cos, sin) * (DK_B ** -0.5)
        o_inner, q_m, dec_q, k_dec = _ret_tile(q, k, v, c)
        for h in range(H_B):
            o_ref[:, h * DV_B:(h + 1) * DV_B] = o_inner[h]
            qm_ref[h] = q_m[h]
        for p in range(H_B // 2):
            kd_ref[p] = k_dec[p]

    rsel = (lax.broadcasted_iota(jnp.int32, (r, LANES), 0) >> shift) == gi
    ti = (lax.broadcasted_iota(jnp.int32, (r, 1), 0) & (c - 1)).astype(F32)
    zero = jnp.zeros((r, LANES), BF16)
    for p in range(H_B // 2):
        s_pair = s0_ref[0, p]
        sb = s_pair.astype(BF16)
        for hh in range(2):
            h = 2 * p + hh
            qm = jnp.where(rsel, qm_ref[h], zero)
            hs = slice(h * DV_B, (h + 1) * DV_B)
            o_ref[:, hs] += jnp.dot(qm, sb, preferred_element_type=F32) * jnp.exp((ti + 1.0) * _log_gamma(h))
        kd = jnp.where(rsel, kd_ref[p], zero)
        so_ref[0, p] = _state_update(s_pair, kd, v, p, c)

    @pl.when(gi == pl.num_programs(0) - 1)
    def _():
        ng = ng_ref[...]
        for h in range(H_B):
            hs = slice(h * DV_B, (h + 1) * DV_B)
            y_ref[:, hs] = _ret_epilogue(o_ref[:, hs], g_ref[:, hs], ng).astype(BF16)


def retention_sample(proj, cos, sin, ret_norm_g, state0, db, t_new, col0):
    n = db * t_new
    wq = H_B * DK_B
    wv = H_B * DV_B
    qc, vc = col0 // wq, col0 // wv
    s0 = state0.reshape(db, H_B // 2, LANES, DV_B)
    st_spec = pl.BlockSpec((1, H_B // 2, LANES, DV_B), lambda g: (g, 0, 0, 0))
    y, st = pl.pallas_call(
        functools.partial(_ret_sample_body, c=t_new),
        grid=(db,),
        in_specs=[pl.BlockSpec((n, wq), lambda g: (0, qc)),
                  pl.BlockSpec((n, wq), lambda g: (0, qc + 1)),
                  pl.BlockSpec((n, wv), lambda g: (0, vc + 1)),
                  pl.BlockSpec((n, wv), lambda g: (0, vc + 2)),
                  pl.BlockSpec((n, wq), lambda g: (0, 0)),
                  pl.BlockSpec((n, wq), lambda g: (0, 0)),
                  pl.BlockSpec((1, DV_B), lambda g: (0, 0)),
                  st_spec],
        out_specs=[pl.BlockSpec((n, wv), lambda g: (0, 0)), st_spec],
        out_shape=[jax.ShapeDtypeStruct((n, wv), BF16),
                   jax.ShapeDtypeStruct((db, H_B // 2, LANES, DV_B), F32)],
        scratch_shapes=[pltpu.VMEM((n, wv), F32),
                        pltpu.VMEM((H_B, n, LANES), BF16),
                        pltpu.VMEM((H_B // 2, n, LANES), BF16)],
        compiler_params=_cparams(("arbitrary",)),
        name="retention_sample",
    )(proj, proj, proj, proj, cos, sin, ret_norm_g.reshape(1, DV_B), s0)
    return y, st.reshape(db, H_B, DK_B, DV_B)


def _rotary_tables(pos):
    half = DK_B // 2
    inv = 1.0 / (10000.0 ** (jnp.arange(half, dtype=F32) / half))
    ang = pos.astype(F32)[:, None] * inv[None, :]
    cos, sin = jnp.cos(ang), jnp.sin(ang)
    cos_t = jnp.tile(jnp.concatenate([cos, cos], axis=-1), (1, H_B))
    sin_t = jnp.tile(jnp.concatenate([-sin, sin], axis=-1), (1, H_B))
    return cos_t, sin_t


_HALO = 32
_CONV_ROWS = 64


def _conv_body(a_ref, b_ref, st_ref, w_ref, cb_ref, lg_ref, lb_ref, y_ref, so_ref, full_ref, sh_ref, uc_ref, *, ts):
    ti = pl.program_id(1)
    lo = _HALO - (K_CONV - 1)

    @pl.when(ti == 0)
    def _():
        full_ref[0:lo, :] = jnp.zeros((lo, full_ref.shape[1]), F32)
        full_ref[lo:_HALO, :] = st_ref[0]

    full_ref[_HALO:_HALO + ts, :] = a_ref[0] * jax.nn.sigmoid(b_ref[0])
    nsh = sh_ref.shape[1]
    for s in range(1, SUBLANES):
        sh_ref[s - 1] = full_ref[s:s + nsh, :]
    rb = min(ts, _CONV_ROWS)
    for c in range(full_ref.shape[1] // LANES):
        cs = slice(c * LANES, (c + 1) * LANES)
        for r0 in range(0, ts, rb):
            acc = None
            for j in range(K_CONV):
                s = (lo + j) % SUBLANES
                a0 = lo + j - s + r0
                win = full_ref[a0:a0 + rb, cs] if s == 0 else sh_ref[s - 1, a0:a0 + rb, cs]
                term = win * w_ref[j:j + 1, cs]
                acc = term if acc is None else acc + term
            uc_ref[r0:r0 + rb, cs] = acc + cb_ref[:, cs]
    uc = uc_ref[...]
    mu = jnp.mean(uc, axis=-1, keepdims=True)
    dev = uc - mu
    var = jnp.mean(dev * dev, axis=-1, keepdims=True)
    y = (dev * lax.rsqrt(var + EPS)) * lg_ref[...] + lb_ref[...]
    y_ref[0] = _silu(y).astype(BF16)
    so_ref[0] = full_ref[lo + ts:_HALO + ts, :]
    if ts >= _HALO:
        full_ref[0:_HALO, :] = full_ref[ts:ts + _HALO, :]


def conv_module(proj3, state, conv_w, conv_b, ln_g, ln_b, col0):
    b, s, _ = proj3.shape
    c = conv_w.shape[1]
    ts = _pick(s, (256, 128)) if s >= 128 else s
    nt = s // ts
    assert nt == 1 or ts >= _HALO
    cc = col0 // c
    vec = lambda: pl.BlockSpec((1, c), lambda i, t: (0, 0))
    return pl.pallas_call(
        functools.partial(_conv_body, ts=ts),
        grid=(b, nt),
        in_specs=[pl.BlockSpec((1, ts, c), lambda i, t: (i, t, cc)),
                  pl.BlockSpec((1, ts, c), lambda i, t: (i, t, cc + 1)),
                  pl.BlockSpec((1, K_CONV - 1, c), lambda i, t: (i, 0, 0)),
                  pl.BlockSpec((K_CONV, c), lambda i, t: (0, 0)),
                  vec(), vec(), vec()],
        out_specs=[pl.BlockSpec((1, ts, c), lambda i, t: (i, t, 0)),
                   pl.BlockSpec((1, K_CONV - 1, c), lambda i, t: (i, 0, 0))],
        out_shape=[jax.ShapeDtypeStruct((b, s, c), BF16),
                   jax.ShapeDtypeStruct((b, K_CONV - 1, c), F32)],
        scratch_shapes=[pltpu.VMEM((_HALO + ts, c), F32),
                        pltpu.VMEM((SUBLANES - 1, _HALO + ts - SUBLANES, c), F32),
                        pltpu.VMEM((ts, c), F32)],
        compiler_params=_cparams(("arbitrary", "arbitrary")),
        name="conv_module",
    )(proj3, proj3, state, conv_w, conv_b.reshape(1, c), ln_g.reshape(1, c), ln_b.reshape(1, c))


def _merge_body(a_ref, b_ref, c_ref, wa_ref, wb_ref, wc_ref, ga_ref, gb_ref, gc_ref,
                ba_ref, bb_ref, bc_ref, o_ref, *copies, emit):
    ws = [wa_ref[...], wb_ref[...], wc_ref[...]]
    if emit:
        ws = [w.astype(BF16) for w in ws]
        for copy_ref, w in zip(copies, ws):
            copy_ref[...] = w
    ya = jnp.dot(a_ref[...], ws[0], preferred_element_type=F32)
    m = jax.nn.sigmoid(ga_ref[...] + ba_ref[...]) * ya
    yb = jnp.dot(b_ref[...], ws[1], preferred_element_type=F32)
    m = m + jax.nn.sigmoid(gb_ref[...] + bb_ref[...]) * yb
    yc = jnp.dot(c_ref[...], ws[2], preferred_element_type=F32)
    m = m + jax.nn.sigmoid(gc_ref[...] + bc_ref[...]) * yc
    o_ref[...] = m.astype(BF16)


def branch_merge(a, b, c, wa, wb, wc, proj, b_gate, col0):
    n, kd = a.shape
    (wa_arr, emit, l, _, d), (wb_arr, _, _, _, _), (wc_arr, _, _, _, _) = (_weight_form(w) for w in (wa, wb, wc))
    tm = _pick(n, (512, 256, 128))
    tn = _pick(d, (1024, 512, 256, 128))
    nj = d // tn
    g0 = col0 // tn
    bg = b_gate.reshape(1, 3 * d)
    assert not emit or n == tm
    x_spec = lambda: pl.BlockSpec((tm, kd), lambda j, i: (i, 0))
    w_spec = lambda: (pl.BlockSpec((None, kd, tn), lambda j, i: (l, 0, j)) if emit
                      else pl.BlockSpec((kd, tn), lambda j, i: (0, j)))
    g_spec = lambda k: pl.BlockSpec((tm, tn), lambda j, i, k=k: (i, g0 + k * nj + j))
    bias_spec = lambda k: pl.BlockSpec((1, tn), lambda j, i, k=k: (0, k * nj + j))
    out_specs = [pl.BlockSpec((tm, tn), lambda j, i: (i, j))]
    out_shape = [jax.ShapeDtypeStruct((n, d), BF16)]
    if emit:
        out_specs += [pl.BlockSpec((kd, tn), lambda j, i: (0, j))] * 3
        out_shape += [jax.ShapeDtypeStruct((kd, d), BF16)] * 3
    outs = pl.pallas_call(
        functools.partial(_merge_body, emit=emit),
        grid=(nj, n // tm),
        in_specs=[x_spec(), x_spec(), x_spec(), w_spec(), w_spec(), w_spec(),
                  g_spec(0), g_spec(1), g_spec(2), bias_spec(0), bias_spec(1), bias_spec(2)],
        out_specs=out_specs,
        out_shape=out_shape,
        compiler_params=_cparams(("parallel", "arbitrary")),
        name="branch_merge",
    )(a, b, c, wa_arr, wb_arr, wc_arr, proj, proj, proj, bg, bg, bg)
    return tuple(outs) if emit else outs[0]


def _ffn_act_body(g_ref, u_ref, st_ref, w_ref, cb_ref, y_ref, so_ref, full_ref, *, ts):
    ti = pl.program_id(2)
    lo = SUBLANES - (K_FFN - 1)

    @pl.when(ti == 0)
    def _():
        full_ref[0:lo, :] = jnp.zeros((lo, full_ref.shape[1]), F32)
        full_ref[lo:SUBLANES, :] = st_ref[0]

    full_ref[SUBLANES:SUBLANES + ts, :] = g_ref[0]
    acc = jnp.zeros((ts, full_ref.shape[1]), F32)
    for j in range(K_FFN):
        acc = acc + full_ref[lo + j:lo + j + ts, :] * w_ref[j:j + 1, :]
    y_ref[0] = (_silu(acc + cb_ref[...]) * u_ref[0]).astype(BF16)
    so_ref[0] = full_ref[lo + ts:SUBLANES + ts, :]
    if ts >= SUBLANES:
        full_ref[0:SUBLANES, :] = full_ref[ts:ts + SUBLANES, :]


def ffn_act(h3, state, conv_w, conv_b):
    b, s, f2 = h3.shape
    f = f2 // 2
    ts = _pick(s, (512, 256, 128)) if s >= 128 else s
    tc = _pick(f, (512, 256, 128)) if ts >= 128 else f
    nc = f // tc
    nt = s // ts
    assert nt == 1 or ts >= SUBLANES
    return pl.pallas_call(
        functools.partial(_ffn_act_body, ts=ts),
        grid=(b, nc, nt),
        in_specs=[pl.BlockSpec((1, ts, tc), lambda i, c, t: (i, t, c)),
                  pl.BlockSpec((1, ts, tc), lambda i, c, t: (i, t, nc + c)),
                  pl.BlockSpec((1, K_FFN - 1, tc), lambda i, c, t: (i, 0, c)),
                  pl.BlockSpec((K_FFN, tc), lambda i, c, t: (0, c)),
                  pl.BlockSpec((1, tc), lambda i, c, t: (0, c))],
        out_specs=[pl.BlockSpec((1, ts, tc), lambda i, c, t: (i, t, c)),
                   pl.BlockSpec((1, K_FFN - 1, tc), lambda i, c, t: (i, 0, c))],
        out_shape=[jax.ShapeDtypeStruct((b, s, f), BF16),
                   jax.ShapeDtypeStruct((b, K_FFN - 1, f), F32)],
        scratch_shapes=[pltpu.VMEM((SUBLANES + ts, tc), F32)],
        compiler_params=_cparams(("arbitrary", "arbitrary", "arbitrary")),
        name="ffn_act",
    )(h3, h3, state, conv_w, conv_b.reshape(1, f))


def _layer(x, batch, seq, pos, attn_fn, ret_fn, conv0, ffn0, lw, mw, fuse_ffn):
    n, d = x.shape
    a_cols = H_A * (4 * DK_A + DV_A)
    b_cols = H_B * (2 * DK_B + 2 * DV_B)
    c_conv = lw['conv_w'].shape[1]
    emit = isinstance(mw['w_in'], tuple)
    copies = {}

    def take(name, res):
        if not emit:
            return res
        copies[name] = res[1]
        return res[0]

    proj = take('w_in', norm_matmul(x, lw['norm1_g'], mw['w_in']))
    a_in, kf, vf = attn_fn(proj)
    cos, sin = _rotary_tables(pos)
    b_in, ret_new = ret_fn(proj, cos, sin, a_cols)
    c_in, conv_new = conv_module(proj.reshape(batch, seq, -1), conv0, lw['conv_w'], lw['conv_b'],
                                 lw['conv_ln_g'], lw['conv_ln_b'], a_cols + b_cols)
    c_in = c_in.reshape(n, c_conv)
    merged = branch_merge(a_in, b_in, c_in, mw['w_o_a'], mw['w_o_b'], mw['w_o_c'], proj, lw['b_gate'],
                          a_cols + b_cols + 2 * c_conv)
    if emit:
        m, copies['w_o_a'], copies['w_o_b'], copies['w_o_c'] = merged
    else:
        m = merged
    x = take('w_out', matmul_residual(m, mw['w_out'], x))
    if fuse_ffn:
        act, ffn_new = ffn_up_prompt(x, lw['norm2_g'], mw['w_ffn_in'], ffn0, lw['ffn_conv_w'], lw['ffn_conv_b'],
                                     batch, seq)
    else:
        h2 = take('w_ffn_in', norm_matmul(x, lw['norm2_g'], mw['w_ffn_in']))
        act, ffn_new = ffn_act(h2.reshape(batch, seq, -1), ffn0, lw['ffn_conv_w'], lw['ffn_conv_b'])
        act = act.reshape(n, -1)
    x = take('w_ffn_down', matmul_residual(act, mw['w_ffn_down'], x))
    states = (kf.reshape(batch, seq, H_A, 2 * DK_A), vf.reshape(batch, seq, H_A, DV_A), ret_new, conv_new, ffn_new)
    return x, states, copies


def kernel(x_prompt, x_sample, cache_k, cache_v, state_ret, state_conv, state_ffn, page_table, rel_bias, norm1_g, w_in, b_gate, q_norm_g, k_norm_g, lam_vec, subln_g, w_o_a, ret_norm_g, w_o_b, conv_w, conv_b, conv_ln_g, conv_ln_b, w_o_c, w_out, norm2_g, w_ffn_in, ffn_conv_w, ffn_conv_b, w_ffn_down):
    bp, sp, d = x_prompt.shape
    db, t_new, _ = x_sample.shape
    depth = w_in.shape[0]
    past = page_table.shape[1] * cache_k.shape[2]
    pos_p = jnp.arange(sp)
    pos_s = jnp.tile(past + jnp.arange(t_new), db)
    yp = x_prompt.reshape(bp * sp, d)
    ys = x_sample.reshape(db * t_new, d)
    c_conv = conv_w.shape[2]
    d_ff = ffn_conv_w.shape[2]
    zero_conv = jnp.zeros((bp, K_CONV - 1, c_conv), F32)
    zero_ffn = jnp.zeros((bp, K_FFN - 1, d_ff), F32)
    sts_p, sts_s = [], []
    stacked = {'w_in': w_in, 'w_o_a': w_o_a, 'w_o_b': w_o_b, 'w_o_c': w_o_c, 'w_out': w_out,
               'w_ffn_in': w_ffn_in, 'w_ffn_down': w_ffn_down}
    for l in range(depth):
        lw = {'norm1_g': norm1_g[l], 'b_gate': b_gate[l], 'q_norm_g': q_norm_g[l], 'k_norm_g': k_norm_g[l],
              'conv_w': conv_w[l], 'conv_b': conv_b[l], 'conv_ln_g': conv_ln_g[l], 'conv_ln_b': conv_ln_b[l],
              'norm2_g': norm2_g[l], 'ffn_conv_w': ffn_conv_w[l], 'ffn_conv_b': ffn_conv_b[l]}
        lam_init = 0.8 - 0.6 * math.exp(-0.3 * l)

        def s_attn(proj, l=l, lam_init=lam_init, lw=lw):
            qn, kf, vf = attn_prep_sample(proj, lw['q_norm_g'], lw['k_norm_g'])
            return sample_attn(qn, kf, vf, cache_k, cache_v, l, page_table, rel_bias, lam_vec[l], subln_g[l],
                               lam_init, db, t_new), kf, vf

        def s_ret(proj, cos, sin, col0, l=l):
            return retention_sample(proj, cos, sin, ret_norm_g[l], state_ret[l], db, t_new, col0)

        ys, st, copies = _layer(ys, db, t_new, pos_s, s_attn, s_ret, state_conv[l], state_ffn[l], lw,
                                {name: (arr, l) for name, arr in stacked.items()}, False)
        sts_s.append(st)

        def p_attn(proj, l=l, lam_init=lam_init, lw=lw):
            t = _pick(sp, (_ATTN_TILE, LANES))
            qt, kf, kb, vf, vt = attn_prep_prompt(proj, lw['q_norm_g'], lw['k_norm_g'], bp, sp, t)
            return prompt_attn(qt, kb, vt, rel_bias, lam_vec[l], subln_g[l], lam_init, bp, sp, t), kf, vf

        def p_ret(proj, cos, sin, col0, l=l):
            return retention_prompt(proj, cos, sin, ret_norm_g[l], bp, sp, col0)

        yp, st, _ = _layer(yp, bp, sp, pos_p, p_attn, p_ret, zero_conv, zero_ffn, lw, copies, True)
        sts_p.append(st)

    def stk(sts, i):
        return jnp.stack([st[i] for st in sts], axis=0)

    return (yp.reshape(bp, sp, d), ys.reshape(db, t_new, d),
            stk(sts_p, 0), stk(sts_p, 1), stk(sts_p, 2), stk(sts_p, 3), stk(sts_p, 4),
            stk(sts_s, 0), stk(sts_s, 1), stk(sts_s, 2), stk(sts_s, 3), stk(sts_s, 4))
```

```python
import functools
import math

import numpy as np
import jax
import jax.numpy as jnp
from jax import lax
from jax.experimental import pallas as pl
from jax.experimental.pallas import tpu as pltpu

F32 = jnp.float32
BF16 = jnp.bfloat16

H_A = 8
DK_A = 64
DV_A = 128
H_B = 8
DK_B = 64
DV_B = 128
K_CONV = 31
K_FFN = 3
N_BUCKETS = 32
MAX_DIST = 128
EPS = 1e-6
NEG = -1e30

LANES = 128
SUBLANES = 8
VMEM_LIMIT = 56 * 1024 * 1024

_NT = (((1,), (1,)), ((), ()))
_TN = (((0,), (0,)), ((), ()))


def _cparams(sem):
    return pltpu.CompilerParams(dimension_semantics=sem, vmem_limit_bytes=VMEM_LIMIT)


def _pick(n, cands):
    for c in cands:
        if n % c == 0:
            return c
    return n


def _t5_bucket_np(d):
    max_exact = N_BUCKETS // 2
    d = np.maximum(d, 0)
    df = np.maximum(d, 1).astype(np.float32)
    large = max_exact + (np.log(df / np.float32(max_exact)) / np.float32(math.log(MAX_DIST / max_exact))
                         * np.float32(N_BUCKETS - max_exact)).astype(np.int32)
    return np.where(d < max_exact, d, np.minimum(large, N_BUCKETS - 1)).astype(np.int32)


def _bias_from_buckets(idx, rb_ref, h):
    acc = jnp.zeros(idx.shape, F32)
    for b in range(N_BUCKETS):
        acc = jnp.where(idx == b, rb_ref[b, h], acc)
    return acc


def _lam(lv_ref, lam_init):
    lv = lv_ref[...]
    a = jnp.sum(lv[0:1] * lv[1:2], axis=-1, keepdims=True)
    b = jnp.sum(lv[2:3] * lv[3:4], axis=-1, keepdims=True)
    return jnp.exp(a) - jnp.exp(b) + lam_init


def _silu(x):
    return x * jax.nn.sigmoid(x)


def _weight_form(w):
    if isinstance(w, tuple):
        arr, l = w
        return arr, True, l, arr.shape[1], arr.shape[2]
    return w, False, None, w.shape[0], w.shape[1]


def _norm_mm_body(x_ref, g_ref, w_ref, o_ref, *rest, emit):
    h_ref = rest[-1]

    @pl.when(pl.program_id(1) == 0)
    def _():
        x = x_ref[...]
        ms = jnp.mean(x * x, axis=-1, keepdims=True)
        h_ref[...] = ((x * lax.rsqrt(ms + EPS)) * g_ref[...]).astype(BF16)

    w = w_ref[...]
    if emit:
        w = w.astype(BF16)
        rest[0][...] = w
    o_ref[...] = jnp.dot(h_ref[...], w, preferred_element_type=F32)


def norm_matmul(x, g, w):
    n, d = x.shape
    w_arr, emit, l, _, c = _weight_form(w)
    tm = _pick(n, (1024, 512, 256, 128))
    tn = _pick(c, (1024, 512, 256, 128) if emit else (2048, 1024, 512, 256, 128))
    assert not emit or n == tm
    w_spec = (pl.BlockSpec((None, d, tn), lambda i, j: (l, 0, j)) if emit
              else pl.BlockSpec((d, tn), lambda i, j: (0, j)))
    out_specs = [pl.BlockSpec((tm, tn), lambda i, j: (i, j))]
    out_shape = [jax.ShapeDtypeStruct((n, c), F32)]
    if emit:
        out_specs.append(pl.BlockSpec((d, tn), lambda i, j: (0, j)))
        out_shape.append(jax.ShapeDtypeStruct((d, c), BF16))
    outs = pl.pallas_call(
        functools.partial(_norm_mm_body, emit=emit),
        grid=(n // tm, c // tn),
        in_specs=[pl.BlockSpec((tm, d), lambda i, j: (i, 0)),
                  pl.BlockSpec((1, d), lambda i, j: (0, 0)),
                  w_spec],
        out_specs=out_specs,
        out_shape=out_shape,
        scratch_shapes=[pltpu.VMEM((tm, d), BF16)],
        compiler_params=_cparams(("parallel", "arbitrary")),
        name="norm_matmul",
    )(x, g.reshape(1, d), w_arr)
    return tuple(outs) if emit else outs[0]


def _ffn_up_body(x_ref, g_ref, wg_ref, wu_ref, cw_ref, cb_ref, st_ref, y_ref, so_ref, h_ref, gbuf_ref, carry_ref,
                 *, tiles_per_seq):
    i = pl.program_id(0)
    j = pl.program_id(1)
    tm = x_ref.shape[0]
    lo = SUBLANES - (K_FFN - 1)

    @pl.when(j == 0)
    def _():
        x = x_ref[...]
        ms = jnp.mean(x * x, axis=-1, keepdims=True)
        h_ref[...] = ((x * lax.rsqrt(ms + EPS)) * g_ref[...]).astype(BF16)

    h = h_ref[...]
    first = (i % tiles_per_seq) == 0

    @pl.when(first)
    def _():
        gbuf_ref[0:lo, :] = jnp.zeros((lo, gbuf_ref.shape[1]), F32)
        gbuf_ref[lo:SUBLANES, :] = st_ref[0]

    @pl.when(jnp.logical_not(first))
    def _():
        gbuf_ref[0:SUBLANES, :] = carry_ref[j]

    gbuf_ref[SUBLANES:SUBLANES + tm, :] = jnp.dot(h, wg_ref[...], preferred_element_type=F32)
    acc = jnp.zeros((tm, gbuf_ref.shape[1]), F32)
    for k in range(K_FFN):
        acc = acc + gbuf_ref[lo + k:lo + k + tm, :] * cw_ref[k:k + 1, :]
    up = jnp.dot(h, wu_ref[...], preferred_element_type=F32)
    y_ref[...] = (_silu(acc + cb_ref[...]) * up).astype(BF16)
    carry_ref[j] = gbuf_ref[tm:tm + SUBLANES, :]
    so_ref[0] = gbuf_ref[lo + tm:SUBLANES + tm, :]


def ffn_up_prompt(x, g, w, state, conv_w, conv_b, batch, seq):
    n, d = x.shape
    f = w.shape[1] // 2
    tm = _pick(seq, (1024, 512, 256, 128))
    tn = _pick(f, (512, 256, 128))
    nj = f // tn
    tps = seq // tm
    act, tails = pl.pallas_call(
        functools.partial(_ffn_up_body, tiles_per_seq=tps),
        grid=(n // tm, nj),
        in_specs=[pl.BlockSpec((tm, d), lambda i, j: (i, 0)),
                  pl.BlockSpec((1, d), lambda i, j: (0, 0)),
                  pl.BlockSpec((d, tn), lambda i, j: (0, j)),
                  pl.BlockSpec((d, tn), lambda i, j: (0, nj + j)),
                  pl.BlockSpec((K_FFN, tn), lambda i, j: (0, j)),
                  pl.BlockSpec((1, tn), lambda i, j: (0, j)),
                  pl.BlockSpec((1, K_FFN - 1, tn), lambda i, j: (i // tps, 0, j))],
        out_specs=[pl.BlockSpec((tm, tn), lambda i, j: (i, j)),
                   pl.BlockSpec((1, K_FFN - 1, tn), lambda i, j: (i, 0, j))],
        out_shape=[jax.ShapeDtypeStruct((n, f), BF16),
                   jax.ShapeDtypeStruct((n // tm, K_FFN - 1, f), F32)],
        scratch_shapes=[pltpu.VMEM((tm, d), BF16),
                        pltpu.VMEM((SUBLANES + tm, tn), F32),
                        pltpu.VMEM((nj, SUBLANES, tn), F32)],
        compiler_params=_cparams(("arbitrary", "arbitrary")),
        name="ffn_up_prompt",
    )(x, g.reshape(1, d), w, w, conv_w, conv_b.reshape(1, f), state)
    return act, tails[tps - 1::tps]


def _mm_res_body(a_ref, w_ref, r_ref, o_ref, *rest, nk, emit):
    acc_ref = rest[-1]
    k = pl.program_id(2)
    w = w_ref[...]
    if emit:
        w = w.astype(BF16)
        rest[0][...] = w
    part = jnp.dot(a_ref[...], w, preferred_element_type=F32)
    if nk == 1:
        o_ref[...] = r_ref[...] + part
    else:
        @pl.when(k == 0)
        def _():
            acc_ref[...] = part

        @pl.when(jnp.logical_and(k > 0, k < nk - 1))
        def _():
            acc_ref[...] += part

        @pl.when(k == nk - 1)
        def _():
            o_ref[...] = r_ref[...] + (acc_ref[...] + part)


def matmul_residual(a, w, res):
    n = a.shape[0]
    w_arr, emit, l, kd, c = _weight_form(w)
    tk = kd if kd <= 2048 else _pick(kd, (2816, 2048, 1024, 512))
    nk = kd // tk
    if nk == 1 and not emit and c <= 2048:
        tm, tn = _pick(n, (512, 256, 128)), c
    else:
        tm, tn = _pick(n, (1024, 512, 256, 128)), _pick(c, (1024, 512, 256, 128))
    assert not emit or n == tm
    w_spec = (pl.BlockSpec((None, tk, tn), lambda i, j, k: (l, k, j)) if emit
              else pl.BlockSpec((tk, tn), lambda i, j, k: (k, j)))
    out_specs = [pl.BlockSpec((tm, tn), lambda i, j, k: (i, j))]
    out_shape = [jax.ShapeDtypeStruct((n, c), F32)]
    if emit:
        out_specs.append(pl.BlockSpec((tk, tn), lambda i, j, k: (k, j)))
        out_shape.append(jax.ShapeDtypeStruct((kd, c), BF16))
    outs = pl.pallas_call(
        functools.partial(_mm_res_body, nk=nk, emit=emit),
        grid=(n // tm, c // tn, nk),
        in_specs=[pl.BlockSpec((tm, tk), lambda i, j, k: (i, k)),
                  w_spec,
                  pl.BlockSpec((tm, tn), lambda i, j, k: (i, j))],
        out_specs=out_specs,
        out_shape=out_shape,
        scratch_shapes=[pltpu.VMEM((tm, tn), F32)],
        compiler_params=_cparams(("parallel", "parallel", "arbitrary")),
        name="matmul_residual",
    )(a, w_arr, res)
    return tuple(outs) if emit else outs[0]


def _group_rms(x, gain, gm):
    s = x * x
    hi = s.astype(BF16)
    lo = (s - hi.astype(F32)).astype(BF16)
    parts = []
    for c in range(x.shape[1] // LANES):
        sl = slice(c * LANES, (c + 1) * LANES)
        parts.append(jnp.dot(hi[:, sl], gm, preferred_element_type=F32)
                     + jnp.dot(lo[:, sl], gm, preferred_element_type=F32))
    ms = jnp.concatenate(parts, axis=-1) * (1.0 / DK_A)
    return (x * lax.rsqrt(ms + EPS)) * gain


_LOG2E = math.log2(math.e)
_Q_SCALE = DK_A ** -0.5 * _LOG2E
_DVX = DV_A + 16


def _attn_prep_sample_body(q_ref, k_ref, v_ref, qg_ref, kg_ref, gm_ref, qo_ref, kf_ref, vf_ref):
    gm = gm_ref[...]
    qo_ref[...] = _group_rms(q_ref[...], qg_ref[...], gm) * _Q_SCALE
    kf_ref[...] = _group_rms(k_ref[...], kg_ref[...], gm)
    vf_ref[...] = v_ref[...]


def _attn_prep_prompt_body(q_ref, k_ref, v_ref, qg_ref, kg_ref, gm_ref, qt_ref, kf_ref, kb_ref, vf_ref, vt_ref):
    gm = gm_ref[...]
    qn = _group_rms(q_ref[...], qg_ref[...], gm) * _Q_SCALE
    kn = _group_rms(k_ref[...], kg_ref[...], gm)
    kf_ref[...] = kn
    kb_ref[...] = kn.astype(BF16)
    v = v_ref[...]
    vf_ref[...] = v
    for h in range(H_A):
        hs = slice(h * LANES, (h + 1) * LANES)
        qt_ref[0, h, 0] = qn[:, hs].T.astype(BF16)
        vt_ref[0, h, 0, 0:DV_A, :] = v[:, hs].T.astype(BF16)
        vt_ref[0, h, 0, DV_A:_DVX, :] = jnp.ones((_DVX - DV_A, v.shape[0]), BF16)


def _attn_prep_consts(q_gain, k_gain):
    w = H_A * 2 * DK_A
    gm = np.kron(np.eye(LANES // DK_A, dtype=np.float32), np.ones((DK_A, DK_A), np.float32))
    return (jnp.tile(q_gain, w // DK_A).reshape(1, w), jnp.tile(k_gain, w // DK_A).reshape(1, w),
            jnp.asarray(gm, BF16))


def attn_prep_sample(proj, q_gain, k_gain):
    n = proj.shape[0]
    w = H_A * 2 * DK_A
    tm = _pick(n, (512, 256, 128))
    col = lambda c: pl.BlockSpec((tm, w), lambda i, c=c: (i, c))
    cst = lambda shape: pl.BlockSpec(shape, lambda i: (0, 0))
    out = lambda: pl.BlockSpec((tm, w), lambda i: (i, 0))
    return pl.pallas_call(
        _attn_prep_sample_body,
        grid=(n // tm,),
        in_specs=[col(0), col(1), col(2), cst((1, w)), cst((1, w)), cst((LANES, LANES))],
        out_specs=[out(), out(), out()],
        out_shape=[jax.ShapeDtypeStruct((n, w), F32)] * 3,
        compiler_params=_cparams(("parallel",)),
        name="attn_prep_sample",
    )(proj, proj, proj, *_attn_prep_consts(q_gain, k_gain))


def attn_prep_prompt(proj, q_gain, k_gain, batch, seq, t):
    n = batch * seq
    w = H_A * 2 * DK_A
    nt = seq // t
    col = lambda c: pl.BlockSpec((t, w), lambda b, i, c=c: (b * nt + i, c))
    cst = lambda shape: pl.BlockSpec(shape, lambda b, i: (0, 0))
    out = lambda: pl.BlockSpec((t, w), lambda b, i: (b * nt + i, 0))
    tr = lambda rows=LANES: pl.BlockSpec((1, H_A, 1, rows, t), lambda b, i: (b, 0, i, 0, 0))
    tr_shape = lambda rows=LANES: jax.ShapeDtypeStruct((batch, H_A, nt, rows, t), BF16)
    return pl.pallas_call(
        _attn_prep_prompt_body,
        grid=(batch, nt),
        in_specs=[col(0), col(1), col(2), cst((1, w)), cst((1, w)), cst((LANES, LANES))],
        out_specs=[tr(), out(), out(), out(), tr(_DVX)],
        out_shape=[tr_shape(), jax.ShapeDtypeStruct((n, w), F32), jax.ShapeDtypeStruct((n, w), BF16),
                   jax.ShapeDtypeStruct((n, w), F32), tr_shape(_DVX)],
        compiler_params=_cparams(("parallel", "parallel")),
        name="attn_prep_prompt",
    )(proj, proj, proj, *_attn_prep_consts(q_gain, k_gain))


def _prompt_attn_body(rb_ref, ib_ref, lv_ref, sg_ref, qt_ref, k_ref, vt_ref, o_ref,
                      bias_ref, m_ref, acc_ref, *, t, lam_init, n_hg):
    hg = pl.program_id(1)
    qi = pl.program_id(2)
    heads = range(_ATTN_HEADS)
    build = qi == 0
    if n_hg == 1:
        build = jnp.logical_and(build, pl.program_id(0) == 0)

    @pl.when(build)
    def _():
        for g in heads:
            h = hg * _ATTN_HEADS + g
            far_bias = rb_ref[N_BUCKETS - 1, h]
            for o in range(2):
                idx = ib_ref[o]
                bias_ref[g, o] = jnp.where(idx >= 0, (_bias_from_buckets(idx, rb_ref, h) - far_bias) * _LOG2E, NEG)

    q2t = []
    for g in heads:
        qt = qt_ref[0, g, 0]
        row = lax.broadcasted_iota(jnp.int32, qt.shape, 0)
        zero = jnp.zeros_like(qt)
        q2t.append(jnp.concatenate([jnp.where(row < DK_A, qt, zero), jnp.where(row >= DK_A, qt, zero)], axis=1))

    m_ref[...] = jnp.full(m_ref.shape, NEG, F32)
    acc_ref[...] = jnp.zeros(acc_ref.shape, F32)

    def tile(kj, o):
        off = pl.multiple_of(kj * t, t)
        s = [jnp.dot(k_ref[0, pl.ds(off, t), g * LANES:(g + 1) * LANES], q2t[g], preferred_element_type=F32)
             for g in heads]
        if o is not None:
            s = [s[g] + bias_ref[g, o] for g in heads]
        m_old = [m_ref[g] for g in heads]
        m_new = [jnp.maximum(m_old[g], jnp.max(s[g], axis=0, keepdims=True)) for g in heads]
        alpha = [jnp.exp2(m_old[g] - m_new[g]) for g in heads]
        p = [jnp.exp2(s[g] - m_new[g]).astype(BF16) for g in heads]
        for g in heads:
            m_ref[g] = m_new[g]
        pv = [jnp.dot(vt_ref[0, g, kj], p[g], preferred_element_type=F32) for g in heads]
        for g in heads:
            acc_ref[g] = alpha[g] * acc_ref[g] + pv[g]

    def far(kj, carry):
        tile(kj, None)
        return carry

    lax.fori_loop(0, jnp.maximum(qi - 1, 0), far, 0)

    @pl.when(qi >= 1)
    def _():
        tile(qi - 1, 1)

    tile(qi, 0)

    lam = _lam(lv_ref, lam_init)
    for g in heads:
        o2t = acc_ref[g, 0:DV_A, :] / acc_ref[g, DV_A:DV_A + 1, :]
        o = (o2t[:, 0:t] - lam * o2t[:, t:2 * t]).T
        ms = jnp.mean(o * o, axis=-1, keepdims=True)
        y = ((o * lax.rsqrt(ms + EPS)) * sg_ref[...]) * (1.0 - lam_init)
        o_ref[0, :, g * LANES:(g + 1) * LANES] = y.astype(BF16)


_ATTN_TILE = 256
_ATTN_HEADS = 8


def prompt_attn(qt, kb, vt, rel_bias, lam_vec, subln_g, lam_init, batch, seq, t):
    w = H_A * DV_A
    assert t >= MAX_DIST and seq % t == 0
    nq = seq // t
    hgs = _ATTN_HEADS
    key = np.arange(t)[:, None]
    qry = (np.arange(2 * t) % t)[None, :]
    d0 = qry - key
    ib = jnp.asarray(np.stack([np.where(d0 >= 0, _t5_bucket_np(d0), -1), _t5_bucket_np(d0 + t)]), jnp.int32)
    assert int(_t5_bucket_np(np.array([t + 1]))[0]) == N_BUCKETS - 1
    out = pl.pallas_call(
        functools.partial(_prompt_attn_body, t=t, lam_init=lam_init, n_hg=H_A // hgs),
        grid=(batch, H_A // hgs, nq),
        in_specs=[pl.BlockSpec(memory_space=pltpu.SMEM),
                  pl.BlockSpec((2, t, 2 * t), lambda b, h, i: (0, 0, 0)),
                  pl.BlockSpec((4, DK_A), lambda b, h, i: (0, 0)),
                  pl.BlockSpec((1, DV_A), lambda b, h, i: (0, 0)),
                  pl.BlockSpec((1, hgs, 1, LANES, t), lambda b, h, i: (b, h, i, 0, 0)),
                  pl.BlockSpec((1, seq, hgs * LANES), lambda b, h, i: (b, 0, h)),
                  pl.BlockSpec((1, hgs, nq, _DVX, t), lambda b, h, i: (b, h, 0, 0, 0))],
        out_specs=pl.BlockSpec((1, t, hgs * LANES), lambda b, h, i: (b, i, h)),
        out_shape=jax.ShapeDtypeStruct((batch, seq, w), BF16),
        scratch_shapes=[pltpu.VMEM((hgs, 2, t, 2 * t), F32),
                        pltpu.VMEM((hgs, 1, 2 * t), F32),
                        pltpu.VMEM((hgs, _DVX, 2 * t), F32)],
        compiler_params=_cparams(("arbitrary", "arbitrary", "arbitrary")),
        name="prompt_attn",
    )(rel_bias, ib, lam_vec, subln_g.reshape(1, DV_A), qt, kb.reshape(batch, seq, w), vt)
    return out.reshape(batch * seq, w)


_RH = SUBLANES


_KPAD = LANES // H_A


def _sample_attn_body(pt_ref, rb_ref, ibp_ref, ibn_ref, lv_ref, sg_ref, q_ref, kn_ref, vn_ref, *rest,
                      pps, nsteps, lam_init):
    k_refs = rest[0:pps]
    v_refs = rest[pps:2 * pps]
    o_ref = rest[2 * pps]
    biasp_ref, biasn_ref, m_ref, l_ref, acc_ref = rest[2 * pps + 1:]
    b = pl.program_id(0)
    s_id = pl.program_id(1)

    t_new = o_ref.shape[1]

    @pl.when(jnp.logical_and(b == 0, s_id == 0))
    def _():
        for h in range(H_A):
            sl = slice(h * _RH, (h + 1) * _RH)
            far_bias = rb_ref[N_BUCKETS - 1, h]
            for o in range(2):
                idx = ibp_ref[o, sl, :]
                biasp_ref[o, sl, :] = jnp.where(idx >= 0, (_bias_from_buckets(idx, rb_ref, h) - far_bias) * _LOG2E,
                                                NEG)
            idx = ibn_ref[sl, :]
            biasn_ref[sl, :] = jnp.where(idx >= 0, (_bias_from_buckets(idx, rb_ref, h) - far_bias) * _LOG2E, NEG)

    @pl.when(s_id == 0)
    def _():
        m_ref[...] = jnp.full(m_ref.shape, NEG, F32)
        l_ref[...] = jnp.zeros(l_ref.shape, F32)
        acc_ref[...] = jnp.zeros(acc_ref.shape, F32)

    q = q_ref[0]
    row = lax.broadcasted_iota(jnp.int32, (_RH, LANES), 0)
    lane = lax.broadcasted_iota(jnp.int32, (_RH, LANES), 1)
    keep = (row < t_new) == (lane < DK_A)
    q2 = jnp.concatenate([jnp.where(keep, q[:, h * LANES:(h + 1) * LANES], 0.0) for h in range(H_A)],
                         axis=0).astype(BF16)

    def attend(kt, vt, bias):
        s = lax.dot_general(q2, kt, _NT, preferred_element_type=F32) + bias
        m_old = m_ref[...]
        m_new = jnp.maximum(m_old, jnp.max(s, axis=-1, keepdims=True))
        alpha = jnp.exp2(m_old - m_new)
        p = jnp.exp2(s - m_new)
        l_ref[...] = alpha * l_ref[...] + jnp.sum(p, axis=-1, keepdims=True)
        acc_ref[...] = alpha * acc_ref[...] + jnp.dot(p.astype(BF16), vt, preferred_element_type=F32)
        m_ref[...] = m_new

    kt = jnp.concatenate([r[0, 0].astype(BF16) for r in k_refs], axis=0)
    vt = jnp.concatenate([r[0, 0].astype(BF16) for r in v_refs], axis=0)
    last = s_id == nsteps - 1
    bias = jnp.concatenate([biasp_ref[0]] * (pps - 1) + [biasp_ref[jnp.where(last, 1, 0)]], axis=1)
    attend(kt, vt, bias)

    @pl.when(last)
    def _():
        attend(kn_ref[0].astype(BF16), vn_ref[0].astype(BF16), biasn_ref[...])
        o2 = acc_ref[...] / l_ref[...]
        lam = _lam(lv_ref, lam_init)
        for h in range(H_A):
            x = o2[h * _RH:(h + 1) * _RH]
            o = x - lam * pltpu.roll(x, t_new, 0)
            ms = jnp.mean(o * o, axis=-1, keepdims=True)
            y = ((o * lax.rsqrt(ms + EPS)) * sg_ref[...]) * (1.0 - lam_init)
            o_ref[0, :, h * LANES:(h + 1) * LANES] = y[0:t_new]


def sample_attn(qn, kf, vf, cache_k, cache_v, layer, page_table, rel_bias, lam_vec, subln_g, lam_init, db, t_new):
    w = H_A * DV_A
    depth, n_pool, page = cache_k.shape[:3]
    n_pages = page_table.shape[1]
    assert 2 * t_new == _RH and t_new <= _KPAD and page >= MAX_DIST
    pps = _pick(n_pages, (16, 8, 4, 2, 1))
    nsteps = n_pages // pps
    rows = H_A * _RH
    kw = page * H_A
    rh = (np.arange(rows) // _RH)[:, None]
    rt = (np.arange(rows) % t_new)[:, None]
    ck_key, ck_head = (np.arange(kw) // H_A)[None, :], (np.arange(kw) % H_A)[None, :]
    ib_far = np.where(rh == ck_head, N_BUCKETS - 1, -1)
    ib_last = np.where(rh == ck_head, _t5_bucket_np(page + rt - ck_key), -1)
    nk_key, nk_head = (np.arange(_KPAD * H_A) // H_A)[None, :], (np.arange(_KPAD * H_A) % H_A)[None, :]
    d_new = rt - nk_key
    ib_new = np.where((rh == nk_head) & (d_new >= 0) & (nk_key < t_new), _t5_bucket_np(d_new), -1)
    ibp = jnp.asarray(np.stack([ib_far, ib_last]), jnp.int32)
    ibn = jnp.asarray(ib_new, jnp.int32)
    assert int(_t5_bucket_np(np.array([page + 1]))[0]) == N_BUCKETS - 1

    q3 = qn.reshape(db, t_new, w)
    q_pad = jnp.concatenate([q3, q3], axis=1)

    def new_rows(x):
        return jnp.pad(x.reshape(db, t_new * H_A, LANES), ((0, 0), (0, (_KPAD - t_new) * H_A), (0, 0)))

    ck = cache_k.reshape(depth, n_pool, kw, LANES)
    cv = cache_v.reshape(depth, n_pool, kw, LANES)
    new_spec = pl.BlockSpec((1, _KPAD * H_A, LANES), lambda b, s, pt: (b, 0, 0))

    def page_spec(i):
        return pl.BlockSpec((1, 1, kw, LANES), lambda b, s, pt, i=i: (layer, pt[b, s * pps + i], 0, 0))

    grid_spec = pltpu.PrefetchScalarGridSpec(
        num_scalar_prefetch=1,
        grid=(db, nsteps),
        in_specs=[pl.BlockSpec(memory_space=pltpu.SMEM),
                  pl.BlockSpec((2, rows, kw), lambda b, s, pt: (0, 0, 0)),
                  pl.BlockSpec((rows, _KPAD * H_A), lambda b, s, pt: (0, 0)),
                  pl.BlockSpec((4, DK_A), lambda b, s, pt: (0, 0)),
                  pl.BlockSpec((1, DV_A), lambda b, s, pt: (0, 0)),
                  pl.BlockSpec((1, _RH, w), lambda b, s, pt: (b, 0, 0)),
                  new_spec, new_spec]
                 + [page_spec(i) for i in range(pps)] + [page_spec(i) for i in range(pps)],
        out_specs=pl.BlockSpec((1, t_new, w), lambda b, s, pt: (b, 0, 0)),
        scratch_shapes=[pltpu.VMEM((2, rows, kw), F32),
                        pltpu.VMEM((rows, _KPAD * H_A), F32),
                        pltpu.VMEM((rows, 1), F32),
                        pltpu.VMEM((rows, 1), F32),
                        pltpu.VMEM((rows, DV_A), F32)],
    )
    out = pl.pallas_call(
        functools.partial(_sample_attn_body, pps=pps, nsteps=nsteps, lam_init=lam_init),
        grid_spec=grid_spec,
        out_shape=jax.ShapeDtypeStruct((db, t_new, w), F32),
        compiler_params=_cparams(("arbitrary", "arbitrary")),
        name="sample_attn",
    )(page_table, rel_bias, ibp, ibn, lam_vec, subln_g.reshape(1, DV_A), q_pad, new_rows(kf), new_rows(vf),
      *([ck] * pps), *([cv] * pps))
    return out.reshape(db * t_new, w).astype(BF16)


def _log_gamma(h):
    return float(np.log1p(-np.exp2(np.float32(-5.0 - h), dtype=np.float32), dtype=np.float32))


def _swap_halves(x):
    n = x.shape[1]
    half = DK_B // 2
    lane = lax.broadcasted_iota(jnp.int32, x.shape, 1)
    return jnp.where((lane % DK_B) < half, pltpu.roll(x, n - half, 1), pltpu.roll(x, half, 1))


def _rotary(x, cos, sin):
    return x * cos + _swap_halves(x) * sin


def _ret_tile(q, k, v, c):
    r = q.shape[0]
    shift = int(math.log2(c))
    assert 1 << shift == c
    ri = lax.broadcasted_iota(jnp.int32, (r, r), 0)
    ci = lax.broadcasted_iota(jnp.int32, (r, r), 1)
    same = (ri >> shift) == (ci >> shift)
    diff = ((ri & (c - 1)) - (ci & (c - 1))).astype(F32)
    valid = jnp.logical_and(same, diff >= 0.0)
    ti = (lax.broadcasted_iota(jnp.int32, (r, 1), 0) & (c - 1)).astype(F32)
    lane = lax.broadcasted_iota(jnp.int32, (r, LANES), 1)
    first = lane < DK_B
    o_inner, q_m, dec_q, k_dec = [], [], [], []
    for p in range(H_B // 2):
        ps = slice(p * LANES, (p + 1) * LANES)
        qp, kp = q[:, ps], k[:, ps]
        kpb = kp.astype(BF16)
        lg0, lg1 = _log_gamma(2 * p), _log_gamma(2 * p + 1)
        k_dec.append((kp * jnp.where(first, jnp.exp((c - 1.0 - ti) * lg0), jnp.exp((c - 1.0 - ti) * lg1))).astype(BF16))
        for hh in range(2):
            h = 2 * p + hh
            lg = lg1 if hh else lg0
            qm = jnp.where(first if hh == 0 else jnp.logical_not(first), qp, 0.0).astype(BF16)
            dmat = jnp.where(valid, jnp.exp(jnp.maximum(diff, 0.0) * lg), 0.0)
            inner = lax.dot_general(qm, kpb, _NT, preferred_element_type=F32) * dmat
            vh = v[:, h * DV_B:(h + 1) * DV_B].astype(BF16)
            o_inner.append(jnp.dot(inner.astype(BF16), vh, preferred_element_type=F32))
            q_m.append(qm)
            dec_q.append(jnp.exp((ti + 1.0) * lg))
    return o_inner, q_m, dec_q, k_dec


def _state_update(s_pair, kd, v, p, c):
    row = lax.broadcasted_iota(jnp.int32, (LANES, DV_B), 0)
    top = row < DK_B
    u0 = lax.dot_general(kd, v[:, (2 * p) * DV_B:(2 * p + 1) * DV_B].astype(BF16), _TN, preferred_element_type=F32)
    u1 = lax.dot_general(kd, v[:, (2 * p + 1) * DV_B:(2 * p + 2) * DV_B].astype(BF16), _TN, preferred_element_type=F32)
    gc = jnp.where(top, math.exp(c * _log_gamma(2 * p)), math.exp(c * _log_gamma(2 * p + 1)))
    return s_pair * gc + jnp.where(top, u0, u1)


def _ret_epilogue(o, g, ng):
    ms = jnp.mean(o * o, axis=-1, keepdims=True)
    return ((o * lax.rsqrt(ms + EPS)) * ng) * _silu(g)


def _ret_prompt_body(q_ref, k_ref, v_ref, g_ref, cos_ref, sin_ref, ng_ref, y_ref, so_ref, s_ref, *, c):
    ci = pl.program_id(1)

    @pl.when(ci == 0)
    def _():
        s_ref[...] = jnp.zeros(s_ref.shape, F32)

    cos, sin = cos_ref[...], sin_ref[...]
    q = _rotary(q_ref[...], cos, sin)
    k = _rotary(k_ref[...], cos, sin) * (DK_B ** -0.5)
    v = v_ref[...]
    o_inner, q_m, dec_q, k_dec = _ret_tile(q, k, v, c)
    ng = ng_ref[...]
    for p in range(H_B // 2):
        s_pair = s_ref[p]
        sb = s_pair.astype(BF16)
        for hh in range(2):
            h = 2 * p + hh
            o = o_inner[h] + jnp.dot(q_m[h], sb, preferred_element_type=F32) * dec_q[h]
            hs = slice(h * DV_B, (h + 1) * DV_B)
            y_ref[:, hs] = _ret_epilogue(o, g_ref[:, hs], ng).astype(BF16)
        s_ref[p] = _state_update(s_pair, k_dec[p], v, p, c)

    @pl.when(ci == pl.num_programs(1) - 1)
    def _():
        so_ref[0] = s_ref[...]


def retention_prompt(proj, cos, sin, ret_norm_g, batch, seq, col0):
    n = batch * seq
    c = 128 if seq % 128 == 0 else seq
    nc = seq // c
    wq = H_B * DK_B
    wv = H_B * DV_B
    assert col0 % wv == 0
    qc, vc = col0 // wq, col0 // wv
    row = lambda b, i: b * nc + i
    y, st = pl.pallas_call(
        functools.partial(_ret_prompt_body, c=c),
        grid=(batch, nc),
        in_specs=[pl.BlockSpec((c, wq), lambda b, i: (row(b, i), qc)),
                  pl.BlockSpec((c, wq), lambda b, i: (row(b, i), qc + 1)),
                  pl.BlockSpec((c, wv), lambda b, i: (row(b, i), vc + 1)),
                  pl.BlockSpec((c, wv), lambda b, i: (row(b, i), vc + 2)),
                  pl.BlockSpec((c, wq), lambda b, i: (i, 0)),
                  pl.BlockSpec((c, wq), lambda b, i: (i, 0)),
                  pl.BlockSpec((1, DV_B), lambda b, i: (0, 0))],
        out_specs=[pl.BlockSpec((c, wv), lambda b, i: (row(b, i), 0)),
                   pl.BlockSpec((1, H_B // 2, LANES, DV_B), lambda b, i: (b, 0, 0, 0))],
        out_shape=[jax.ShapeDtypeStruct((n, wv), BF16),
                   jax.ShapeDtypeStruct((batch, H_B // 2, LANES, DV_B), F32)],
        scratch_shapes=[pltpu.VMEM((H_B // 2, LANES, DV_B), F32)],
        compiler_params=_cparams(("arbitrary", "arbitrary")),
        name="retention_prompt",
    )(proj, proj, proj, proj, cos, sin, ret_norm_g.reshape(1, DV_B))
    return y, st.reshape(batch, H_B, DK_B, DV_B)


def _ret_sample_body(q_ref, k_ref, v_ref, g_ref, cos_ref, sin_ref, ng_ref, s0_ref, y_ref, so_ref,
                     o_ref, qm_ref, kd_ref, *, c):
    gi = pl.program_id(0)
    r = q_ref.shape[0]
    shift = int(math.log2(c))
    v = v_ref[...]

    @pl.when(gi == 0)
    def _():
        cos, sin = cos_ref[...], sin_ref[...]
        q = _rotary(q_ref[...], cos, sin)
        k = _rotary(k_ref[...], cos, sin) * (DK_B ** -0.5)
        o_inner, q_m, dec_q, k_dec = _ret_tile(q, k, v, c)
        for h in range(H_B):
            o_ref[:, h * DV_B:(h + 1) * DV_B] = o_inner[h]
            qm_ref[h] = q_m[h]
        for p in range(H_B // 2):
            kd_ref[p] = k_dec[p]

    rsel = (lax.broadcasted_iota(jnp.int32, (r, LANES), 0) >> shift) == gi
    ti = (lax.broadcasted_iota(jnp.int32, (r, 1), 0) & (c - 1)).astype(F32)
    zero = jnp.zeros((r, LANES), BF16)
    for p in range(H_B // 2):
        s_pair = s0_ref[0, p]
        sb = s_pair.astype(BF16)
        for hh in range(2):
            h = 2 * p + hh
            qm = jnp.where(rsel, qm_ref[h], zero)
            hs = slice(h * DV_B, (h + 1) * DV_B)
            o_ref[:, hs] += jnp.dot(qm, sb, preferred_element_type=F32) * jnp.exp((ti + 1.0) * _log_gamma(h))
        kd = jnp.where(rsel, kd_ref[p], zero)
        so_ref[0, p] = _state_update(s_pair, kd, v, p, c)

    @pl.when(gi == pl.num_programs(0) - 1)
    def _():
        ng = ng_ref[...]
        for h in range(H_B):
            hs = slice(h * DV_B, (h + 1) * DV_B)
            y_ref[:, hs] = _ret_epilogue(o_ref[:, hs], g_ref[:, hs], ng).astype(BF16)


def retention_sample(proj, cos, sin, ret_norm_g, state0, db, t_new, col0):
    n = db * t_new
    wq = H_B * DK_B
    wv = H_B * DV_B
    qc, vc = col0 // wq, col0 // wv
    s0 = state0.reshape(db, H_B // 2, LANES, DV_B)
    st_spec = pl.BlockSpec((1, H_B // 2, LANES, DV_B), lambda g: (g, 0, 0, 0))
    y, st = pl.pallas_call(
        functools.partial(_ret_sample_body, c=t_new),
        grid=(db,),
        in_specs=[pl.BlockSpec((n, wq), lambda g: (0, qc)),
                  pl.BlockSpec((n, wq), lambda g: (0, qc + 1)),
                  pl.BlockSpec((n, wv), lambda g: (0, vc + 1)),
                  pl.BlockSpec((n, wv), lambda g: (0, vc + 2)),
                  pl.BlockSpec((n, wq), lambda g: (0, 0)),
                  pl.BlockSpec((n, wq), lambda g: (0, 0)),
                  pl.BlockSpec((1, DV_B), lambda g: (0, 0)),
                  st_spec],
        out_specs=[pl.BlockSpec((n, wv), lambda g: (0, 0)), st_spec],
        out_shape=[jax.ShapeDtypeStruct((n, wv), BF16),
                   jax.ShapeDtypeStruct((db, H_B // 2, LANES, DV_B), F32)],
        scratch_shapes=[pltpu.VMEM((n, wv), F32),
                        pltpu.VMEM((H_B, n, LANES), BF16),
                        pltpu.VMEM((H_B // 2, n, LANES), BF16)],
        compiler_params=_cparams(("arbitrary",)),
        name="retention_sample",
    )(proj, proj, proj, proj, cos, sin, ret_norm_g.reshape(1, DV_B), s0)
    return y, st.reshape(db, H_B, DK_B, DV_B)


def _rotary_tables(pos):
    half = DK_B // 2
    inv = 1.0 / (10000.0 ** (jnp.arange(half, dtype=F32) / half))
    ang = pos.astype(F32)[:, None] * inv[None, :]
    cos, sin = jnp.cos(ang), jnp.sin(ang)
    cos_t = jnp.tile(jnp.concatenate([cos, cos], axis=-1), (1, H_B))
    sin_t = jnp.tile(jnp.concatenate([-sin, sin], axis=-1), (1, H_B))
    return cos_t, sin_t


_HALO = 32
_CONV_ROWS = 64
_SEQS_PER_STEP = 8


def _conv_body(a_ref, b_ref, st_ref, w_ref, cb_ref, lg_ref, lb_ref, y_ref, so_ref, full_ref, sh_ref, uc_ref, *, ts):
    for q in range(a_ref.shape[0]):
        _conv_sequence(a_ref.at[q], b_ref.at[q], st_ref.at[q], w_ref, cb_ref, lg_ref, lb_ref,
                       y_ref.at[q], so_ref.at[q], full_ref, sh_ref, uc_ref, ts)


def _conv_sequence(a_ref, b_ref, st_ref, w_ref, cb_ref, lg_ref, lb_ref, y_ref, so_ref, full_ref, sh_ref, uc_ref, ts):
    ti = pl.program_id(1)
    lo = _HALO - (K_CONV - 1)

    @pl.when(ti == 0)
    def _():
        full_ref[0:lo, :] = jnp.zeros((lo, full_ref.shape[1]), F32)
        full_ref[lo:_HALO, :] = st_ref[...]

    full_ref[_HALO:_HALO + ts, :] = a_ref[...] * jax.nn.sigmoid(b_ref[...])
    nsh = sh_ref.shape[1]
    for s in range(1, SUBLANES):
        sh_ref[s - 1] = full_ref[s:s + nsh, :]
    rb = min(ts, _CONV_ROWS)
    for c in range(full_ref.shape[1] // LANES):
        cs = slice(c * LANES, (c + 1) * LANES)
        for r0 in range(0, ts, rb):
            acc = None
            for j in range(K_CONV):
                s = (lo + j) % SUBLANES
                a0 = lo + j - s + r0
                win = full_ref[a0:a0 + rb, cs] if s == 0 else sh_ref[s - 1, a0:a0 + rb, cs]
                term = win * w_ref[j:j + 1, cs]
                acc = term if acc is None else acc + term
            uc_ref[r0:r0 + rb, cs] = acc + cb_ref[:, cs]
    uc = uc_ref[...]
    mu = jnp.mean(uc, axis=-1, keepdims=True)
    dev = uc - mu
    var = jnp.mean(dev * dev, axis=-1, keepdims=True)
    y = (dev * lax.rsqrt(var + EPS)) * lg_ref[...] + lb_ref[...]
    y_ref[...] = _silu(y).astype(BF16)
    so_ref[...] = full_ref[lo + ts:_HALO + ts, :]
    if ts >= _HALO:
        full_ref[0:_HALO, :] = full_ref[ts:ts + _HALO, :]


def conv_module(proj3, state, conv_w, conv_b, ln_g, ln_b, col0):
    b, s, _ = proj3.shape
    c = conv_w.shape[1]
    ts = _pick(s, (256, 128)) if s >= 128 else s
    nt = s // ts
    assert nt == 1 or ts >= _HALO
    cc = col0 // c
    gb = _pick(b, (_SEQS_PER_STEP, 1)) if ts < LANES else 1
    vec = lambda: pl.BlockSpec((1, c), lambda i, t: (0, 0))
    return pl.pallas_call(
        functools.partial(_conv_body, ts=ts),
        grid=(b // gb, nt),
        in_specs=[pl.BlockSpec((gb, ts, c), lambda i, t: (i, t, cc)),
                  pl.BlockSpec((gb, ts, c), lambda i, t: (i, t, cc + 1)),
                  pl.BlockSpec((gb, K_CONV - 1, c), lambda i, t: (i, 0, 0)),
                  pl.BlockSpec((K_CONV, c), lambda i, t: (0, 0)),
                  vec(), vec(), vec()],
        out_specs=[pl.BlockSpec((gb, ts, c), lambda i, t: (i, t, 0)),
                   pl.BlockSpec((gb, K_CONV - 1, c), lambda i, t: (i, 0, 0))],
        out_shape=[jax.ShapeDtypeStruct((b, s, c), BF16),
                   jax.ShapeDtypeStruct((b, K_CONV - 1, c), F32)],
        scratch_shapes=[pltpu.VMEM((_HALO + ts, c), F32),
                        pltpu.VMEM((SUBLANES - 1, _HALO + ts - SUBLANES, c), F32),
                        pltpu.VMEM((ts, c), F32)],
        compiler_params=_cparams(("arbitrary", "arbitrary")),
        name="conv_module",
    )(proj3, proj3, state, conv_w, conv_b.reshape(1, c), ln_g.reshape(1, c), ln_b.reshape(1, c))


def _merge_body(a_ref, b_ref, c_ref, wa_ref, wb_ref, wc_ref, ga_ref, gb_ref, gc_ref,
                ba_ref, bb_ref, bc_ref, o_ref, *copies, emit):
    ws = [wa_ref[...], wb_ref[...], wc_ref[...]]
    if emit:
        ws = [w.astype(BF16) for w in ws]
        for copy_ref, w in zip(copies, ws):
            copy_ref[...] = w
    ya = jnp.dot(a_ref[...], ws[0], preferred_element_type=F32)
    m = jax.nn.sigmoid(ga_ref[...] + ba_ref[...]) * ya
    yb = jnp.dot(b_ref[...], ws[1], preferred_element_type=F32)
    m = m + jax.nn.sigmoid(gb_ref[...] + bb_ref[...]) * yb
    yc = jnp.dot(c_ref[...], ws[2], preferred_element_type=F32)
    m = m + jax.nn.sigmoid(gc_ref[...] + bc_ref[...]) * yc
    o_ref[...] = m.astype(BF16)


def branch_merge(a, b, c, wa, wb, wc, proj, b_gate, col0):
    n, kd = a.shape
    (wa_arr, emit, l, _, d), (wb_arr, _, _, _, _), (wc_arr, _, _, _, _) = (_weight_form(w) for w in (wa, wb, wc))
    tm = _pick(n, (512, 256, 128))
    tn = _pick(d, (1024, 512, 256, 128))
    nj = d // tn
    g0 = col0 // tn
    bg = b_gate.reshape(1, 3 * d)
    assert not emit or n == tm
    x_spec = lambda: pl.BlockSpec((tm, kd), lambda j, i: (i, 0))
    w_spec = lambda: (pl.BlockSpec((None, kd, tn), lambda j, i: (l, 0, j)) if emit
                      else pl.BlockSpec((kd, tn), lambda j, i: (0, j)))
    g_spec = lambda k: pl.BlockSpec((tm, tn), lambda j, i, k=k: (i, g0 + k * nj + j))
    bias_spec = lambda k: pl.BlockSpec((1, tn), lambda j, i, k=k: (0, k * nj + j))
    out_specs = [pl.BlockSpec((tm, tn), lambda j, i: (i, j))]
    out_shape = [jax.ShapeDtypeStruct((n, d), BF16)]
    if emit:
        out_specs += [pl.BlockSpec((kd, tn), lambda j, i: (0, j))] * 3
        out_shape += [jax.ShapeDtypeStruct((kd, d), BF16)] * 3
    outs = pl.pallas_call(
        functools.partial(_merge_body, emit=emit),
        grid=(nj, n // tm),
        in_specs=[x_spec(), x_spec(), x_spec(), w_spec(), w_spec(), w_spec(),
                  g_spec(0), g_spec(1), g_spec(2), bias_spec(0), bias_spec(1), bias_spec(2)],
        out_specs=out_specs,
        out_shape=out_shape,
        compiler_params=_cparams(("parallel", "arbitrary")),
        name="branch_merge",
    )(a, b, c, wa_arr, wb_arr, wc_arr, proj, proj, proj, bg, bg, bg)
    return tuple(outs) if emit else outs[0]


def _ffn_act_body(g_ref, u_ref, st_ref, w_ref, cb_ref, y_ref, so_ref, full_ref, *, ts):
    ti = pl.program_id(2)
    lo = SUBLANES - (K_FFN - 1)
    for q in range(g_ref.shape[0]):
        @pl.when(ti == 0)
        def _():
            full_ref[0:lo, :] = jnp.zeros((lo, full_ref.shape[1]), F32)
            full_ref[lo:SUBLANES, :] = st_ref[q]

        full_ref[SUBLANES:SUBLANES + ts, :] = g_ref[q]
        acc = jnp.zeros((ts, full_ref.shape[1]), F32)
        for j in range(K_FFN):
            acc = acc + full_ref[lo + j:lo + j + ts, :] * w_ref[j:j + 1, :]
        y_ref[q] = (_silu(acc + cb_ref[...]) * u_ref[q]).astype(BF16)
        so_ref[q] = full_ref[lo + ts:SUBLANES + ts, :]
        if ts >= SUBLANES:
            full_ref[0:SUBLANES, :] = full_ref[ts:ts + SUBLANES, :]


def ffn_act(h3, state, conv_w, conv_b):
    b, s, f2 = h3.shape
    f = f2 // 2
    ts = _pick(s, (512, 256, 128)) if s >= 128 else s
    tc = _pick(f, (512, 256, 128)) if ts >= 128 else f
    nc = f // tc
    nt = s // ts
    assert nt == 1 or ts >= SUBLANES
    gb = _pick(b, (_SEQS_PER_STEP, 1)) if ts < LANES else 1
    return pl.pallas_call(
        functools.partial(_ffn_act_body, ts=ts),
        grid=(b // gb, nc, nt),
        in_specs=[pl.BlockSpec((gb, ts, tc), lambda i, c, t: (i, t, c)),
                  pl.BlockSpec((gb, ts, tc), lambda i, c, t: (i, t, nc + c)),
                  pl.BlockSpec((gb, K_FFN - 1, tc), lambda i, c, t: (i, 0, c)),
                  pl.BlockSpec((K_FFN, tc), lambda i, c, t: (0, c)),
                  pl.BlockSpec((1, tc), lambda i, c, t: (0, c))],
        out_specs=[pl.BlockSpec((gb, ts, tc), lambda i, c, t: (i, t, c)),
                   pl.BlockSpec((gb, K_FFN - 1, tc), lambda i, c, t: (i, 0, c))],
        out_shape=[jax.ShapeDtypeStruct((b, s, f), BF16),
                   jax.ShapeDtypeStruct((b, K_FFN - 1, f), F32)],
        scratch_shapes=[pltpu.VMEM((SUBLANES + ts, tc), F32)],
        compiler_params=_cparams(("arbitrary", "arbitrary", "arbitrary")),
        name="ffn_act",
    )(h3, h3, state, conv_w, conv_b.reshape(1, f))


def _layer(x, batch, seq, pos, attn_fn, ret_fn, conv0, ffn0, lw, mw, fuse_ffn):
    n, d = x.shape
    a_cols = H_A * (4 * DK_A + DV_A)
    b_cols = H_B * (2 * DK_B + 2 * DV_B)
    c_conv = lw['conv_w'].shape[1]
    emit = isinstance(mw['w_in'], tuple)
    copies = {}

    def take(name, res):
        if not emit:
            return res
        copies[name] = res[1]
        return res[0]

    proj = take('w_in', norm_matmul(x, lw['norm1_g'], mw['w_in']))
    a_in, kf, vf = attn_fn(proj)
    cos, sin = _rotary_tables(pos)
    b_in, ret_new = ret_fn(proj, cos, sin, a_cols)
    c_in, conv_new = conv_module(proj.reshape(batch, seq, -1), conv0, lw['conv_w'], lw['conv_b'],
                                 lw['conv_ln_g'], lw['conv_ln_b'], a_cols + b_cols)
    c_in = c_in.reshape(n, c_conv)
    merged = branch_merge(a_in, b_in, c_in, mw['w_o_a'], mw['w_o_b'], mw['w_o_c'], proj, lw['b_gate'],
                          a_cols + b_cols + 2 * c_conv)
    if emit:
        m, copies['w_o_a'], copies['w_o_b'], copies['w_o_c'] = merged
    else:
        m = merged
    x = take('w_out', matmul_residual(m, mw['w_out'], x))
    if fuse_ffn:
        act, ffn_new = ffn_up_prompt(x, lw['norm2_g'], mw['w_ffn_in'], ffn0, lw['ffn_conv_w'], lw['ffn_conv_b'],
                                     batch, seq)
    else:
        h2 = take('w_ffn_in', norm_matmul(x, lw['norm2_g'], mw['w_ffn_in']))
        act, ffn_new = ffn_act(h2.reshape(batch, seq, -1), ffn0, lw['ffn_conv_w'], lw['ffn_conv_b'])
        act = act.reshape(n, -1)
    x = take('w_ffn_down', matmul_residual(act, mw['w_ffn_down'], x))
    states = (kf.reshape(batch, seq, H_A, 2 * DK_A), vf.reshape(batch, seq, H_A, DV_A), ret_new, conv_new, ffn_new)
    return x, states, copies


def kernel(x_prompt, x_sample, cache_k, cache_v, state_ret, state_conv, state_ffn, page_table, rel_bias, norm1_g, w_in, b_gate, q_norm_g, k_norm_g, lam_vec, subln_g, w_o_a, ret_norm_g, w_o_b, conv_w, conv_b, conv_ln_g, conv_ln_b, w_o_c, w_out, norm2_g, w_ffn_in, ffn_conv_w, ffn_conv_b, w_ffn_down):
    bp, sp, d = x_prompt.shape
    db, t_new, _ = x_sample.shape
    depth = w_in.shape[0]
    past = page_table.shape[1] * cache_k.shape[2]
    pos_p = jnp.arange(sp)
    pos_s = jnp.tile(past + jnp.arange(t_new), db)
    yp = x_prompt.reshape(bp * sp, d)
    ys = x_sample.reshape(db * t_new, d)
    c_conv = conv_w.shape[2]
    d_ff = ffn_conv_w.shape[2]
    zero_conv = jnp.zeros((bp, K_CONV - 1, c_conv), F32)
    zero_ffn = jnp.zeros((bp, K_FFN - 1, d_ff), F32)
    sts_p, sts_s = [], []
    stacked = {'w_in': w_in, 'w_o_a': w_o_a, 'w_o_b': w_o_b, 'w_o_c': w_o_c, 'w_out': w_out,
               'w_ffn_in': w_ffn_in, 'w_ffn_down': w_ffn_down}
    for l in range(depth):
        lw = {'norm1_g': norm1_g[l], 'b_gate': b_gate[l], 'q_norm_g': q_norm_g[l], 'k_norm_g': k_norm_g[l],
              'conv_w': conv_w[l], 'conv_b': conv_b[l], 'conv_ln_g': conv_ln_g[l], 'conv_ln_b': conv_ln_b[l],
              'norm2_g': norm2_g[l], 'ffn_conv_w': ffn_conv_w[l], 'ffn_conv_b': ffn_conv_b[l]}
        lam_init = 0.8 - 0.6 * math.exp(-0.3 * l)

        def s_attn(proj, l=l, lam_init=lam_init, lw=lw):
            qn, kf, vf = attn_prep_sample(proj, lw['q_norm_g'], lw['k_norm_g'])
            return sample_attn(qn, kf, vf, cache_k, cache_v, l, page_table, rel_bias, lam_vec[l], subln_g[l],
                               lam_init, db, t_new), kf, vf

        def s_ret(proj, cos, sin, col0, l=l):
            return retention_sample(proj, cos, sin, ret_norm_g[l], state_ret[l], db, t_new, col0)

        ys, st, copies = _layer(ys, db, t_new, pos_s, s_attn, s_ret, state_conv[l], state_ffn[l], lw,
                                {name: (arr, l) for name, arr in stacked.items()}, False)
        sts_s.append(st)

        def p_attn(proj, l=l, lam_init=lam_init, lw=lw):
            t = _pick(sp, (_ATTN_TILE, LANES))
            qt, kf, kb, vf, vt = attn_prep_prompt(proj, lw['q_norm_g'], lw['k_norm_g'], bp, sp, t)
            return prompt_attn(qt, kb, vt, rel_bias, lam_vec[l], subln_g[l], lam_init, bp, sp, t), kf, vf

        def p_ret(proj, cos, sin, col0, l=l):
            return retention_prompt(proj, cos, sin, ret_norm_g[l], bp, sp, col0)

        yp, st, _ = _layer(yp, bp, sp, pos_p, p_attn, p_ret, zero_conv, zero_ffn, lw, copies, True)
        sts_p.append(st)

    def stk(sts, i):
        return jnp.stack([st[i] for st in sts], axis=0)

    return (yp.reshape(bp, sp, d), ys.reshape(db, t_new, d),
            stk(sts_p, 0), stk(sts_p, 1), stk(sts_p, 2), stk(sts_p, 3), stk(sts_p, 4),
            stk(sts_s, 0), stk(sts_s, 1), stk(sts_s, 2), stk(sts_s, 3), stk(sts_s, 4))
```

```python
import functools
import math

import numpy as np
import jax
import jax.numpy as jnp
from jax import lax
from jax.experimental import pallas as pl
from jax.experimental.pallas import tpu as pltpu

F32 = jnp.float32
BF16 = jnp.bfloat16

H_A = 8
DK_A = 64
DV_A = 128
H_B = 8
DK_B = 64
DV_B = 128
K_CONV = 31
K_FFN = 3
N_BUCKETS = 32
MAX_DIST = 128
EPS = 1e-6
NEG = -1e30

LANES = 128
SUBLANES = 8
VMEM_LIMIT = 56 * 1024 * 1024

_NT = (((1,), (1,)), ((), ()))
_TN = (((0,), (0,)), ((), ()))


def _cparams(sem):
    return pltpu.CompilerParams(dimension_semantics=sem, vmem_limit_bytes=VMEM_LIMIT)


def _pick(n, cands):
    for c in cands:
        if n % c == 0:
            return c
    return n


def _t5_bucket_np(d):
    max_exact = N_BUCKETS // 2
    d = np.maximum(d, 0)
    df = np.maximum(d, 1).astype(np.float32)
    large = max_exact + (np.log(df / np.float32(max_exact)) / np.float32(math.log(MAX_DIST / max_exact))
                         * np.float32(N_BUCKETS - max_exact)).astype(np.int32)
    return np.where(d < max_exact, d, np.minimum(large, N_BUCKETS - 1)).astype(np.int32)


def _bias_from_buckets(idx, rb_ref, h):
    acc = jnp.zeros(idx.shape, F32)
    for b in range(N_BUCKETS):
        acc = jnp.where(idx == b, rb_ref[b, h], acc)
    return acc


def _lam(lv_ref, lam_init):
    lv = lv_ref[...]
    a = jnp.sum(lv[0:1] * lv[1:2], axis=-1, keepdims=True)
    b = jnp.sum(lv[2:3] * lv[3:4], axis=-1, keepdims=True)
    return jnp.exp(a) - jnp.exp(b) + lam_init


def _silu(x):
    return x * jax.nn.sigmoid(x)


def _weight_form(w):
    if isinstance(w, tuple):
        arr, l = w
        return arr, True, l, arr.shape[1], arr.shape[2]
    return w, False, None, w.shape[0], w.shape[1]


def _norm_mm_body(x_ref, g_ref, w_ref, o_ref, *rest, emit):
    h_ref = rest[-1]

    @pl.when(pl.program_id(1) == 0)
    def _():
        x = x_ref[...]
        ms = jnp.mean(x * x, axis=-1, keepdims=True)
        h_ref[...] = ((x * lax.rsqrt(ms + EPS)) * g_ref[...]).astype(BF16)

    w = w_ref[...]
    if emit:
        w = w.astype(BF16)
        rest[0][...] = w
    o_ref[...] = jnp.dot(h_ref[...], w, preferred_element_type=F32)


def norm_matmul(x, g, w):
    n, d = x.shape
    w_arr, emit, l, _, c = _weight_form(w)
    tm = _pick(n, (1024, 512, 256, 128))
    tn = _pick(c, (1024, 512, 256, 128) if emit else (2048, 1024, 512, 256, 128))
    assert not emit or n == tm
    w_spec = (pl.BlockSpec((None, d, tn), lambda i, j: (l, 0, j)) if emit
              else pl.BlockSpec((d, tn), lambda i, j: (0, j)))
    out_specs = [pl.BlockSpec((tm, tn), lambda i, j: (i, j))]
    out_shape = [jax.ShapeDtypeStruct((n, c), F32)]
    if emit:
        out_specs.append(pl.BlockSpec((d, tn), lambda i, j: (0, j)))
        out_shape.append(jax.ShapeDtypeStruct((d, c), BF16))
    outs = pl.pallas_call(
        functools.partial(_norm_mm_body, emit=emit),
        grid=(n // tm, c // tn),
        in_specs=[pl.BlockSpec((tm, d), lambda i, j: (i, 0)),
                  pl.BlockSpec((1, d), lambda i, j: (0, 0)),
                  w_spec],
        out_specs=out_specs,
        out_shape=out_shape,
        scratch_shapes=[pltpu.VMEM((tm, d), BF16)],
        compiler_params=_cparams(("parallel", "arbitrary")),
        name="norm_matmul",
    )(x, g.reshape(1, d), w_arr)
    return tuple(outs) if emit else outs[0]


def _ffn_up_body(x_ref, g_ref, wg_ref, wu_ref, cw_ref, cb_ref, st_ref, y_ref, so_ref, h_ref, gbuf_ref, carry_ref,
                 *, tiles_per_seq):
    i = pl.program_id(0)
    j = pl.program_id(1)
    tm = x_ref.shape[0]
    lo = SUBLANES - (K_FFN - 1)

    @pl.when(j == 0)
    def _():
        x = x_ref[...]
        ms = jnp.mean(x * x, axis=-1, keepdims=True)
        h_ref[...] = ((x * lax.rsqrt(ms + EPS)) * g_ref[...]).astype(BF16)

    h = h_ref[...]
    first = (i % tiles_per_seq) == 0

    @pl.when(first)
    def _():
        gbuf_ref[0:lo, :] = jnp.zeros((lo, gbuf_ref.shape[1]), F32)
        gbuf_ref[lo:SUBLANES, :] = st_ref[0]

    @pl.when(jnp.logical_not(first))
    def _():
        gbuf_ref[0:SUBLANES, :] = carry_ref[j]

    gbuf_ref[SUBLANES:SUBLANES + tm, :] = jnp.dot(h, wg_ref[...], preferred_element_type=F32)
    acc = jnp.zeros((tm, gbuf_ref.shape[1]), F32)
    for k in range(K_FFN):
        acc = acc + gbuf_ref[lo + k:lo + k + tm, :] * cw_ref[k:k + 1, :]
    up = jnp.dot(h, wu_ref[...], preferred_element_type=F32)
    y_ref[...] = (_silu(acc + cb_ref[...]) * up).astype(BF16)
    carry_ref[j] = gbuf_ref[tm:tm + SUBLANES, :]
    so_ref[0] = gbuf_ref[lo + tm:SUBLANES + tm, :]


def ffn_up_prompt(x, g, w, state, conv_w, conv_b, batch, seq):
    n, d = x.shape
    f = w.shape[1] // 2
    tm = _pick(seq, (1024, 512, 256, 128))
    tn = _pick(f, (512, 256, 128))
    nj = f // tn
    tps = seq // tm
    act, tails = pl.pallas_call(
        functools.partial(_ffn_up_body, tiles_per_seq=tps),
        grid=(n // tm, nj),
        in_specs=[pl.BlockSpec((tm, d), lambda i, j: (i, 0)),
                  pl.BlockSpec((1, d), lambda i, j: (0, 0)),
                  pl.BlockSpec((d, tn), lambda i, j: (0, j)),
                  pl.BlockSpec((d, tn), lambda i, j: (0, nj + j)),
                  pl.BlockSpec((K_FFN, tn), lambda i, j: (0, j)),
                  pl.BlockSpec((1, tn), lambda i, j: (0, j)),
                  pl.BlockSpec((1, K_FFN - 1, tn), lambda i, j: (i // tps, 0, j))],
        out_specs=[pl.BlockSpec((tm, tn), lambda i, j: (i, j)),
                   pl.BlockSpec((1, K_FFN - 1, tn), lambda i, j: (i, 0, j))],
        out_shape=[jax.ShapeDtypeStruct((n, f), BF16),
                   jax.ShapeDtypeStruct((n // tm, K_FFN - 1, f), F32)],
        scratch_shapes=[pltpu.VMEM((tm, d), BF16),
                        pltpu.VMEM((SUBLANES + tm, tn), F32),
                        pltpu.VMEM((nj, SUBLANES, tn), F32)],
        compiler_params=_cparams(("arbitrary", "arbitrary")),
        name="ffn_up_prompt",
    )(x, g.reshape(1, d), w, w, conv_w, conv_b.reshape(1, f), state)
    return act, tails[tps - 1::tps]


def _mm_res_body(a_ref, w_ref, r_ref, o_ref, *rest, nk, emit):
    acc_ref = rest[-1]
    k = pl.program_id(2)
    w = w_ref[...]
    if emit:
        w = w.astype(BF16)
        rest[0][...] = w
    part = jnp.dot(a_ref[...], w, preferred_element_type=F32)
    if nk == 1:
        o_ref[...] = r_ref[...] + part
    else:
        @pl.when(k == 0)
        def _():
            acc_ref[...] = part

        @pl.when(jnp.logical_and(k > 0, k < nk - 1))
        def _():
            acc_ref[...] += part

        @pl.when(k == nk - 1)
        def _():
            o_ref[...] = r_ref[...] + (acc_ref[...] + part)


def matmul_residual(a, w, res):
    n = a.shape[0]
    w_arr, emit, l, kd, c = _weight_form(w)
    tk = kd if kd <= 2048 else _pick(kd, (2816, 2048, 1024, 512))
    nk = kd // tk
    if nk == 1 and not emit and c <= 2048:
        tm, tn = _pick(n, (512, 256, 128)), c
    else:
        tm, tn = _pick(n, (1024, 512, 256, 128)), _pick(c, (1024, 512, 256, 128))
    assert not emit or n == tm
    w_spec = (pl.BlockSpec((None, tk, tn), lambda i, j, k: (l, k, j)) if emit
              else pl.BlockSpec((tk, tn), lambda i, j, k: (k, j)))
    out_specs = [pl.BlockSpec((tm, tn), lambda i, j, k: (i, j))]
    out_shape = [jax.ShapeDtypeStruct((n, c), F32)]
    if emit:
        out_specs.append(pl.BlockSpec((tk, tn), lambda i, j, k: (k, j)))
        out_shape.append(jax.ShapeDtypeStruct((kd, c), BF16))
    outs = pl.pallas_call(
        functools.partial(_mm_res_body, nk=nk, emit=emit),
        grid=(n // tm, c // tn, nk),
        in_specs=[pl.BlockSpec((tm, tk), lambda i, j, k: (i, k)),
                  w_spec,
                  pl.BlockSpec((tm, tn), lambda i, j, k: (i, j))],
        out_specs=out_specs,
        out_shape=out_shape,
        scratch_shapes=[pltpu.VMEM((tm, tn), F32)],
        compiler_params=_cparams(("parallel", "parallel", "arbitrary")),
        name="matmul_residual",
    )(a, w_arr, res)
    return tuple(outs) if emit else outs[0]


def _group_rms(x, gain, gm):
    s = x * x
    hi = s.astype(BF16)
    lo = (s - hi.astype(F32)).astype(BF16)
    parts = []
    for c in range(x.shape[1] // LANES):
        sl = slice(c * LANES, (c + 1) * LANES)
        parts.append(jnp.dot(hi[:, sl], gm, preferred_element_type=F32)
                     + jnp.dot(lo[:, sl], gm, preferred_element_type=F32))
    ms = jnp.concatenate(parts, axis=-1) * (1.0 / DK_A)
    return (x * lax.rsqrt(ms + EPS)) * gain


_LOG2E = math.log2(math.e)
_Q_SCALE = DK_A ** -0.5 * _LOG2E
_DVX = DV_A + 16


def _attn_prep_sample_body(q_ref, k_ref, v_ref, qg_ref, kg_ref, gm_ref, qo_ref, kf_ref, vf_ref):
    gm = gm_ref[...]
    qo_ref[...] = _group_rms(q_ref[...], qg_ref[...], gm) * _Q_SCALE
    kf_ref[...] = _group_rms(k_ref[...], kg_ref[...], gm)
    vf_ref[...] = v_ref[...]


def _attn_prep_prompt_body(q_ref, k_ref, v_ref, qg_ref, kg_ref, gm_ref, qt_ref, kf_ref, kb_ref, vf_ref, vt_ref):
    gm = gm_ref[...]
    qn = _group_rms(q_ref[...], qg_ref[...], gm) * _Q_SCALE
    kn = _group_rms(k_ref[...], kg_ref[...], gm)
    kf_ref[...] = kn
    kb_ref[...] = kn.astype(BF16)
    v = v_ref[...]
    vf_ref[...] = v
    for h in range(H_A):
        hs = slice(h * LANES, (h + 1) * LANES)
        qt_ref[0, h, 0] = qn[:, hs].T.astype(BF16)
        vt_ref[0, h, 0, 0:DV_A, :] = v[:, hs].T.astype(BF16)
        vt_ref[0, h, 0, DV_A:_DVX, :] = jnp.ones((_DVX - DV_A, v.shape[0]), BF16)


def _attn_prep_consts(q_gain, k_gain):
    w = H_A * 2 * DK_A
    gm = np.kron(np.eye(LANES // DK_A, dtype=np.float32), np.ones((DK_A, DK_A), np.float32))
    return (jnp.tile(q_gain, w // DK_A).reshape(1, w), jnp.tile(k_gain, w // DK_A).reshape(1, w),
            jnp.asarray(gm, BF16))


def attn_prep_sample(proj, q_gain, k_gain):
    n = proj.shape[0]
    w = H_A * 2 * DK_A
    tm = _pick(n, (512, 256, 128))
    col = lambda c: pl.BlockSpec((tm, w), lambda i, c=c: (i, c))
    cst = lambda shape: pl.BlockSpec(shape, lambda i: (0, 0))
    out = lambda: pl.BlockSpec((tm, w), lambda i: (i, 0))
    return pl.pallas_call(
        _attn_prep_sample_body,
        grid=(n // tm,),
        in_specs=[col(0), col(1), col(2), cst((1, w)), cst((1, w)), cst((LANES, LANES))],
        out_specs=[out(), out(), out()],
        out_shape=[jax.ShapeDtypeStruct((n, w), F32)] * 3,
        compiler_params=_cparams(("parallel",)),
        name="attn_prep_sample",
    )(proj, proj, proj, *_attn_prep_consts(q_gain, k_gain))


def attn_prep_prompt(proj, q_gain, k_gain, batch, seq, t):
    n = batch * seq
    w = H_A * 2 * DK_A
    nt = seq // t
    col = lambda c: pl.BlockSpec((t, w), lambda b, i, c=c: (b * nt + i, c))
    cst = lambda shape: pl.BlockSpec(shape, lambda b, i: (0, 0))
    out = lambda: pl.BlockSpec((t, w), lambda b, i: (b * nt + i, 0))
    tr = lambda rows=LANES: pl.BlockSpec((1, H_A, 1, rows, t), lambda b, i: (b, 0, i, 0, 0))
    tr_shape = lambda rows=LANES: jax.ShapeDtypeStruct((batch, H_A, nt, rows, t), BF16)
    return pl.pallas_call(
        _attn_prep_prompt_body,
        grid=(batch, nt),
        in_specs=[col(0), col(1), col(2), cst((1, w)), cst((1, w)), cst((LANES, LANES))],
        out_specs=[tr(), out(), out(), out(), tr(_DVX)],
        out_shape=[tr_shape(), jax.ShapeDtypeStruct((n, w), F32), jax.ShapeDtypeStruct((n, w), BF16),
                   jax.ShapeDtypeStruct((n, w), F32), tr_shape(_DVX)],
        compiler_params=_cparams(("parallel", "parallel")),
        name="attn_prep_prompt",
    )(proj, proj, proj, *_attn_prep_consts(q_gain, k_gain))


def _prompt_attn_body(rb_ref, ib_ref, lv_ref, sg_ref, qt_ref, k_ref, vt_ref, o_ref,
                      bias_ref, m_ref, acc_ref, *, t, lam_init, n_hg):
    hg = pl.program_id(1)
    qi = pl.program_id(2)
    heads = range(_ATTN_HEADS)
    build = qi == 0
    if n_hg == 1:
        build = jnp.logical_and(build, pl.program_id(0) == 0)

    @pl.when(build)
    def _():
        for g in heads:
            h = hg * _ATTN_HEADS + g
            far_bias = rb_ref[N_BUCKETS - 1, h]
            for o in range(2):
                idx = ib_ref[o]
                bias_ref[g, o] = jnp.where(idx >= 0, (_bias_from_buckets(idx, rb_ref, h) - far_bias) * _LOG2E, NEG)

    q2t = []
    for g in heads:
        qt = qt_ref[0, g, 0]
        row = lax.broadcasted_iota(jnp.int32, qt.shape, 0)
        zero = jnp.zeros_like(qt)
        q2t.append(jnp.concatenate([jnp.where(row < DK_A, qt, zero), jnp.where(row >= DK_A, qt, zero)], axis=1))

    m_ref[...] = jnp.full(m_ref.shape, NEG, F32)
    acc_ref[...] = jnp.zeros(acc_ref.shape, F32)

    def tile(kj, o):
        off = pl.multiple_of(kj * t, t)
        s = [jnp.dot(k_ref[0, pl.ds(off, t), g * LANES:(g + 1) * LANES], q2t[g], preferred_element_type=F32)
             for g in heads]
        if o is not None:
            s = [s[g] + bias_ref[g, o] for g in heads]
        m_old = [m_ref[g] for g in heads]
        m_new = [jnp.maximum(m_old[g], jnp.max(s[g], axis=0, keepdims=True)) for g in heads]
        alpha = [jnp.exp2(m_old[g] - m_new[g]) for g in heads]
        p = [jnp.exp2(s[g] - m_new[g]).astype(BF16) for g in heads]
        for g in heads:
            m_ref[g] = m_new[g]
        pv = [jnp.dot(vt_ref[0, g, kj], p[g], preferred_element_type=F32) for g in heads]
        for g in heads:
            acc_ref[g] = alpha[g] * acc_ref[g] + pv[g]

    def far(kj, carry):
        tile(kj, None)
        return carry

    lax.fori_loop(0, jnp.maximum(qi - 1, 0), far, 0)

    @pl.when(qi >= 1)
    def _():
        tile(qi - 1, 1)

    tile(qi, 0)

    lam = _lam(lv_ref, lam_init)
    for g in heads:
        o2t = acc_ref[g, 0:DV_A, :] / acc_ref[g, DV_A:DV_A + 1, :]
        o = (o2t[:, 0:t] - lam * o2t[:, t:2 * t]).T
        ms = jnp.mean(o * o, axis=-1, keepdims=True)
        y = ((o * lax.rsqrt(ms + EPS)) * sg_ref[...]) * (1.0 - lam_init)
        o_ref[0, :, g * LANES:(g + 1) * LANES] = y.astype(BF16)


_ATTN_TILE = 256
_ATTN_HEADS = 8


def prompt_attn(qt, kb, vt, rel_bias, lam_vec, subln_g, lam_init, batch, seq, t):
    w = H_A * DV_A
    assert t >= MAX_DIST and seq % t == 0
    nq = seq // t
    hgs = _ATTN_HEADS
    key = np.arange(t)[:, None]
    qry = (np.arange(2 * t) % t)[None, :]
    d0 = qry - key
    ib = jnp.asarray(np.stack([np.where(d0 >= 0, _t5_bucket_np(d0), -1), _t5_bucket_np(d0 + t)]), jnp.int32)
    assert int(_t5_bucket_np(np.array([t + 1]))[0]) == N_BUCKETS - 1
    out = pl.pallas_call(
        functools.partial(_prompt_attn_body, t=t, lam_init=lam_init, n_hg=H_A // hgs),
        grid=(batch, H_A // hgs, nq),
        in_specs=[pl.BlockSpec(memory_space=pltpu.SMEM),
                  pl.BlockSpec((2, t, 2 * t), lambda b, h, i: (0, 0, 0)),
                  pl.BlockSpec((4, DK_A), lambda b, h, i: (0, 0)),
                  pl.BlockSpec((1, DV_A), lambda b, h, i: (0, 0)),
                  pl.BlockSpec((1, hgs, 1, LANES, t), lambda b, h, i: (b, h, i, 0, 0)),
                  pl.BlockSpec((1, seq, hgs * LANES), lambda b, h, i: (b, 0, h)),
                  pl.BlockSpec((1, hgs, nq, _DVX, t), lambda b, h, i: (b, h, 0, 0, 0))],
        out_specs=pl.BlockSpec((1, t, hgs * LANES), lambda b, h, i: (b, i, h)),
        out_shape=jax.ShapeDtypeStruct((batch, seq, w), BF16),
        scratch_shapes=[pltpu.VMEM((hgs, 2, t, 2 * t), F32),
                        pltpu.VMEM((hgs, 1, 2 * t), F32),
                        pltpu.VMEM((hgs, _DVX, 2 * t), F32)],
        compiler_params=_cparams(("arbitrary", "arbitrary", "arbitrary")),
        name="prompt_attn",
    )(rel_bias, ib, lam_vec, subln_g.reshape(1, DV_A), qt, kb.reshape(batch, seq, w), vt)
    return out.reshape(batch * seq, w)


_RH = SUBLANES


_KPAD = LANES // H_A


def _sample_attn_body(pt_ref, rb_ref, ibp_ref, ibn_ref, lv_ref, sg_ref, q_ref, kn_ref, vn_ref, *rest,
                      pps, nsteps, lam_init):
    k_refs = rest[0:pps]
    v_refs = rest[pps:2 * pps]
    o_ref = rest[2 * pps]
    biasp_ref, biasn_ref, m_ref, l_ref, acc_ref = rest[2 * pps + 1:]
    b = pl.program_id(0)
    s_id = pl.program_id(1)

    t_new = o_ref.shape[1]

    @pl.when(jnp.logical_and(b == 0, s_id == 0))
    def _():
        for h in range(H_A):
            sl = slice(h * _RH, (h + 1) * _RH)
            far_bias = rb_ref[N_BUCKETS - 1, h]
            for o in range(2):
                idx = ibp_ref[o, sl, :]
                biasp_ref[o, sl, :] = jnp.where(idx >= 0, (_bias_from_buckets(idx, rb_ref, h) - far_bias) * _LOG2E,
                                                NEG)
            idx = ibn_ref[sl, :]
            biasn_ref[sl, :] = jnp.where(idx >= 0, (_bias_from_buckets(idx, rb_ref, h) - far_bias) * _LOG2E, NEG)

    @pl.when(s_id == 0)
    def _():
        m_ref[...] = jnp.full(m_ref.shape, NEG, F32)
        l_ref[...] = jnp.zeros(l_ref.shape, F32)
        acc_ref[...] = jnp.zeros(acc_ref.shape, F32)

    q = q_ref[0]
    row = lax.broadcasted_iota(jnp.int32, (_RH, LANES), 0)
    lane = lax.broadcasted_iota(jnp.int32, (_RH, LANES), 1)
    keep = (row < t_new) == (lane < DK_A)
    q2 = jnp.concatenate([jnp.where(keep, q[:, h * LANES:(h + 1) * LANES], 0.0) for h in range(H_A)],
                         axis=0).astype(BF16)

    def attend(kt, vt, bias):
        s = lax.dot_general(q2, kt, _NT, preferred_element_type=F32) + bias
        m_old = m_ref[...]
        m_new = jnp.maximum(m_old, jnp.max(s, axis=-1, keepdims=True))
        alpha = jnp.exp2(m_old - m_new)
        p = jnp.exp2(s - m_new)
        l_ref[...] = alpha * l_ref[...] + jnp.sum(p, axis=-1, keepdims=True)
        acc_ref[...] = alpha * acc_ref[...] + jnp.dot(p.astype(BF16), vt, preferred_element_type=F32)
        m_ref[...] = m_new

    kt = jnp.concatenate([r[0, 0].astype(BF16) for r in k_refs], axis=0)
    vt = jnp.concatenate([r[0, 0].astype(BF16) for r in v_refs], axis=0)
    last = s_id == nsteps - 1
    bias = jnp.concatenate([biasp_ref[0]] * (pps - 1) + [biasp_ref[jnp.where(last, 1, 0)]], axis=1)
    attend(kt, vt, bias)

    @pl.when(last)
    def _():
        attend(kn_ref[0].astype(BF16), vn_ref[0].astype(BF16), biasn_ref[...])
        o2 = acc_ref[...] / l_ref[...]
        lam = _lam(lv_ref, lam_init)
        for h in range(H_A):
            x = o2[h * _RH:(h + 1) * _RH]
            o = x - lam * pltpu.roll(x, t_new, 0)
            ms = jnp.mean(o * o, axis=-1, keepdims=True)
            y = ((o * lax.rsqrt(ms + EPS)) * sg_ref[...]) * (1.0 - lam_init)
            o_ref[0, :, h * LANES:(h + 1) * LANES] = y[0:t_new]


def sample_attn(qn, kf, vf, cache_k, cache_v, layer, page_table, rel_bias, lam_vec, subln_g, lam_init, db, t_new):
    w = H_A * DV_A
    depth, n_pool, page = cache_k.shape[:3]
    n_pages = page_table.shape[1]
    assert 2 * t_new == _RH and t_new <= _KPAD and page >= MAX_DIST
    pps = _pick(n_pages, (16, 8, 4, 2, 1))
    nsteps = n_pages // pps
    rows = H_A * _RH
    kw = page * H_A
    rh = (np.arange(rows) // _RH)[:, None]
    rt = (np.arange(rows) % t_new)[:, None]
    ck_key, ck_head = (np.arange(kw) // H_A)[None, :], (np.arange(kw) % H_A)[None, :]
    ib_far = np.where(rh == ck_head, N_BUCKETS - 1, -1)
    ib_last = np.where(rh == ck_head, _t5_bucket_np(page + rt - ck_key), -1)
    nk_key, nk_head = (np.arange(_KPAD * H_A) // H_A)[None, :], (np.arange(_KPAD * H_A) % H_A)[None, :]
    d_new = rt - nk_key
    ib_new = np.where((rh == nk_head) & (d_new >= 0) & (nk_key < t_new), _t5_bucket_np(d_new), -1)
    ibp = jnp.asarray(np.stack([ib_far, ib_last]), jnp.int32)
    ibn = jnp.asarray(ib_new, jnp.int32)
    assert int(_t5_bucket_np(np.array([page + 1]))[0]) == N_BUCKETS - 1

    q3 = qn.reshape(db, t_new, w)
    q_pad = jnp.concatenate([q3, q3], axis=1)

    def new_rows(x):
        return jnp.pad(x.reshape(db, t_new * H_A, LANES), ((0, 0), (0, (_KPAD - t_new) * H_A), (0, 0)))

    ck = cache_k.reshape(depth, n_pool, kw, LANES)
    cv = cache_v.reshape(depth, n_pool, kw, LANES)
    new_spec = pl.BlockSpec((1, _KPAD * H_A, LANES), lambda b, s, pt: (b, 0, 0))

    def page_spec(i):
        return pl.BlockSpec((1, 1, kw, LANES), lambda b, s, pt, i=i: (layer, pt[b, s * pps + i], 0, 0))

    grid_spec = pltpu.PrefetchScalarGridSpec(
        num_scalar_prefetch=1,
        grid=(db, nsteps),
        in_specs=[pl.BlockSpec(memory_space=pltpu.SMEM),
                  pl.BlockSpec((2, rows, kw), lambda b, s, pt: (0, 0, 0)),
                  pl.BlockSpec((rows, _KPAD * H_A), lambda b, s, pt: (0, 0)),
                  pl.BlockSpec((4, DK_A), lambda b, s, pt: (0, 0)),
                  pl.BlockSpec((1, DV_A), lambda b, s, pt: (0, 0)),
                  pl.BlockSpec((1, _RH, w), lambda b, s, pt: (b, 0, 0)),
                  new_spec, new_spec]
                 + [page_spec(i) for i in range(pps)] + [page_spec(i) for i in range(pps)],
        out_specs=pl.BlockSpec((1, t_new, w), lambda b, s, pt: (b, 0, 0)),
        scratch_shapes=[pltpu.VMEM((2, rows, kw), F32),
                        pltpu.VMEM((rows, _KPAD * H_A), F32),
                        pltpu.VMEM((rows, 1), F32),
                        pltpu.VMEM((rows, 1), F32),
                        pltpu.VMEM((rows, DV_A), F32)],
    )
    out = pl.pallas_call(
        functools.partial(_sample_attn_body, pps=pps, nsteps=nsteps, lam_init=lam_init),
        grid_spec=grid_spec,
        out_shape=jax.ShapeDtypeStruct((db, t_new, w), F32),
        compiler_params=_cparams(("arbitrary", "arbitrary")),
        name="sample_attn",
    )(page_table, rel_bias, ibp, ibn, lam_vec, subln_g.reshape(1, DV_A), q_pad, new_rows(kf), new_rows(vf),
      *([ck] * pps), *([cv] * pps))
    return out.reshape(db * t_new, w).astype(BF16)


def _log_gamma(h):
    return float(np.log1p(-np.exp2(np.float32(-5.0 - h), dtype=np.float32), dtype=np.float32))


def _swap_halves(x):
    n = x.shape[1]
    half = DK_B // 2
    lane = lax.broadcasted_iota(jnp.int32, x.shape, 1)
    return jnp.where((lane % DK_B) < half, pltpu.roll(x, n - half, 1), pltpu.roll(x, half, 1))


def _rotary(x, cos, sin):
    return x * cos + _swap_halves(x) * sin


def _ret_tile(q, k, v, c):
    r = q.shape[0]
    shift = int(math.log2(c))
    assert 1 << shift == c
    ri = lax.broadcasted_iota(jnp.int32, (r, r), 0)
    ci = lax.broadcasted_iota(jnp.int32, (r, r), 1)
    same = (ri >> shift) == (ci >> shift)
    diff = ((ri & (c - 1)) - (ci & (c - 1))).astype(F32)
    valid = jnp.logical_and(same, diff >= 0.0)
    ti = (lax.broadcasted_iota(jnp.int32, (r, 1), 0) & (c - 1)).astype(F32)
    lane = lax.broadcasted_iota(jnp.int32, (r, LANES), 1)
    first = lane < DK_B
    o_inner, q_m, dec_q, k_dec = [], [], [], []
    for p in range(H_B // 2):
        ps = slice(p * LANES, (p + 1) * LANES)
        qp, kp = q[:, ps], k[:, ps]
        kpb = kp.astype(BF16)
        lg0, lg1 = _log_gamma(2 * p), _log_gamma(2 * p + 1)
        k_dec.append((kp * jnp.where(first, jnp.exp((c - 1.0 - ti) * lg0), jnp.exp((c - 1.0 - ti) * lg1))).astype(BF16))
        for hh in range(2):
            h = 2 * p + hh
            lg = lg1 if hh else lg0
            qm = jnp.where(first if hh == 0 else jnp.logical_not(first), qp, 0.0).astype(BF16)
            dmat = jnp.where(valid, jnp.exp(jnp.maximum(diff, 0.0) * lg), 0.0)
            inner = lax.dot_general(qm, kpb, _NT, preferred_element_type=F32) * dmat
            vh = v[:, h * DV_B:(h + 1) * DV_B].astype(BF16)
            o_inner.append(jnp.dot(inner.astype(BF16), vh, preferred_element_type=F32))
            q_m.append(qm)
            dec_q.append(jnp.exp((ti + 1.0) * lg))
    return o_inner, q_m, dec_q, k_dec


def _state_update(s_pair, kd, v, p, c):
    row = lax.broadcasted_iota(jnp.int32, (LANES, DV_B), 0)
    top = row < DK_B
    u0 = lax.dot_general(kd, v[:, (2 * p) * DV_B:(2 * p + 1) * DV_B].astype(BF16), _TN, preferred_element_type=F32)
    u1 = lax.dot_general(kd, v[:, (2 * p + 1) * DV_B:(2 * p + 2) * DV_B].astype(BF16), _TN, preferred_element_type=F32)
    gc = jnp.where(top, math.exp(c * _log_gamma(2 * p)), math.exp(c * _log_gamma(2 * p + 1)))
    return s_pair * gc + jnp.where(top, u0, u1)


def _ret_epilogue(o, g, ng):
    ms = jnp.mean(o * o, axis=-1, keepdims=True)
    return ((o * lax.rsqrt(ms + EPS)) * ng) * _silu(g)


def _ret_prompt_body(q_ref, k_ref, v_ref, g_ref, cos_ref, sin_ref, ng_ref, y_ref, so_ref, s_ref, *, c):
    ci = pl.program_id(1)

    @pl.when(ci == 0)
    def _():
        s_ref[...] = jnp.zeros(s_ref.shape, F32)

    ng = ng_ref[...]
    for ch in range(q_ref.shape[0] // c):
        rs = slice(ch * c, (ch + 1) * c)
        cos, sin = cos_ref[rs, :], sin_ref[rs, :]
        q = _rotary(q_ref[rs, :], cos, sin)
        k = _rotary(k_ref[rs, :], cos, sin) * (DK_B ** -0.5)
        v = v_ref[rs, :]
        o_inner, q_m, dec_q, k_dec = _ret_tile(q, k, v, c)
        for p in range(H_B // 2):
            s_pair = s_ref[p]
            sb = s_pair.astype(BF16)
            for hh in range(2):
                h = 2 * p + hh
                o = o_inner[h] + jnp.dot(q_m[h], sb, preferred_element_type=F32) * dec_q[h]
                hs = slice(h * DV_B, (h + 1) * DV_B)
                y_ref[rs, hs] = _ret_epilogue(o, g_ref[rs, hs], ng).astype(BF16)
            s_ref[p] = _state_update(s_pair, k_dec[p], v, p, c)

    @pl.when(ci == pl.num_programs(1) - 1)
    def _():
        so_ref[0] = s_ref[...]


def retention_prompt(proj, cos, sin, ret_norm_g, batch, seq, col0):
    n = batch * seq
    c = 128 if seq % 128 == 0 else seq
    rows = c * _pick(seq // c, (4, 2, 1))
    nc = seq // rows
    wq = H_B * DK_B
    wv = H_B * DV_B
    assert col0 % wv == 0
    qc, vc = col0 // wq, col0 // wv
    row = lambda b, i: b * nc + i
    y, st = pl.pallas_call(
        functools.partial(_ret_prompt_body, c=c),
        grid=(batch, nc),
        in_specs=[pl.BlockSpec((rows, wq), lambda b, i: (row(b, i), qc)),
                  pl.BlockSpec((rows, wq), lambda b, i: (row(b, i), qc + 1)),
                  pl.BlockSpec((rows, wv), lambda b, i: (row(b, i), vc + 1)),
                  pl.BlockSpec((rows, wv), lambda b, i: (row(b, i), vc + 2)),
                  pl.BlockSpec((rows, wq), lambda b, i: (i, 0)),
                  pl.BlockSpec((rows, wq), lambda b, i: (i, 0)),
                  pl.BlockSpec((1, DV_B), lambda b, i: (0, 0))],
        out_specs=[pl.BlockSpec((rows, wv), lambda b, i: (row(b, i), 0)),
                   pl.BlockSpec((1, H_B // 2, LANES, DV_B), lambda b, i: (b, 0, 0, 0))],
        out_shape=[jax.ShapeDtypeStruct((n, wv), BF16),
                   jax.ShapeDtypeStruct((batch, H_B // 2, LANES, DV_B), F32)],
        scratch_shapes=[pltpu.VMEM((H_B // 2, LANES, DV_B), F32)],
        compiler_params=_cparams(("arbitrary", "arbitrary")),
        name="retention_prompt",
    )(proj, proj, proj, proj, cos, sin, ret_norm_g.reshape(1, DV_B))
    return y, st.reshape(batch, H_B, DK_B, DV_B)


def _ret_sample_body(q_ref, k_ref, v_ref, g_ref, cos_ref, sin_ref, ng_ref, s0_ref, y_ref, so_ref,
                     o_ref, qm_ref, kd_ref, *, c):
    gi = pl.program_id(0)
    r = q_ref.shape[0]
    shift = int(math.log2(c))
    v = v_ref[...]

    @pl.when(gi == 0)
    def _():
        cos, sin = cos_ref[...], sin_ref[...]
        q = _rotary(q_ref[...], cos, sin)
        k = _rotary(k_ref[...], cos, sin) * (DK_B ** -0.5)
        o_inner, q_m, dec_q, k_dec = _ret_tile(q, k, v, c)
        for h in range(H_B):
            o_ref[:, h * DV_B:(h + 1) * DV_B] = o_inner[h]
            qm_ref[h] = q_m[h]
        for p in range(H_B // 2):
            kd_ref[p] = k_dec[p]

    rsel = (lax.broadcasted_iota(jnp.int32, (r, LANES), 0) >> shift) == gi
    ti = (lax.broadcasted_iota(jnp.int32, (r, 1), 0) & (c - 1)).astype(F32)
    zero = jnp.zeros((r, LANES), BF16)
    for p in range(H_B // 2):
        s_pair = s0_ref[0, p]
        sb = s_pair.astype(BF16)
        for hh in range(2):
            h = 2 * p + hh
            qm = jnp.where(rsel, qm_ref[h], zero)
            hs = slice(h * DV_B, (h + 1) * DV_B)
            o_ref[:, hs] += jnp.dot(qm, sb, preferred_element_type=F32) * jnp.exp((ti + 1.0) * _log_gamma(h))
        kd = jnp.where(rsel, kd_ref[p], zero)
        so_ref[0, p] = _state_update(s_pair, kd, v, p, c)

    @pl.when(gi == pl.num_programs(0) - 1)
    def _():
        ng = ng_ref[...]
        for h in range(H_B):
            hs = slice(h * DV_B, (h + 1) * DV_B)
            y_ref[:, hs] = _ret_epilogue(o_ref[:, hs], g_ref[:, hs], ng).astype(BF16)


def retention_sample(proj, cos, sin, ret_norm_g, state0, db, t_new, col0):
    n = db * t_new
    wq = H_B * DK_B
    wv = H_B * DV_B
    qc, vc = col0 // wq, col0 // wv
    s0 = state0.reshape(db, H_B // 2, LANES, DV_B)
    st_spec = pl.BlockSpec((1, H_B // 2, LANES, DV_B), lambda g: (g, 0, 0, 0))
    y, st = pl.pallas_call(
        functools.partial(_ret_sample_body, c=t_new),
        grid=(db,),
        in_specs=[pl.BlockSpec((n, wq), lambda g: (0, qc)),
                  pl.BlockSpec((n, wq), lambda g: (0, qc + 1)),
                  pl.BlockSpec((n, wv), lambda g: (0, vc + 1)),
                  pl.BlockSpec((n, wv), lambda g: (0, vc + 2)),
                  pl.BlockSpec((n, wq), lambda g: (0, 0)),
                  pl.BlockSpec((n, wq), lambda g: (0, 0)),
                  pl.BlockSpec((1, DV_B), lambda g: (0, 0)),
                  st_spec],
        out_specs=[pl.BlockSpec((n, wv), lambda g: (0, 0)), st_spec],
        out_shape=[jax.ShapeDtypeStruct((n, wv), BF16),
                   jax.ShapeDtypeStruct((db, H_B // 2, LANES, DV_B), F32)],
        scratch_shapes=[pltpu.VMEM((n, wv), F32),
                        pltpu.VMEM((H_B, n, LANES), BF16),
                        pltpu.VMEM((H_B // 2, n, LANES), BF16)],
        compiler_params=_cparams(("arbitrary",)),
        name="retention_sample",
    )(proj, proj, proj, proj, cos, sin, ret_norm_g.reshape(1, DV_B), s0)
    return y, st.reshape(db, H_B, DK_B, DV_B)


def _rotary_tables(pos):
    half = DK_B // 2
    inv = 1.0 / (10000.0 ** (jnp.arange(half, dtype=F32) / half))
    ang = pos.astype(F32)[:, None] * inv[None, :]
    cos, sin = jnp.cos(ang), jnp.sin(ang)
    cos_t = jnp.tile(jnp.concatenate([cos, cos], axis=-1), (1, H_B))
    sin_t = jnp.tile(jnp.concatenate([-sin, sin], axis=-1), (1, H_B))
    return cos_t, sin_t


_HALO = 32
_CONV_ROWS = 64
_SEQS_PER_STEP = 8


def _conv_body(a_ref, b_ref, st_ref, w_ref, cb_ref, lg_ref, lb_ref, y_ref, so_ref, full_ref, sh_ref, uc_ref, *, ts):
    for q in range(a_ref.shape[0]):
        _conv_sequence(a_ref.at[q], b_ref.at[q], st_ref.at[q], w_ref, cb_ref, lg_ref, lb_ref,
                       y_ref.at[q], so_ref.at[q], full_ref, sh_ref, uc_ref, ts)


def _conv_sequence(a_ref, b_ref, st_ref, w_ref, cb_ref, lg_ref, lb_ref, y_ref, so_ref, full_ref, sh_ref, uc_ref, ts):
    ti = pl.program_id(1)
    lo = _HALO - (K_CONV - 1)

    @pl.when(ti == 0)
    def _():
        full_ref[0:lo, :] = jnp.zeros((lo, full_ref.shape[1]), F32)
        full_ref[lo:_HALO, :] = st_ref[...]

    full_ref[_HALO:_HALO + ts, :] = a_ref[...] * jax.nn.sigmoid(b_ref[...])
    nsh = sh_ref.shape[1]
    for s in range(1, SUBLANES):
        sh_ref[s - 1] = full_ref[s:s + nsh, :]
    rb = min(ts, _CONV_ROWS)
    for c in range(full_ref.shape[1] // LANES):
        cs = slice(c * LANES, (c + 1) * LANES)
        for r0 in range(0, ts, rb):
            acc = None
            for j in range(K_CONV):
                s = (lo + j) % SUBLANES
                a0 = lo + j - s + r0
                win = full_ref[a0:a0 + rb, cs] if s == 0 else sh_ref[s - 1, a0:a0 + rb, cs]
                term = win * w_ref[j:j + 1, cs]
                acc = term if acc is None else acc + term
            uc_ref[r0:r0 + rb, cs] = acc + cb_ref[:, cs]
    uc = uc_ref[...]
    mu = jnp.mean(uc, axis=-1, keepdims=True)
    dev = uc - mu
    var = jnp.mean(dev * dev, axis=-1, keepdims=True)
    y = (dev * lax.rsqrt(var + EPS)) * lg_ref[...] + lb_ref[...]
    y_ref[...] = _silu(y).astype(BF16)
    so_ref[...] = full_ref[lo + ts:_HALO + ts, :]
    if ts >= _HALO:
        full_ref[0:_HALO, :] = full_ref[ts:ts + _HALO, :]


def conv_module(proj3, state, conv_w, conv_b, ln_g, ln_b, col0):
    b, s, _ = proj3.shape
    c = conv_w.shape[1]
    ts = _pick(s, (256, 128)) if s >= 128 else s
    nt = s // ts
    assert nt == 1 or ts >= _HALO
    cc = col0 // c
    gb = _pick(b, (_SEQS_PER_STEP, 1)) if ts < LANES else 1
    vec = lambda: pl.BlockSpec((1, c), lambda i, t: (0, 0))
    return pl.pallas_call(
        functools.partial(_conv_body, ts=ts),
        grid=(b // gb, nt),
        in_specs=[pl.BlockSpec((gb, ts, c), lambda i, t: (i, t, cc)),
                  pl.BlockSpec((gb, ts, c), lambda i, t: (i, t, cc + 1)),
                  pl.BlockSpec((gb, K_CONV - 1, c), lambda i, t: (i, 0, 0)),
                  pl.BlockSpec((K_CONV, c), lambda i, t: (0, 0)),
                  vec(), vec(), vec()],
        out_specs=[pl.BlockSpec((gb, ts, c), lambda i, t: (i, t, 0)),
                   pl.BlockSpec((gb, K_CONV - 1, c), lambda i, t: (i, 0, 0))],
        out_shape=[jax.ShapeDtypeStruct((b, s, c), BF16),
                   jax.ShapeDtypeStruct((b, K_CONV - 1, c), F32)],
        scratch_shapes=[pltpu.VMEM((_HALO + ts, c), F32),
                        pltpu.VMEM((SUBLANES - 1, _HALO + ts - SUBLANES, c), F32),
                        pltpu.VMEM((ts, c), F32)],
        compiler_params=_cparams(("arbitrary", "arbitrary")),
        name="conv_module",
    )(proj3, proj3, state, conv_w, conv_b.reshape(1, c), ln_g.reshape(1, c), ln_b.reshape(1, c))


def _merge_body(a_ref, b_ref, c_ref, wa_ref, wb_ref, wc_ref, ga_ref, gb_ref, gc_ref,
                ba_ref, bb_ref, bc_ref, o_ref, *copies, emit):
    ws = [wa_ref[...], wb_ref[...], wc_ref[...]]
    if emit:
        ws = [w.astype(BF16) for w in ws]
        for copy_ref, w in zip(copies, ws):
            copy_ref[...] = w
    ya = jnp.dot(a_ref[...], ws[0], preferred_element_type=F32)
    m = jax.nn.sigmoid(ga_ref[...] + ba_ref[...]) * ya
    yb = jnp.dot(b_ref[...], ws[1], preferred_element_type=F32)
    m = m + jax.nn.sigmoid(gb_ref[...] + bb_ref[...]) * yb
    yc = jnp.dot(c_ref[...], ws[2], preferred_element_type=F32)
    m = m + jax.nn.sigmoid(gc_ref[...] + bc_ref[...]) * yc
    o_ref[...] = m.astype(BF16)


def branch_merge(a, b, c, wa, wb, wc, proj, b_gate, col0):
    n, kd = a.shape
    (wa_arr, emit, l, _, d), (wb_arr, _, _, _, _), (wc_arr, _, _, _, _) = (_weight_form(w) for w in (wa, wb, wc))
    tm = _pick(n, (512, 256, 128))
    tn = _pick(d, (1024, 512, 256, 128))
    nj = d // tn
    g0 = col0 // tn
    bg = b_gate.reshape(1, 3 * d)
    assert not emit or n == tm
    x_spec = lambda: pl.BlockSpec((tm, kd), lambda j, i: (i, 0))
    w_spec = lambda: (pl.BlockSpec((None, kd, tn), lambda j, i: (l, 0, j)) if emit
                      else pl.BlockSpec((kd, tn), lambda j, i: (0, j)))
    g_spec = lambda k: pl.BlockSpec((tm, tn), lambda j, i, k=k: (i, g0 + k * nj + j))
    bias_spec = lambda k: pl.BlockSpec((1, tn), lambda j, i, k=k: (0, k * nj + j))
    out_specs = [pl.BlockSpec((tm, tn), lambda j, i: (i, j))]
    out_shape = [jax.ShapeDtypeStruct((n, d), BF16)]
    if emit:
        out_specs += [pl.BlockSpec((kd, tn), lambda j, i: (0, j))] * 3
        out_shape += [jax.ShapeDtypeStruct((kd, d), BF16)] * 3
    outs = pl.pallas_call(
        functools.partial(_merge_body, emit=emit),
        grid=(nj, n // tm),
        in_specs=[x_spec(), x_spec(), x_spec(), w_spec(), w_spec(), w_spec(),
                  g_spec(0), g_spec(1), g_spec(2), bias_spec(0), bias_spec(1), bias_spec(2)],
        out_specs=out_specs,
        out_shape=out_shape,
        compiler_params=_cparams(("parallel", "arbitrary")),
        name="branch_merge",
    )(a, b, c, wa_arr, wb_arr, wc_arr, proj, proj, proj, bg, bg, bg)
    return tuple(outs) if emit else outs[0]


def _ffn_act_body(g_ref, u_ref, st_ref, w_ref, cb_ref, y_ref, so_ref, full_ref, *, ts):
    ti = pl.program_id(2)
    lo = SUBLANES - (K_FFN - 1)
    for q in range(g_ref.shape[0]):
        @pl.when(ti == 0)
        def _():
            full_ref[0:lo, :] = jnp.zeros((lo, full_ref.shape[1]), F32)
            full_ref[lo:SUBLANES, :] = st_ref[q]

        full_ref[SUBLANES:SUBLANES + ts, :] = g_ref[q]
        acc = jnp.zeros((ts, full_ref.shape[1]), F32)
        for j in range(K_FFN):
            acc = acc + full_ref[lo + j:lo + j + ts, :] * w_ref[j:j + 1, :]
        y_ref[q] = (_silu(acc + cb_ref[...]) * u_ref[q]).astype(BF16)
        so_ref[q] = full_ref[lo + ts:SUBLANES + ts, :]
        if ts >= SUBLANES:
            full_ref[0:SUBLANES, :] = full_ref[ts:ts + SUBLANES, :]


def ffn_act(h3, state, conv_w, conv_b):
    b, s, f2 = h3.shape
    f = f2 // 2
    ts = _pick(s, (512, 256, 128)) if s >= 128 else s
    tc = _pick(f, (512, 256, 128)) if ts >= 128 else f
    nc = f // tc
    nt = s // ts
    assert nt == 1 or ts >= SUBLANES
    gb = _pick(b, (_SEQS_PER_STEP, 1)) if ts < LANES else 1
    return pl.pallas_call(
        functools.partial(_ffn_act_body, ts=ts),
        grid=(b // gb, nc, nt),
        in_specs=[pl.BlockSpec((gb, ts, tc), lambda i, c, t: (i, t, c)),
                  pl.BlockSpec((gb, ts, tc), lambda i, c, t: (i, t, nc + c)),
                  pl.BlockSpec((gb, K_FFN - 1, tc), lambda i, c, t: (i, 0, c)),
                  pl.BlockSpec((K_FFN, tc), lambda i, c, t: (0, c)),
                  pl.BlockSpec((1, tc), lambda i, c, t: (0, c))],
        out_specs=[pl.BlockSpec((gb, ts, tc), lambda i, c, t: (i, t, c)),
                   pl.BlockSpec((gb, K_FFN - 1, tc), lambda i, c, t: (i, 0, c))],
        out_shape=[jax.ShapeDtypeStruct((b, s, f), BF16),
                   jax.ShapeDtypeStruct((b, K_FFN - 1, f), F32)],
        scratch_shapes=[pltpu.VMEM((SUBLANES + ts, tc), F32)],
        compiler_params=_cparams(("arbitrary", "arbitrary", "arbitrary")),
        name="ffn_act",
    )(h3, h3, state, conv_w, conv_b.reshape(1, f))


def _layer(x, batch, seq, pos, attn_fn, ret_fn, conv0, ffn0, lw, mw, fuse_ffn):
    n, d = x.shape
    a_cols = H_A * (4 * DK_A + DV_A)
    b_cols = H_B * (2 * DK_B + 2 * DV_B)
    c_conv = lw['conv_w'].shape[1]
    emit = isinstance(mw['w_in'], tuple)
    copies = {}

    def take(name, res):
        if not emit:
            return res
        copies[name] = res[1]
        return res[0]

    proj = take('w_in', norm_matmul(x, lw['norm1_g'], mw['w_in']))
    a_in, kf, vf = attn_fn(proj)
    cos, sin = _rotary_tables(pos)
    b_in, ret_new = ret_fn(proj, cos, sin, a_cols)
    c_in, conv_new = conv_module(proj.reshape(batch, seq, -1), conv0, lw['conv_w'], lw['conv_b'],
                                 lw['conv_ln_g'], lw['conv_ln_b'], a_cols + b_cols)
    c_in = c_in.reshape(n, c_conv)
    merged = branch_merge(a_in, b_in, c_in, mw['w_o_a'], mw['w_o_b'], mw['w_o_c'], proj, lw['b_gate'],
                          a_cols + b_cols + 2 * c_conv)
    if emit:
        m, copies['w_o_a'], copies['w_o_b'], copies['w_o_c'] = merged
    else:
        m = merged
    x = take('w_out', matmul_residual(m, mw['w_out'], x))
    if fuse_ffn:
        act, ffn_new = ffn_up_prompt(x, lw['norm2_g'], mw['w_ffn_in'], ffn0, lw['ffn_conv_w'], lw['ffn_conv_b'],
                                     batch, seq)
    else:
        h2 = take('w_ffn_in', norm_matmul(x, lw['norm2_g'], mw['w_ffn_in']))
        act, ffn_new = ffn_act(h2.reshape(batch, seq, -1), ffn0, lw['ffn_conv_w'], lw['ffn_conv_b'])
        act = act.reshape(n, -1)
    x = take('w_ffn_down', matmul_residual(act, mw['w_ffn_down'], x))
    states = (kf.reshape(batch, seq, H_A, 2 * DK_A), vf.reshape(batch, seq, H_A, DV_A), ret_new, conv_new, ffn_new)
    return x, states, copies


def kernel(x_prompt, x_sample, cache_k, cache_v, state_ret, state_conv, state_ffn, page_table, rel_bias, norm1_g, w_in, b_gate, q_norm_g, k_norm_g, lam_vec, subln_g, w_o_a, ret_norm_g, w_o_b, conv_w, conv_b, conv_ln_g, conv_ln_b, w_o_c, w_out, norm2_g, w_ffn_in, ffn_conv_w, ffn_conv_b, w_ffn_down):
    bp, sp, d = x_prompt.shape
    db, t_new, _ = x_sample.shape
    depth = w_in.shape[0]
    past = page_table.shape[1] * cache_k.shape[2]
    pos_p = jnp.arange(sp)
    pos_s = jnp.tile(past + jnp.arange(t_new), db)
    yp = x_prompt.reshape(bp * sp, d)
    ys = x_sample.reshape(db * t_new, d)
    c_conv = conv_w.shape[2]
    d_ff = ffn_conv_w.shape[2]
    zero_conv = jnp.zeros((bp, K_CONV - 1, c_conv), F32)
    zero_ffn = jnp.zeros((bp, K_FFN - 1, d_ff), F32)
    sts_p, sts_s = [], []
    stacked = {'w_in': w_in, 'w_o_a': w_o_a, 'w_o_b': w_o_b, 'w_o_c': w_o_c, 'w_out': w_out,
               'w_ffn_in': w_ffn_in, 'w_ffn_down': w_ffn_down}
    for l in range(depth):
        lw = {'norm1_g': norm1_g[l], 'b_gate': b_gate[l], 'q_norm_g': q_norm_g[l], 'k_norm_g': k_norm_g[l],
              'conv_w': conv_w[l], 'conv_b': conv_b[l], 'conv_ln_g': conv_ln_g[l], 'conv_ln_b': conv_ln_b[l],
              'norm2_g': norm2_g[l], 'ffn_conv_w': ffn_conv_w[l], 'ffn_conv_b': ffn_conv_b[l]}
        lam_init = 0.8 - 0.6 * math.exp(-0.3 * l)

        def s_attn(proj, l=l, lam_init=lam_init, lw=lw):
            qn, kf, vf = attn_prep_sample(proj, lw['q_norm_g'], lw['k_norm_g'])
            return sample_attn(qn, kf, vf, cache_k, cache_v, l, page_table, rel_bias, lam_vec[l], subln_g[l],
                               lam_init, db, t_new), kf, vf

        def s_ret(proj, cos, sin, col0, l=l):
            return retention_sample(proj, cos, sin, ret_norm_g[l], state_ret[l], db, t_new, col0)

        ys, st, copies = _layer(ys, db, t_new, pos_s, s_attn, s_ret, state_conv[l], state_ffn[l], lw,
                                {name: (arr, l) for name, arr in stacked.items()}, False)
        sts_s.append(st)

        def p_attn(proj, l=l, lam_init=lam_init, lw=lw):
            t = _pick(sp, (_ATTN_TILE, LANES))
            qt, kf, kb, vf, vt = attn_prep_prompt(proj, lw['q_norm_g'], lw['k_norm_g'], bp, sp, t)
            return prompt_attn(qt, kb, vt, rel_bias, lam_vec[l], subln_g[l], lam_init, bp, sp, t), kf, vf

        def p_ret(proj, cos, sin, col0, l=l):
            return retention_prompt(proj, cos, sin, ret_norm_g[l], bp, sp, col0)

        yp, st, _ = _layer(yp, bp, sp, pos_p, p_attn, p_ret, zero_conv, zero_ffn, lw, copies, True)
        sts_p.append(st)

    def stk(sts, i):
        return jnp.stack([st[i] for st in sts], axis=0)

    return (yp.reshape(bp, sp, d), ys.reshape(db, t_new, d),
            stk(sts_p, 0), stk(sts_p, 1), stk(sts_p, 2), stk(sts_p, 3), stk(sts_p, 4),
            stk(sts_s, 0), stk(sts_s, 1), stk(sts_s, 2), stk(sts_s, 3), stk(sts_s, 4))
```

```python
import functools
import math

import numpy as np
import jax
import jax.numpy as jnp
from jax import lax
from jax.experimental import pallas as pl
from jax.experimental.pallas import tpu as pltpu

F32 = jnp.float32
BF16 = jnp.bfloat16

H_A = 8
DK_A = 64
DV_A = 128
H_B = 8
DK_B = 64
DV_B = 128
K_CONV = 31
K_FFN = 3
N_BUCKETS = 32
MAX_DIST = 128
EPS = 1e-6
NEG = -1e30

LANES = 128
SUBLANES = 8
VMEM_LIMIT = 56 * 1024 * 1024

_NT = (((1,), (1,)), ((), ()))
_TN = (((0,), (0,)), ((), ()))


def _cparams(sem):
    return pltpu.CompilerParams(dimension_semantics=sem, vmem_limit_bytes=VMEM_LIMIT)


def _pick(n, cands):
    for c in cands:
        if n % c == 0:
            return c
    return n


def _t5_bucket_np(d):
    max_exact = N_BUCKETS // 2
    d = np.maximum(d, 0)
    df = np.maximum(d, 1).astype(np.float32)
    large = max_exact + (np.log(df / np.float32(max_exact)) / np.float32(math.log(MAX_DIST / max_exact))
                         * np.float32(N_BUCKETS - max_exact)).astype(np.int32)
    return np.where(d < max_exact, d, np.minimum(large, N_BUCKETS - 1)).astype(np.int32)


def _bias_from_buckets(idx, rb_ref, h):
    acc = jnp.zeros(idx.shape, F32)
    for b in range(N_BUCKETS):
        acc = jnp.where(idx == b, rb_ref[b, h], acc)
    return acc


def _lam(lv_ref, lam_init):
    lv = lv_ref[...]
    a = jnp.sum(lv[0:1] * lv[1:2], axis=-1, keepdims=True)
    b = jnp.sum(lv[2:3] * lv[3:4], axis=-1, keepdims=True)
    return jnp.exp(a) - jnp.exp(b) + lam_init


def _silu(x):
    return x * jax.nn.sigmoid(x)


def _weight_form(w):
    if isinstance(w, tuple):
        arr, l = w
        return arr, True, l, arr.shape[1], arr.shape[2]
    return w, False, None, w.shape[0], w.shape[1]


def _norm_mm_body(x_ref, g_ref, w_ref, o_ref, *rest, emit):
    h_ref = rest[-1]

    @pl.when(pl.program_id(1) == 0)
    def _():
        x = x_ref[...]
        ms = jnp.mean(x * x, axis=-1, keepdims=True)
        h_ref[...] = ((x * lax.rsqrt(ms + EPS)) * g_ref[...]).astype(BF16)

    w = w_ref[...]
    if emit:
        w = w.astype(BF16)
        rest[0][...] = w
    o_ref[...] = jnp.dot(h_ref[...], w, preferred_element_type=F32)


def norm_matmul(x, g, w):
    n, d = x.shape
    w_arr, emit, l, _, c = _weight_form(w)
    tm = _pick(n, (1024, 512, 256, 128))
    tn = _pick(c, (1024, 512, 256, 128) if emit else (2048, 1024, 512, 256, 128))
    assert not emit or n == tm
    w_spec = (pl.BlockSpec((None, d, tn), lambda i, j: (l, 0, j)) if emit
              else pl.BlockSpec((d, tn), lambda i, j: (0, j)))
    out_specs = [pl.BlockSpec((tm, tn), lambda i, j: (i, j))]
    out_shape = [jax.ShapeDtypeStruct((n, c), F32)]
    if emit:
        out_specs.append(pl.BlockSpec((d, tn), lambda i, j: (0, j)))
        out_shape.append(jax.ShapeDtypeStruct((d, c), BF16))
    outs = pl.pallas_call(
        functools.partial(_norm_mm_body, emit=emit),
        grid=(n // tm, c // tn),
        in_specs=[pl.BlockSpec((tm, d), lambda i, j: (i, 0)),
                  pl.BlockSpec((1, d), lambda i, j: (0, 0)),
                  w_spec],
        out_specs=out_specs,
        out_shape=out_shape,
        scratch_shapes=[pltpu.VMEM((tm, d), BF16)],
        compiler_params=_cparams(("parallel", "arbitrary")),
        name="norm_matmul",
    )(x, g.reshape(1, d), w_arr)
    return tuple(outs) if emit else outs[0]


def _ffn_up_body(x_ref, g_ref, wg_ref, wu_ref, cw_ref, cb_ref, st_ref, y_ref, so_ref, h_ref, gbuf_ref, carry_ref,
                 *, tiles_per_seq):
    i = pl.program_id(0)
    j = pl.program_id(1)
    tm = x_ref.shape[0]
    lo = SUBLANES - (K_FFN - 1)

    @pl.when(j == 0)
    def _():
        x = x_ref[...]
        ms = jnp.mean(x * x, axis=-1, keepdims=True)
        h_ref[...] = ((x * lax.rsqrt(ms + EPS)) * g_ref[...]).astype(BF16)

    h = h_ref[...]
    first = (i % tiles_per_seq) == 0

    @pl.when(first)
    def _():
        gbuf_ref[0:lo, :] = jnp.zeros((lo, gbuf_ref.shape[1]), F32)
        gbuf_ref[lo:SUBLANES, :] = st_ref[0]

    @pl.when(jnp.logical_not(first))
    def _():
        gbuf_ref[0:SUBLANES, :] = carry_ref[j]

    gbuf_ref[SUBLANES:SUBLANES + tm, :] = jnp.dot(h, wg_ref[...], preferred_element_type=F32)
    acc = jnp.zeros((tm, gbuf_ref.shape[1]), F32)
    for k in range(K_FFN):
        acc = acc + gbuf_ref[lo + k:lo + k + tm, :] * cw_ref[k:k + 1, :]
    up = jnp.dot(h, wu_ref[...], preferred_element_type=F32)
    y_ref[...] = (_silu(acc + cb_ref[...]) * up).astype(BF16)
    carry_ref[j] = gbuf_ref[tm:tm + SUBLANES, :]
    so_ref[0] = gbuf_ref[lo + tm:SUBLANES + tm, :]


def ffn_up_prompt(x, g, w, state, conv_w, conv_b, batch, seq):
    n, d = x.shape
    f = w.shape[1] // 2
    tm = _pick(seq, (1024, 512, 256, 128))
    tn = _pick(f, (512, 256, 128))
    nj = f // tn
    tps = seq // tm
    act, tails = pl.pallas_call(
        functools.partial(_ffn_up_body, tiles_per_seq=tps),
        grid=(n // tm, nj),
        in_specs=[pl.BlockSpec((tm, d), lambda i, j: (i, 0)),
                  pl.BlockSpec((1, d), lambda i, j: (0, 0)),
                  pl.BlockSpec((d, tn), lambda i, j: (0, j)),
                  pl.BlockSpec((d, tn), lambda i, j: (0, nj + j)),
                  pl.BlockSpec((K_FFN, tn), lambda i, j: (0, j)),
                  pl.BlockSpec((1, tn), lambda i, j: (0, j)),
                  pl.BlockSpec((1, K_FFN - 1, tn), lambda i, j: (i // tps, 0, j))],
        out_specs=[pl.BlockSpec((tm, tn), lambda i, j: (i, j)),
                   pl.BlockSpec((1, K_FFN - 1, tn), lambda i, j: (i, 0, j))],
        out_shape=[jax.ShapeDtypeStruct((n, f), BF16),
                   jax.ShapeDtypeStruct((n // tm, K_FFN - 1, f), F32)],
        scratch_shapes=[pltpu.VMEM((tm, d), BF16),
                        pltpu.VMEM((SUBLANES + tm, tn), F32),
                        pltpu.VMEM((nj, SUBLANES, tn), F32)],
        compiler_params=_cparams(("arbitrary", "arbitrary")),
        name="ffn_up_prompt",
    )(x, g.reshape(1, d), w, w, conv_w, conv_b.reshape(1, f), state)
    return act, tails[tps - 1::tps]


def _mm_res_body(a_ref, w_ref, r_ref, o_ref, *rest, nk, emit):
    acc_ref = rest[-1]
    k = pl.program_id(2)
    w = w_ref[...]
    if emit:
        w = w.astype(BF16)
        rest[0][...] = w
    part = jnp.dot(a_ref[...], w, preferred_element_type=F32)
    if nk == 1:
        o_ref[...] = r_ref[...] + part
    else:
        @pl.when(k == 0)
        def _():
            acc_ref[...] = part

        @pl.when(jnp.logical_and(k > 0, k < nk - 1))
        def _():
            acc_ref[...] += part

        @pl.when(k == nk - 1)
        def _():
            o_ref[...] = r_ref[...] + (acc_ref[...] + part)


def matmul_residual(a, w, res):
    n = a.shape[0]
    w_arr, emit, l, kd, c = _weight_form(w)
    tk = kd if kd <= 2048 else _pick(kd, (2816, 2048, 1024, 512))
    nk = kd // tk
    if nk == 1 and not emit and c <= 2048:
        tm, tn = _pick(n, (512, 256, 128)), c
    else:
        tm, tn = _pick(n, (1024, 512, 256, 128)), _pick(c, (1024, 512, 256, 128))
    assert not emit or n == tm
    w_spec = (pl.BlockSpec((None, tk, tn), lambda i, j, k: (l, k, j)) if emit
              else pl.BlockSpec((tk, tn), lambda i, j, k: (k, j)))
    out_specs = [pl.BlockSpec((tm, tn), lambda i, j, k: (i, j))]
    out_shape = [jax.ShapeDtypeStruct((n, c), F32)]
    if emit:
        out_specs.append(pl.BlockSpec((tk, tn), lambda i, j, k: (k, j)))
        out_shape.append(jax.ShapeDtypeStruct((kd, c), BF16))
    outs = pl.pallas_call(
        functools.partial(_mm_res_body, nk=nk, emit=emit),
        grid=(n // tm, c // tn, nk),
        in_specs=[pl.BlockSpec((tm, tk), lambda i, j, k: (i, k)),
                  w_spec,
                  pl.BlockSpec((tm, tn), lambda i, j, k: (i, j))],
        out_specs=out_specs,
        out_shape=out_shape,
        scratch_shapes=[pltpu.VMEM((tm, tn), F32)],
        compiler_params=_cparams(("parallel", "parallel", "arbitrary")),
        name="matmul_residual",
    )(a, w_arr, res)
    return tuple(outs) if emit else outs[0]


def _group_rms(x, gain, gm):
    s = x * x
    hi = s.astype(BF16)
    lo = (s - hi.astype(F32)).astype(BF16)
    parts = []
    for c in range(x.shape[1] // LANES):
        sl = slice(c * LANES, (c + 1) * LANES)
        parts.append(jnp.dot(hi[:, sl], gm, preferred_element_type=F32)
                     + jnp.dot(lo[:, sl], gm, preferred_element_type=F32))
    ms = jnp.concatenate(parts, axis=-1) * (1.0 / DK_A)
    return (x * lax.rsqrt(ms + EPS)) * gain


_LOG2E = math.log2(math.e)
_Q_SCALE = DK_A ** -0.5 * _LOG2E
_DVX = DV_A + 16


def _attn_prep_sample_body(q_ref, k_ref, v_ref, qg_ref, kg_ref, gm_ref, qo_ref, kf_ref, vf_ref):
    gm = gm_ref[...]
    qo_ref[...] = _group_rms(q_ref[...], qg_ref[...], gm) * _Q_SCALE
    kf_ref[...] = _group_rms(k_ref[...], kg_ref[...], gm)
    vf_ref[...] = v_ref[...]


def _attn_prep_prompt_body(q_ref, k_ref, v_ref, qg_ref, kg_ref, gm_ref, qt_ref, kf_ref, kb_ref, vf_ref, vt_ref):
    gm = gm_ref[...]
    qn = _group_rms(q_ref[...], qg_ref[...], gm) * _Q_SCALE
    kn = _group_rms(k_ref[...], kg_ref[...], gm)
    kf_ref[...] = kn
    kb_ref[...] = kn.astype(BF16)
    v = v_ref[...]
    vf_ref[...] = v
    for h in range(H_A):
        hs = slice(h * LANES, (h + 1) * LANES)
        qt_ref[0, h, 0] = qn[:, hs].T.astype(BF16)
        vt_ref[0, h, 0, 0:DV_A, :] = v[:, hs].T.astype(BF16)
        vt_ref[0, h, 0, DV_A:_DVX, :] = jnp.ones((_DVX - DV_A, v.shape[0]), BF16)


def _attn_prep_consts(q_gain, k_gain):
    w = H_A * 2 * DK_A
    gm = np.kron(np.eye(LANES // DK_A, dtype=np.float32), np.ones((DK_A, DK_A), np.float32))
    return (jnp.tile(q_gain, w // DK_A).reshape(1, w), jnp.tile(k_gain, w // DK_A).reshape(1, w),
            jnp.asarray(gm, BF16))


def attn_prep_sample(proj, q_gain, k_gain):
    n = proj.shape[0]
    w = H_A * 2 * DK_A
    tm = _pick(n, (512, 256, 128))
    col = lambda c: pl.BlockSpec((tm, w), lambda i, c=c: (i, c))
    cst = lambda shape: pl.BlockSpec(shape, lambda i: (0, 0))
    out = lambda: pl.BlockSpec((tm, w), lambda i: (i, 0))
    return pl.pallas_call(
        _attn_prep_sample_body,
        grid=(n // tm,),
        in_specs=[col(0), col(1), col(2), cst((1, w)), cst((1, w)), cst((LANES, LANES))],
        out_specs=[out(), out(), out()],
        out_shape=[jax.ShapeDtypeStruct((n, w), F32)] * 3,
        compiler_params=_cparams(("parallel",)),
        name="attn_prep_sample",
    )(proj, proj, proj, *_attn_prep_consts(q_gain, k_gain))


def attn_prep_prompt(proj, q_gain, k_gain, batch, seq, t):
    n = batch * seq
    w = H_A * 2 * DK_A
    nt = seq // t
    col = lambda c: pl.BlockSpec((t, w), lambda b, i, c=c: (b * nt + i, c))
    cst = lambda shape: pl.BlockSpec(shape, lambda b, i: (0, 0))
    out = lambda: pl.BlockSpec((t, w), lambda b, i: (b * nt + i, 0))
    tr = lambda rows=LANES: pl.BlockSpec((1, H_A, 1, rows, t), lambda b, i: (b, 0, i, 0, 0))
    tr_shape = lambda rows=LANES: jax.ShapeDtypeStruct((batch, H_A, nt, rows, t), BF16)
    return pl.pallas_call(
        _attn_prep_prompt_body,
        grid=(batch, nt),
        in_specs=[col(0), col(1), col(2), cst((1, w)), cst((1, w)), cst((LANES, LANES))],
        out_specs=[tr(), out(), out(), out(), tr(_DVX)],
        out_shape=[tr_shape(), jax.ShapeDtypeStruct((n, w), F32), jax.ShapeDtypeStruct((n, w), BF16),
                   jax.ShapeDtypeStruct((n, w), F32), tr_shape(_DVX)],
        compiler_params=_cparams(("parallel", "parallel")),
        name="attn_prep_prompt",
    )(proj, proj, proj, *_attn_prep_consts(q_gain, k_gain))


def _prompt_attn_body(rb_ref, ib_ref, lv_ref, sg_ref, qt_ref, k_ref, vt_ref, o_ref,
                      bias_ref, m_ref, acc_ref, *, t, lam_init, n_hg):
    hg = pl.program_id(1)
    qi = pl.program_id(2)
    heads = range(_ATTN_HEADS)
    build = qi == 0
    if n_hg == 1:
        build = jnp.logical_and(build, pl.program_id(0) == 0)

    @pl.when(build)
    def _():
        for g in heads:
            h = hg * _ATTN_HEADS + g
            far_bias = rb_ref[N_BUCKETS - 1, h]
            for o in range(2):
                idx = ib_ref[o]
                bias_ref[g, o] = jnp.where(idx >= 0, (_bias_from_buckets(idx, rb_ref, h) - far_bias) * _LOG2E, NEG)

    chains = [(g, c) for g in heads for c in range(2)]
    qct = {}
    for g in heads:
        qt = qt_ref[0, g, 0]
        row = lax.broadcasted_iota(jnp.int32, qt.shape, 0)
        zero = jnp.zeros_like(qt)
        qct[g, 0] = jnp.where(row < DK_A, qt, zero)
        qct[g, 1] = jnp.where(row >= DK_A, qt, zero)

    m_ref[...] = jnp.full(m_ref.shape, NEG, F32)
    acc_ref[...] = jnp.zeros(acc_ref.shape, F32)

    def tile(kj, o):
        off = pl.multiple_of(kj * t, t)
        cols = {c: slice(c * t, (c + 1) * t) for c in range(2)}
        s = {(g, c): jnp.dot(k_ref[0, pl.ds(off, t), g * LANES:(g + 1) * LANES], qct[g, c],
                             preferred_element_type=F32) for g, c in chains}
        if o is not None:
            s = {(g, c): s[g, c] + bias_ref[g, o, :, cols[c]] for g, c in chains}
        m_old = {(g, c): m_ref[g, :, cols[c]] for g, c in chains}
        m_new = {k: jnp.maximum(m_old[k], jnp.max(s[k], axis=0, keepdims=True)) for k in chains}
        alpha = {k: jnp.exp2(m_old[k] - m_new[k]) for k in chains}
        p = {k: jnp.exp2(s[k] - m_new[k]).astype(BF16) for k in chains}
        for g, c in chains:
            m_ref[g, :, cols[c]] = m_new[g, c]
        pv = {(g, c): jnp.dot(vt_ref[0, g, kj], p[g, c], preferred_element_type=F32) for g, c in chains}
        for g, c in chains:
            acc_ref[g, :, cols[c]] = alpha[g, c] * acc_ref[g, :, cols[c]] + pv[g, c]

    def far(kj, carry):
        tile(kj, None)
        return carry

    lax.fori_loop(0, jnp.maximum(qi - 1, 0), far, 0)

    @pl.when(qi >= 1)
    def _():
        tile(qi - 1, 1)

    tile(qi, 0)

    lam = _lam(lv_ref, lam_init)
    for g in heads:
        o2t = acc_ref[g, 0:DV_A, :] / acc_ref[g, DV_A:DV_A + 1, :]
        o = (o2t[:, 0:t] - lam * o2t[:, t:2 * t]).T
        ms = jnp.mean(o * o, axis=-1, keepdims=True)
        y = ((o * lax.rsqrt(ms + EPS)) * sg_ref[...]) * (1.0 - lam_init)
        o_ref[0, :, g * LANES:(g + 1) * LANES] = y.astype(BF16)


_ATTN_TILE = 256
_ATTN_HEADS = 8


def prompt_attn(qt, kb, vt, rel_bias, lam_vec, subln_g, lam_init, batch, seq, t):
    w = H_A * DV_A
    assert t >= MAX_DIST and seq % t == 0
    nq = seq // t
    hgs = _ATTN_HEADS
    key = np.arange(t)[:, None]
    qry = (np.arange(2 * t) % t)[None, :]
    d0 = qry - key
    ib = jnp.asarray(np.stack([np.where(d0 >= 0, _t5_bucket_np(d0), -1), _t5_bucket_np(d0 + t)]), jnp.int32)
    assert int(_t5_bucket_np(np.array([t + 1]))[0]) == N_BUCKETS - 1
    out = pl.pallas_call(
        functools.partial(_prompt_attn_body, t=t, lam_init=lam_init, n_hg=H_A // hgs),
        grid=(batch, H_A // hgs, nq),
        in_specs=[pl.BlockSpec(memory_space=pltpu.SMEM),
                  pl.BlockSpec((2, t, 2 * t), lambda b, h, i: (0, 0, 0)),
                  pl.BlockSpec((4, DK_A), lambda b, h, i: (0, 0)),
                  pl.BlockSpec((1, DV_A), lambda b, h, i: (0, 0)),
                  pl.BlockSpec((1, hgs, 1, LANES, t), lambda b, h, i: (b, h, i, 0, 0)),
                  pl.BlockSpec((1, seq, hgs * LANES), lambda b, h, i: (b, 0, h)),
                  pl.BlockSpec((1, hgs, nq, _DVX, t), lambda b, h, i: (b, h, 0, 0, 0))],
        out_specs=pl.BlockSpec((1, t, hgs * LANES), lambda b, h, i: (b, i, h)),
        out_shape=jax.ShapeDtypeStruct((batch, seq, w), BF16),
        scratch_shapes=[pltpu.VMEM((hgs, 2, t, 2 * t), F32),
                        pltpu.VMEM((hgs, 1, 2 * t), F32),
                        pltpu.VMEM((hgs, _DVX, 2 * t), F32)],
        compiler_params=_cparams(("arbitrary", "arbitrary", "arbitrary")),
        name="prompt_attn",
    )(rel_bias, ib, lam_vec, subln_g.reshape(1, DV_A), qt, kb.reshape(batch, seq, w), vt)
    return out.reshape(batch * seq, w)


_RH = SUBLANES


_KPAD = LANES // H_A


def _sample_attn_body(pt_ref, rb_ref, ibp_ref, ibn_ref, lv_ref, sg_ref, q_ref, kn_ref, vn_ref, *rest,
                      pps, nsteps, lam_init):
    k_refs = rest[0:pps]
    v_refs = rest[pps:2 * pps]
    o_ref = rest[2 * pps]
    biasp_ref, biasn_ref, m_ref, l_ref, acc_ref = rest[2 * pps + 1:]
    b = pl.program_id(0)
    s_id = pl.program_id(1)

    t_new = o_ref.shape[1]

    @pl.when(jnp.logical_and(b == 0, s_id == 0))
    def _():
        for h in range(H_A):
            sl = slice(h * _RH, (h + 1) * _RH)
            far_bias = rb_ref[N_BUCKETS - 1, h]
            for o in range(2):
                idx = ibp_ref[o, sl, :]
                biasp_ref[o, sl, :] = jnp.where(idx >= 0, (_bias_from_buckets(idx, rb_ref, h) - far_bias) * _LOG2E,
                                                NEG)
            idx = ibn_ref[sl, :]
            biasn_ref[sl, :] = jnp.where(idx >= 0, (_bias_from_buckets(idx, rb_ref, h) - far_bias) * _LOG2E, NEG)

    @pl.when(s_id == 0)
    def _():
        m_ref[...] = jnp.full(m_ref.shape, NEG, F32)
        l_ref[...] = jnp.zeros(l_ref.shape, F32)
        acc_ref[...] = jnp.zeros(acc_ref.shape, F32)

    q = q_ref[0]
    row = lax.broadcasted_iota(jnp.int32, (_RH, LANES), 0)
    lane = lax.broadcasted_iota(jnp.int32, (_RH, LANES), 1)
    keep = (row < t_new) == (lane < DK_A)
    q2 = jnp.concatenate([jnp.where(keep, q[:, h * LANES:(h + 1) * LANES], 0.0) for h in range(H_A)],
                         axis=0).astype(BF16)

    def attend(kt, vt, bias):
        s = lax.dot_general(q2, kt, _NT, preferred_element_type=F32) + bias
        m_old = m_ref[...]
        m_new = jnp.maximum(m_old, jnp.max(s, axis=-1, keepdims=True))
        alpha = jnp.exp2(m_old - m_new)
        p = jnp.exp2(s - m_new)
        l_ref[...] = alpha * l_ref[...] + jnp.sum(p, axis=-1, keepdims=True)
        acc_ref[...] = alpha * acc_ref[...] + jnp.dot(p.astype(BF16), vt, preferred_element_type=F32)
        m_ref[...] = m_new

    kt = jnp.concatenate([r[0, 0].astype(BF16) for r in k_refs], axis=0)
    vt = jnp.concatenate([r[0, 0].astype(BF16) for r in v_refs], axis=0)
    last = s_id == nsteps - 1
    bias = jnp.concatenate([biasp_ref[0]] * (pps - 1) + [biasp_ref[jnp.where(last, 1, 0)]], axis=1)
    attend(kt, vt, bias)

    @pl.when(last)
    def _():
        attend(kn_ref[0].astype(BF16), vn_ref[0].astype(BF16), biasn_ref[...])
        o2 = acc_ref[...] / l_ref[...]
        lam = _lam(lv_ref, lam_init)
        for h in range(H_A):
            x = o2[h * _RH:(h + 1) * _RH]
            o = x - lam * pltpu.roll(x, t_new, 0)
            ms = jnp.mean(o * o, axis=-1, keepdims=True)
            y = ((o * lax.rsqrt(ms + EPS)) * sg_ref[...]) * (1.0 - lam_init)
            o_ref[0, :, h * LANES:(h + 1) * LANES] = y[0:t_new]


def sample_attn(qn, kf, vf, cache_k, cache_v, layer, page_table, rel_bias, lam_vec, subln_g, lam_init, db, t_new):
    w = H_A * DV_A
    depth, n_pool, page = cache_k.shape[:3]
    n_pages = page_table.shape[1]
    assert 2 * t_new == _RH and t_new <= _KPAD and page >= MAX_DIST
    pps = _pick(n_pages, (16, 8, 4, 2, 1))
    nsteps = n_pages // pps
    rows = H_A * _RH
    kw = page * H_A
    rh = (np.arange(rows) // _RH)[:, None]
    rt = (np.arange(rows) % t_new)[:, None]
    ck_key, ck_head = (np.arange(kw) // H_A)[None, :], (np.arange(kw) % H_A)[None, :]
    ib_far = np.where(rh == ck_head, N_BUCKETS - 1, -1)
    ib_last = np.where(rh == ck_head, _t5_bucket_np(page + rt - ck_key), -1)
    nk_key, nk_head = (np.arange(_KPAD * H_A) // H_A)[None, :], (np.arange(_KPAD * H_A) % H_A)[None, :]
    d_new = rt - nk_key
    ib_new = np.where((rh == nk_head) & (d_new >= 0) & (nk_key < t_new), _t5_bucket_np(d_new), -1)
    ibp = jnp.asarray(np.stack([ib_far, ib_last]), jnp.int32)
    ibn = jnp.asarray(ib_new, jnp.int32)
    assert int(_t5_bucket_np(np.array([page + 1]))[0]) == N_BUCKETS - 1

    q3 = qn.reshape(db, t_new, w)
    q_pad = jnp.concatenate([q3, q3], axis=1)

    def new_rows(x):
        return jnp.pad(x.reshape(db, t_new * H_A, LANES), ((0, 0), (0, (_KPAD - t_new) * H_A), (0, 0)))

    ck = cache_k.reshape(depth, n_pool, kw, LANES)
    cv = cache_v.reshape(depth, n_pool, kw, LANES)
    new_spec = pl.BlockSpec((1, _KPAD * H_A, LANES), lambda b, s, pt: (b, 0, 0))

    def page_spec(i):
        return pl.BlockSpec((1, 1, kw, LANES), lambda b, s, pt, i=i: (layer, pt[b, s * pps + i], 0, 0))

    grid_spec = pltpu.PrefetchScalarGridSpec(
        num_scalar_prefetch=1,
        grid=(db, nsteps),
        in_specs=[pl.BlockSpec(memory_space=pltpu.SMEM),
                  pl.BlockSpec((2, rows, kw), lambda b, s, pt: (0, 0, 0)),
                  pl.BlockSpec((rows, _KPAD * H_A), lambda b, s, pt: (0, 0)),
                  pl.BlockSpec((4, DK_A), lambda b, s, pt: (0, 0)),
                  pl.BlockSpec((1, DV_A), lambda b, s, pt: (0, 0)),
                  pl.BlockSpec((1, _RH, w), lambda b, s, pt: (b, 0, 0)),
                  new_spec, new_spec]
                 + [page_spec(i) for i in range(pps)] + [page_spec(i) for i in range(pps)],
        out_specs=pl.BlockSpec((1, t_new, w), lambda b, s, pt: (b, 0, 0)),
        scratch_shapes=[pltpu.VMEM((2, rows, kw), F32),
                        pltpu.VMEM((rows, _KPAD * H_A), F32),
                        pltpu.VMEM((rows, 1), F32),
                        pltpu.VMEM((rows, 1), F32),
                        pltpu.VMEM((rows, DV_A), F32)],
    )
    out = pl.pallas_call(
        functools.partial(_sample_attn_body, pps=pps, nsteps=nsteps, lam_init=lam_init),
        grid_spec=grid_spec,
        out_shape=jax.ShapeDtypeStruct((db, t_new, w), F32),
        compiler_params=_cparams(("arbitrary", "arbitrary")),
        name="sample_attn",
    )(page_table, rel_bias, ibp, ibn, lam_vec, subln_g.reshape(1, DV_A), q_pad, new_rows(kf), new_rows(vf),
      *([ck] * pps), *([cv] * pps))
    return out.reshape(db * t_new, w).astype(BF16)


def _log_gamma(h):
    return float(np.log1p(-np.exp2(np.float32(-5.0 - h), dtype=np.float32), dtype=np.float32))


def _swap_halves(x):
    n = x.shape[1]
    half = DK_B // 2
    lane = lax.broadcasted_iota(jnp.int32, x.shape, 1)
    return jnp.where((lane % DK_B) < half, pltpu.roll(x, n - half, 1), pltpu.roll(x, half, 1))


def _rotary(x, cos, sin):
    return x * cos + _swap_halves(x) * sin


def _ret_tile(q, k, v, c):
    r = q.shape[0]
    shift = int(math.log2(c))
    assert 1 << shift == c
    ri = lax.broadcasted_iota(jnp.int32, (r, r), 0)
    ci = lax.broadcasted_iota(jnp.int32, (r, r), 1)
    same = (ri >> shift) == (ci >> shift)
    diff = ((ri & (c - 1)) - (ci & (c - 1))).astype(F32)
    valid = jnp.logical_and(same, diff >= 0.0)
    ti = (lax.broadcasted_iota(jnp.int32, (r, 1), 0) & (c - 1)).astype(F32)
    lane = lax.broadcasted_iota(jnp.int32, (r, LANES), 1)
    first = lane < DK_B
    o_inner, q_m, dec_q, k_dec = [], [], [], []
    for p in range(H_B // 2):
        ps = slice(p * LANES, (p + 1) * LANES)
        qp, kp = q[:, ps], k[:, ps]
        kpb = kp.astype(BF16)
        lg0, lg1 = _log_gamma(2 * p), _log_gamma(2 * p + 1)
        k_dec.append((kp * jnp.where(first, jnp.exp((c - 1.0 - ti) * lg0), jnp.exp((c - 1.0 - ti) * lg1))).astype(BF16))
        for hh in range(2):
            h = 2 * p + hh
            lg = lg1 if hh else lg0
            qm = jnp.where(first if hh == 0 else jnp.logical_not(first), qp, 0.0).astype(BF16)
            dmat = jnp.where(valid, jnp.exp(jnp.maximum(diff, 0.0) * lg), 0.0)
            inner = lax.dot_general(qm, kpb, _NT, preferred_element_type=F32) * dmat
            vh = v[:, h * DV_B:(h + 1) * DV_B].astype(BF16)
            o_inner.append(jnp.dot(inner.astype(BF16), vh, preferred_element_type=F32))
            q_m.append(qm)
            dec_q.append(jnp.exp((ti + 1.0) * lg))
    return o_inner, q_m, dec_q, k_dec


def _state_update(s_pair, kd, v, p, c):
    row = lax.broadcasted_iota(jnp.int32, (LANES, DV_B), 0)
    top = row < DK_B
    u0 = lax.dot_general(kd, v[:, (2 * p) * DV_B:(2 * p + 1) * DV_B].astype(BF16), _TN, preferred_element_type=F32)
    u1 = lax.dot_general(kd, v[:, (2 * p + 1) * DV_B:(2 * p + 2) * DV_B].astype(BF16), _TN, preferred_element_type=F32)
    gc = jnp.where(top, math.exp(c * _log_gamma(2 * p)), math.exp(c * _log_gamma(2 * p + 1)))
    return s_pair * gc + jnp.where(top, u0, u1)


def _ret_epilogue(o, g, ng):
    ms = jnp.mean(o * o, axis=-1, keepdims=True)
    return ((o * lax.rsqrt(ms + EPS)) * ng) * _silu(g)


def _ret_prompt_body(q_ref, k_ref, v_ref, g_ref, cos_ref, sin_ref, ng_ref, y_ref, so_ref, s_ref, *, c):
    ci = pl.program_id(1)

    @pl.when(ci == 0)
    def _():
        s_ref[...] = jnp.zeros(s_ref.shape, F32)

    ng = ng_ref[...]
    for ch in range(q_ref.shape[0] // c):
        rs = slice(ch * c, (ch + 1) * c)
        cos, sin = cos_ref[rs, :], sin_ref[rs, :]
        q = _rotary(q_ref[rs, :], cos, sin)
        k = _rotary(k_ref[rs, :], cos, sin) * (DK_B ** -0.5)
        v = v_ref[rs, :]
        o_inner, q_m, dec_q, k_dec = _ret_tile(q, k, v, c)
        for p in range(H_B // 2):
            s_pair = s_ref[p]
            sb = s_pair.astype(BF16)
            for hh in range(2):
                h = 2 * p + hh
                o = o_inner[h] + jnp.dot(q_m[h], sb, preferred_element_type=F32) * dec_q[h]
                hs = slice(h * DV_B, (h + 1) * DV_B)
                y_ref[rs, hs] = _ret_epilogue(o, g_ref[rs, hs], ng).astype(BF16)
            s_ref[p] = _state_update(s_pair, k_dec[p], v, p, c)

    @pl.when(ci == pl.num_programs(1) - 1)
    def _():
        so_ref[0] = s_ref[...]


def retention_prompt(proj, cos, sin, ret_norm_g, batch, seq, col0):
    n = batch * seq
    c = 128 if seq % 128 == 0 else seq
    rows = c * _pick(seq // c, (4, 2, 1))
    nc = seq // rows
    wq = H_B * DK_B
    wv = H_B * DV_B
    assert col0 % wv == 0
    qc, vc = col0 // wq, col0 // wv
    row = lambda b, i: b * nc + i
    y, st = pl.pallas_call(
        functools.partial(_ret_prompt_body, c=c),
        grid=(batch, nc),
        in_specs=[pl.BlockSpec((rows, wq), lambda b, i: (row(b, i), qc)),
                  pl.BlockSpec((rows, wq), lambda b, i: (row(b, i), qc + 1)),
                  pl.BlockSpec((rows, wv), lambda b, i: (row(b, i), vc + 1)),
                  pl.BlockSpec((rows, wv), lambda b, i: (row(b, i), vc + 2)),
                  pl.BlockSpec((rows, wq), lambda b, i: (i, 0)),
                  pl.BlockSpec((rows, wq), lambda b, i: (i, 0)),
                  pl.BlockSpec((1, DV_B), lambda b, i: (0, 0))],
        out_specs=[pl.BlockSpec((rows, wv), lambda b, i: (row(b, i), 0)),
                   pl.BlockSpec((1, H_B // 2, LANES, DV_B), lambda b, i: (b, 0, 0, 0))],
        out_shape=[jax.ShapeDtypeStruct((n, wv), BF16),
                   jax.ShapeDtypeStruct((batch, H_B // 2, LANES, DV_B), F32)],
        scratch_shapes=[pltpu.VMEM((H_B // 2, LANES, DV_B), F32)],
        compiler_params=_cparams(("arbitrary", "arbitrary")),
        name="retention_prompt",
    )(proj, proj, proj, proj, cos, sin, ret_norm_g.reshape(1, DV_B))
    return y, st.reshape(batch, H_B, DK_B, DV_B)


def _ret_sample_body(q_ref, k_ref, v_ref, g_ref, cos_ref, sin_ref, ng_ref, s0_ref, y_ref, so_ref,
                     o_ref, qm_ref, kd_ref, *, c):
    gi = pl.program_id(0)
    r = q_ref.shape[0]
    shift = int(math.log2(c))
    v = v_ref[...]

    @pl.when(gi == 0)
    def _():
        cos, sin = cos_ref[...], sin_ref[...]
        q = _rotary(q_ref[...], cos, sin)
        k = _rotary(k_ref[...], cos, sin) * (DK_B ** -0.5)
        o_inner, q_m, dec_q, k_dec = _ret_tile(q, k, v, c)
        for h in range(H_B):
            o_ref[:, h * DV_B:(h + 1) * DV_B] = o_inner[h]
            qm_ref[h] = q_m[h]
        for p in range(H_B // 2):
            kd_ref[p] = k_dec[p]

    rsel = (lax.broadcasted_iota(jnp.int32, (r, LANES), 0) >> shift) == gi
    ti = (lax.broadcasted_iota(jnp.int32, (r, 1), 0) & (c - 1)).astype(F32)
    zero = jnp.zeros((r, LANES), BF16)
    for p in range(H_B // 2):
        s_pair = s0_ref[0, p]
        sb = s_pair.astype(BF16)
        for hh in range(2):
            h = 2 * p + hh
            qm = jnp.where(rsel, qm_ref[h], zero)
            hs = slice(h * DV_B, (h + 1) * DV_B)
            o_ref[:, hs] += jnp.dot(qm, sb, preferred_element_type=F32) * jnp.exp((ti + 1.0) * _log_gamma(h))
        kd = jnp.where(rsel, kd_ref[p], zero)
        so_ref[0, p] = _state_update(s_pair, kd, v, p, c)

    @pl.when(gi == pl.num_programs(0) - 1)
    def _():
        ng = ng_ref[...]
        for h in range(H_B):
            hs = slice(h * DV_B, (h + 1) * DV_B)
            y_ref[:, hs] = _ret_epilogue(o_ref[:, hs], g_ref[:, hs], ng).astype(BF16)


def retention_sample(proj, cos, sin, ret_norm_g, state0, db, t_new, col0):
    n = db * t_new
    wq = H_B * DK_B
    wv = H_B * DV_B
    qc, vc = col0 // wq, col0 // wv
    s0 = state0.reshape(db, H_B // 2, LANES, DV_B)
    st_spec = pl.BlockSpec((1, H_B // 2, LANES, DV_B), lambda g: (g, 0, 0, 0))
    y, st = pl.pallas_call(
        functools.partial(_ret_sample_body, c=t_new),
        grid=(db,),
        in_specs=[pl.BlockSpec((n, wq), lambda g: (0, qc)),
                  pl.BlockSpec((n, wq), lambda g: (0, qc + 1)),
                  pl.BlockSpec((n, wv), lambda g: (0, vc + 1)),
                  pl.BlockSpec((n, wv), lambda g: (0, vc + 2)),
                  pl.BlockSpec((n, wq), lambda g: (0, 0)),
                  pl.BlockSpec((n, wq), lambda g: (0, 0)),
                  pl.BlockSpec((1, DV_B), lambda g: (0, 0)),
                  st_spec],
        out_specs=[pl.BlockSpec((n, wv), lambda g: (0, 0)), st_spec],
        out_shape=[jax.ShapeDtypeStruct((n, wv), BF16),
                   jax.ShapeDtypeStruct((db, H_B // 2, LANES, DV_B), F32)],
        scratch_shapes=[pltpu.VMEM((n, wv), F32),
                        pltpu.VMEM((H_B, n, LANES), BF16),
                        pltpu.VMEM((H_B // 2, n, LANES), BF16)],
        compiler_params=_cparams(("arbitrary",)),
        name="retention_sample",
    )(proj, proj, proj, proj, cos, sin, ret_norm_g.reshape(1, DV_B), s0)
    return y, st.reshape(db, H_B, DK_B, DV_B)


def _rotary_tables(pos):
    half = DK_B // 2
    inv = 1.0 / (10000.0 ** (jnp.arange(half, dtype=F32) / half))
    ang = pos.astype(F32)[:, None] * inv[None, :]
    cos, sin = jnp.cos(ang), jnp.sin(ang)
    cos_t = jnp.tile(jnp.concatenate([cos, cos], axis=-1), (1, H_B))
    sin_t = jnp.tile(jnp.concatenate([-sin, sin], axis=-1), (1, H_B))
    return cos_t, sin_t


_HALO = 32
_CONV_ROWS = 64
_SEQS_PER_STEP = 8


def _conv_body(a_ref, b_ref, st_ref, w_ref, cb_ref, lg_ref, lb_ref, y_ref, so_ref, full_ref, sh_ref, uc_ref, *, ts):
    for q in range(a_ref.shape[0]):
        _conv_sequence(a_ref.at[q], b_ref.at[q], st_ref.at[q], w_ref, cb_ref, lg_ref, lb_ref,
                       y_ref.at[q], so_ref.at[q], full_ref, sh_ref, uc_ref, ts)


def _conv_sequence(a_ref, b_ref, st_ref, w_ref, cb_ref, lg_ref, lb_ref, y_ref, so_ref, full_ref, sh_ref, uc_ref, ts):
    ti = pl.program_id(1)
    lo = _HALO - (K_CONV - 1)

    @pl.when(ti == 0)
    def _():
        full_ref[0:lo, :] = jnp.zeros((lo, full_ref.shape[1]), F32)
        full_ref[lo:_HALO, :] = st_ref[...]

    full_ref[_HALO:_HALO + ts, :] = a_ref[...] * jax.nn.sigmoid(b_ref[...])
    nsh = sh_ref.shape[1]
    for s in range(1, SUBLANES):
        sh_ref[s - 1] = full_ref[s:s + nsh, :]
    rb = min(ts, _CONV_ROWS)
    for c in range(full_ref.shape[1] // LANES):
        cs = slice(c * LANES, (c + 1) * LANES)
        for r0 in range(0, ts, rb):
            acc = None
            for j in range(K_CONV):
                s = (lo + j) % SUBLANES
                a0 = lo + j - s + r0
                win = full_ref[a0:a0 + rb, cs] if s == 0 else sh_ref[s - 1, a0:a0 + rb, cs]
                term = win * w_ref[j:j + 1, cs]
                acc = term if acc is None else acc + term
            uc_ref[r0:r0 + rb, cs] = acc + cb_ref[:, cs]
    uc = uc_ref[...]
    mu = jnp.mean(uc, axis=-1, keepdims=True)
    dev = uc - mu
    var = jnp.mean(dev * dev, axis=-1, keepdims=True)
    y = (dev * lax.rsqrt(var + EPS)) * lg_ref[...] + lb_ref[...]
    y_ref[...] = _silu(y).astype(BF16)
    so_ref[...] = full_ref[lo + ts:_HALO + ts, :]
    if ts >= _HALO:
        full_ref[0:_HALO, :] = full_ref[ts:ts + _HALO, :]


def conv_module(proj3, state, conv_w, conv_b, ln_g, ln_b, col0):
    b, s, _ = proj3.shape
    c = conv_w.shape[1]
    ts = _pick(s, (256, 128)) if s >= 128 else s
    nt = s // ts
    assert nt == 1 or ts >= _HALO
    cc = col0 // c
    gb = _pick(b, (_SEQS_PER_STEP, 1)) if ts < LANES else 1
    vec = lambda: pl.BlockSpec((1, c), lambda i, t: (0, 0))
    return pl.pallas_call(
        functools.partial(_conv_body, ts=ts),
        grid=(b // gb, nt),
        in_specs=[pl.BlockSpec((gb, ts, c), lambda i, t: (i, t, cc)),
                  pl.BlockSpec((gb, ts, c), lambda i, t: (i, t, cc + 1)),
                  pl.BlockSpec((gb, K_CONV - 1, c), lambda i, t: (i, 0, 0)),
                  pl.BlockSpec((K_CONV, c), lambda i, t: (0, 0)),
                  vec(), vec(), vec()],
        out_specs=[pl.BlockSpec((gb, ts, c), lambda i, t: (i, t, 0)),
                   pl.BlockSpec((gb, K_CONV - 1, c), lambda i, t: (i, 0, 0))],
        out_shape=[jax.ShapeDtypeStruct((b, s, c), BF16),
                   jax.ShapeDtypeStruct((b, K_CONV - 1, c), F32)],
        scratch_shapes=[pltpu.VMEM((_HALO + ts, c), F32),
                        pltpu.VMEM((SUBLANES - 1, _HALO + ts - SUBLANES, c), F32),
                        pltpu.VMEM((ts, c), F32)],
        compiler_params=_cparams(("arbitrary", "arbitrary")),
        name="conv_module",
    )(proj3, proj3, state, conv_w, conv_b.reshape(1, c), ln_g.reshape(1, c), ln_b.reshape(1, c))


def _merge_body(a_ref, b_ref, c_ref, wa_ref, wb_ref, wc_ref, ga_ref, gb_ref, gc_ref,
                ba_ref, bb_ref, bc_ref, o_ref, *copies, emit):
    ws = [wa_ref[...], wb_ref[...], wc_ref[...]]
    if emit:
        ws = [w.astype(BF16) for w in ws]
        for copy_ref, w in zip(copies, ws):
            copy_ref[...] = w
    ya = jnp.dot(a_ref[...], ws[0], preferred_element_type=F32)
    m = jax.nn.sigmoid(ga_ref[...] + ba_ref[...]) * ya
    yb = jnp.dot(b_ref[...], ws[1], preferred_element_type=F32)
    m = m + jax.nn.sigmoid(gb_ref[...] + bb_ref[...]) * yb
    yc = jnp.dot(c_ref[...], ws[2], preferred_element_type=F32)
    m = m + jax.nn.sigmoid(gc_ref[...] + bc_ref[...]) * yc
    o_ref[...] = m.astype(BF16)


def branch_merge(a, b, c, wa, wb, wc, proj, b_gate, col0):
    n, kd = a.shape
    (wa_arr, emit, l, _, d), (wb_arr, _, _, _, _), (wc_arr, _, _, _, _) = (_weight_form(w) for w in (wa, wb, wc))
    tm = _pick(n, (512, 256, 128))
    tn = _pick(d, (1024, 512, 256, 128))
    nj = d // tn
    g0 = col0 // tn
    bg = b_gate.reshape(1, 3 * d)
    assert not emit or n == tm
    x_spec = lambda: pl.BlockSpec((tm, kd), lambda j, i: (i, 0))
    w_spec = lambda: (pl.BlockSpec((None, kd, tn), lambda j, i: (l, 0, j)) if emit
                      else pl.BlockSpec((kd, tn), lambda j, i: (0, j)))
    g_spec = lambda k: pl.BlockSpec((tm, tn), lambda j, i, k=k: (i, g0 + k * nj + j))
    bias_spec = lambda k: pl.BlockSpec((1, tn), lambda j, i, k=k: (0, k * nj + j))
    out_specs = [pl.BlockSpec((tm, tn), lambda j, i: (i, j))]
    out_shape = [jax.ShapeDtypeStruct((n, d), BF16)]
    if emit:
        out_specs += [pl.BlockSpec((kd, tn), lambda j, i: (0, j))] * 3
        out_shape += [jax.ShapeDtypeStruct((kd, d), BF16)] * 3
    outs = pl.pallas_call(
        functools.partial(_merge_body, emit=emit),
        grid=(nj, n // tm),
        in_specs=[x_spec(), x_spec(), x_spec(), w_spec(), w_spec(), w_spec(),
                  g_spec(0), g_spec(1), g_spec(2), bias_spec(0), bias_spec(1), bias_spec(2)],
        out_specs=out_specs,
        out_shape=out_shape,
        compiler_params=_cparams(("parallel", "arbitrary")),
        name="branch_merge",
    )(a, b, c, wa_arr, wb_arr, wc_arr, proj, proj, proj, bg, bg, bg)
    return tuple(outs) if emit else outs[0]


def _ffn_act_body(g_ref, u_ref, st_ref, w_ref, cb_ref, y_ref, so_ref, full_ref, *, ts):
    ti = pl.program_id(2)
    lo = SUBLANES - (K_FFN - 1)
    for q in range(g_ref.shape[0]):
        @pl.when(ti == 0)
        def _():
            full_ref[0:lo, :] = jnp.zeros((lo, full_ref.shape[1]), F32)
            full_ref[lo:SUBLANES, :] = st_ref[q]

        full_ref[SUBLANES:SUBLANES + ts, :] = g_ref[q]
        acc = jnp.zeros((ts, full_ref.shape[1]), F32)
        for j in range(K_FFN):
            acc = acc + full_ref[lo + j:lo + j + ts, :] * w_ref[j:j + 1, :]
        y_ref[q] = (_silu(acc + cb_ref[...]) * u_ref[q]).astype(BF16)
        so_ref[q] = full_ref[lo + ts:SUBLANES + ts, :]
        if ts >= SUBLANES:
            full_ref[0:SUBLANES, :] = full_ref[ts:ts + SUBLANES, :]


def ffn_act(h3, state, conv_w, conv_b):
    b, s, f2 = h3.shape
    f = f2 // 2
    ts = _pick(s, (512, 256, 128)) if s >= 128 else s
    tc = _pick(f, (512, 256, 128)) if ts >= 128 else f
    nc = f // tc
    nt = s // ts
    assert nt == 1 or ts >= SUBLANES
    gb = _pick(b, (_SEQS_PER_STEP, 1)) if ts < LANES else 1
    return pl.pallas_call(
        functools.partial(_ffn_act_body, ts=ts),
        grid=(b // gb, nc, nt),
        in_specs=[pl.BlockSpec((gb, ts, tc), lambda i, c, t: (i, t, c)),
                  pl.BlockSpec((gb, ts, tc), lambda i, c, t: (i, t, nc + c)),
                  pl.BlockSpec((gb, K_FFN - 1, tc), lambda i, c, t: (i, 0, c)),
                  pl.BlockSpec((K_FFN, tc), lambda i, c, t: (0, c)),
                  pl.BlockSpec((1, tc), lambda i, c, t: (0, c))],
        out_specs=[pl.BlockSpec((gb, ts, tc), lambda i, c, t: (i, t, c)),
                   pl.BlockSpec((gb, K_FFN - 1, tc), lambda i, c, t: (i, 0, c))],
        out_shape=[jax.ShapeDtypeStruct((b, s, f), BF16),
                   jax.ShapeDtypeStruct((b, K_FFN - 1, f), F32)],
        scratch_shapes=[pltpu.VMEM((SUBLANES + ts, tc), F32)],
        compiler_params=_cparams(("arbitrary", "arbitrary", "arbitrary")),
        name="ffn_act",
    )(h3, h3, state, conv_w, conv_b.reshape(1, f))


def _layer(x, batch, seq, pos, attn_fn, ret_fn, conv0, ffn0, lw, mw, fuse_ffn):
    n, d = x.shape
    a_cols = H_A * (4 * DK_A + DV_A)
    b_cols = H_B * (2 * DK_B + 2 * DV_B)
    c_conv = lw['conv_w'].shape[1]
    emit = isinstance(mw['w_in'], tuple)
    copies = {}

    def take(name, res):
        if not emit:
            return res
        copies[name] = res[1]
        return res[0]

    proj = take('w_in', norm_matmul(x, lw['norm1_g'], mw['w_in']))
    a_in, kf, vf = attn_fn(proj)
    cos, sin = _rotary_tables(pos)
    b_in, ret_new = ret_fn(proj, cos, sin, a_cols)
    c_in, conv_new = conv_module(proj.reshape(batch, seq, -1), conv0, lw['conv_w'], lw['conv_b'],
                                 lw['conv_ln_g'], lw['conv_ln_b'], a_cols + b_cols)
    c_in = c_in.reshape(n, c_conv)
    merged = branch_merge(a_in, b_in, c_in, mw['w_o_a'], mw['w_o_b'], mw['w_o_c'], proj, lw['b_gate'],
                          a_cols + b_cols + 2 * c_conv)
    if emit:
        m, copies['w_o_a'], copies['w_o_b'], copies['w_o_c'] = merged
    else:
        m = merged
    x = take('w_out', matmul_residual(m, mw['w_out'], x))
    if fuse_ffn:
        act, ffn_new = ffn_up_prompt(x, lw['norm2_g'], mw['w_ffn_in'], ffn0, lw['ffn_conv_w'], lw['ffn_conv_b'],
                                     batch, seq)
    else:
        h2 = take('w_ffn_in', norm_matmul(x, lw['norm2_g'], mw['w_ffn_in']))
        act, ffn_new = ffn_act(h2.reshape(batch, seq, -1), ffn0, lw['ffn_conv_w'], lw['ffn_conv_b'])
        act = act.reshape(n, -1)
    x = take('w_ffn_down', matmul_residual(act, mw['w_ffn_down'], x))
    states = (kf.reshape(batch, seq, H_A, 2 * DK_A), vf.reshape(batch, seq, H_A, DV_A), ret_new, conv_new, ffn_new)
    return x, states, copies


def kernel(x_prompt, x_sample, cache_k, cache_v, state_ret, state_conv, state_ffn, page_table, rel_bias, norm1_g, w_in, b_gate, q_norm_g, k_norm_g, lam_vec, subln_g, w_o_a, ret_norm_g, w_o_b, conv_w, conv_b, conv_ln_g, conv_ln_b, w_o_c, w_out, norm2_g, w_ffn_in, ffn_conv_w, ffn_conv_b, w_ffn_down):
    bp, sp, d = x_prompt.shape
    db, t_new, _ = x_sample.shape
    depth = w_in.shape[0]
    past = page_table.shape[1] * cache_k.shape[2]
    pos_p = jnp.arange(sp)
    pos_s = jnp.tile(past + jnp.arange(t_new), db)
    yp = x_prompt.reshape(bp * sp, d)
    ys = x_sample.reshape(db * t_new, d)
    c_conv = conv_w.shape[2]
    d_ff = ffn_conv_w.shape[2]
    zero_conv = jnp.zeros((bp, K_CONV - 1, c_conv), F32)
    zero_ffn = jnp.zeros((bp, K_FFN - 1, d_ff), F32)
    sts_p, sts_s = [], []
    stacked = {'w_in': w_in, 'w_o_a': w_o_a, 'w_o_b': w_o_b, 'w_o_c': w_o_c, 'w_out': w_out,
               'w_ffn_in': w_ffn_in, 'w_ffn_down': w_ffn_down}
    for l in range(depth):
        lw = {'norm1_g': norm1_g[l], 'b_gate': b_gate[l], 'q_norm_g': q_norm_g[l], 'k_norm_g': k_norm_g[l],
              'conv_w': conv_w[l], 'conv_b': conv_b[l], 'conv_ln_g': conv_ln_g[l], 'conv_ln_b': conv_ln_b[l],
              'norm2_g': norm2_g[l], 'ffn_conv_w': ffn_conv_w[l], 'ffn_conv_b': ffn_conv_b[l]}
        lam_init = 0.8 - 0.6 * math.exp(-0.3 * l)

        def s_attn(proj, l=l, lam_init=lam_init, lw=lw):
            qn, kf, vf = attn_prep_sample(proj, lw['q_norm_g'], lw['k_norm_g'])
            return sample_attn(qn, kf, vf, cache_k, cache_v, l, page_table, rel_bias, lam_vec[l], subln_g[l],
                               lam_init, db, t_new), kf, vf

        def s_ret(proj, cos, sin, col0, l=l):
            return retention_sample(proj, cos, sin, ret_norm_g[l], state_ret[l], db, t_new, col0)

        ys, st, copies = _layer(ys, db, t_new, pos_s, s_attn, s_ret, state_conv[l], state_ffn[l], lw,
                                {name: (arr, l) for name, arr in stacked.items()}, False)
        sts_s.append(st)

        def p_attn(proj, l=l, lam_init=lam_init, lw=lw):
            t = _pick(sp, (_ATTN_TILE, LANES))
            qt, kf, kb, vf, vt = attn_prep_prompt(proj, lw['q_norm_g'], lw['k_norm_g'], bp, sp, t)
            return prompt_attn(qt, kb, vt, rel_bias, lam_vec[l], subln_g[l], lam_init, bp, sp, t), kf, vf

        def p_ret(proj, cos, sin, col0, l=l):
            return retention_prompt(proj, cos, sin, ret_norm_g[l], bp, sp, col0)

        yp, st, _ = _layer(yp, bp, sp, pos_p, p_attn, p_ret, zero_conv, zero_ffn, lw, copies, True)
        sts_p.append(st)

    def stk(sts, i):
        return jnp.stack([st[i] for st in sts], axis=0)

    return (yp.reshape(bp, sp, d), ys.reshape(db, t_new, d),
            stk(sts_p, 0), stk(sts_p, 1), stk(sts_p, 2), stk(sts_p, 3), stk(sts_p, 4),
            stk(sts_s, 0), stk(sts_s, 1), stk(sts_s, 2), stk(sts_s, 3), stk(sts_s, 4))
```
